```python
import jax, jax.numpy as jnp
from jax import lax
import numpy as np

D_MODEL = 1024
BATCH = 2
SEQ = 8192
DEPTH = 4

GRID_W = 64
CTX_LEN = 256
D_A = 1024
CONV_W = 3
B_HEADS = 16
B_HEAD = 64
D_B = B_HEADS * B_HEAD
R_W = 64
R_A = 64
R_G = 128
GN_EPS = 64e-5
C_HEADS = 16
C_KV = 4
C_GROUP = C_HEADS // C_KV
HEAD_DIM = 64
WINDOW = 128
ATT_BLOCK = 128
ATT_SCALE = HEAD_DIM ** -0.5
ROPE_THETA = 10000.0
NEG_INF = -1e30
D_FF = 2816
NORM_EPS = 1e-6
N_BRANCH = 3
IN_SPLITS = (D_A, D_A, D_A, 3 * D_B, 2 * R_W, 2 * R_A, R_G,
             C_HEADS * HEAD_DIM, C_KV * HEAD_DIM, C_KV * HEAD_DIM, N_BRANCH * D_MODEL)
N_IN = sum(IN_SPLITS)

kernel_name = "hybrid_conv_rwkv7_swa_diffusion_trunk"


def rmsnorm(x, g):
    xf = x.astype(jnp.float32)
    y = xf * lax.rsqrt(jnp.mean(xf * xf, axis=-1, keepdims=True) + NORM_EPS)
    return (y * g.astype(jnp.float32)).astype(x.dtype)


def modulate(x, g, shift, scale):
    return rmsnorm(x, g) * (1.0 + scale) + shift


def dwconv3(x, w):
    xp = jnp.pad(x, ((0, 0), (1, 1), (0, 0)))
    return xp[:, :-2] * w[0] + xp[:, 1:-1] * w[1] + xp[:, 2:] * w[2]


def split_cols(p):
    offsets = tuple(int(o) for o in np.cumsum(IN_SPLITS)[:-1])
    return jnp.split(p, offsets, axis=-1)


def axial_rope_tables(seq, dtype):
    rows = seq // GRID_W
    t_row = jnp.broadcast_to(jnp.arange(rows)[:, None], (rows, GRID_W)).reshape(-1).astype(jnp.float32)
    t_col = jnp.broadcast_to(jnp.arange(GRID_W)[None, :], (rows, GRID_W)).reshape(-1).astype(jnp.float32)
    n_freq = HEAD_DIM // 4
    inv = ROPE_THETA ** (-jnp.arange(n_freq, dtype=jnp.float32) / n_freq)
    ang = jnp.stack([t_row[:, None] * inv, t_col[:, None] * inv], axis=1)
    return jnp.cos(ang).astype(dtype), jnp.sin(ang).astype(dtype)


def apply_rope2d(x, cos, sin):
    shp = x.shape
    xr = x.reshape(shp[:-1] + (2, 2, HEAD_DIM // 4))
    x1, x2 = xr[..., 0, :], xr[..., 1, :]
    c, s = cos[:, None], sin[:, None]
    out = jnp.stack([x1 * c - x2 * s, x1 * s + x2 * c], axis=-2)
    return out.reshape(shp)


def windowed_attention(q, k, v, kc, vc, sink):
    b, s = q.shape[:2]
    nb = s // ATT_BLOCK
    qb = q.reshape(b, nb, ATT_BLOCK, C_KV, C_GROUP, HEAD_DIM)

    def band(t):
        tp = jnp.pad(t, ((0, 0), (ATT_BLOCK, ATT_BLOCK), (0, 0), (0, 0)))
        tp = tp.reshape(b, nb + 2, ATT_BLOCK, C_KV, HEAD_DIM)
        return jnp.concatenate([tp[:, :-2], tp[:, 1:-1], tp[:, 2:]], axis=2)

    kb, vb = band(k), band(v)
    s_loc = jnp.einsum('bnqkgd,bnskd->bnkgqs', qb, kb).astype(jnp.float32)
    s_ctx = jnp.einsum('bnqkgd,bckd->bnkgqc', qb, kc).astype(jnp.float32)
    qi = jnp.arange(ATT_BLOCK)
    si = jnp.arange(3 * ATT_BLOCK)
    rel = si[None, :] - ATT_BLOCK - qi[:, None]
    kpos = jnp.arange(nb)[:, None, None] * ATT_BLOCK + si[None, None, :] - ATT_BLOCK
    mask = (jnp.abs(rel) <= WINDOW)[None] & (kpos >= 0) & (kpos < s)
    s_loc = jnp.where(mask[None, :, None, None], s_loc, NEG_INF)
    sink_b = jnp.broadcast_to(sink.reshape(C_KV, C_GROUP, 1, 1).astype(jnp.float32), s_ctx.shape[:-1] + (1,))
    probs = jax.nn.softmax(jnp.concatenate([s_loc, s_ctx, sink_b], axis=-1), axis=-1).astype(v.dtype)
    n_loc, n_ctx = 3 * ATT_BLOCK, kc.shape[1]
    o = (jnp.einsum('bnkgqs,bnskd->bnqkgd', probs[..., :n_loc], vb)
         + jnp.einsum('bnkgqc,bckd->bnqkgd', probs[..., n_loc:n_loc + n_ctx], vc))
    return o.reshape(b, s, C_HEADS * HEAD_DIM)


def context_attention(qc, kc, vc, sink):
    b, c = qc.shape[:2]
    qg = qc.reshape(b, c, C_KV, C_GROUP, HEAD_DIM)
    sc = jnp.einsum('bqkgd,bskd->bkgqs', qg, kc).astype(jnp.float32)
    sink_b = jnp.broadcast_to(sink.reshape(C_KV, C_GROUP, 1, 1).astype(jnp.float32), sc.shape[:-1] + (1,))
    probs = jax.nn.softmax(jnp.concatenate([sc, sink_b], axis=-1), axis=-1).astype(vc.dtype)
    o = jnp.einsum('bkgqs,bskd->bqkgd', probs[..., :c], vc)
    return o.reshape(b, c, C_HEADS * HEAD_DIM)


def rwkv_inputs(rkv, wl, al, gl, p):
    b, t = rkv.shape[:2]
    f = jnp.float32
    mu = p['rwkv_mu']
    shift_w = jnp.stack([mu[0], 1.0 - mu[0] - mu[1], mu[1]])
    r, k, v = jnp.split(dwconv3(rkv, shift_w), 3, axis=-1)
    wl = wl.reshape(b, t, 2, R_W)
    al = al.reshape(b, t, 2, R_A)
    w_pre = p['rwkv_w0'] + jnp.einsum('btdr,drc->btdc', jnp.tanh(wl), p['rwkv_w2'])
    w_log = -jax.nn.softplus(-w_pre.astype(f)) - 0.5
    decay = jnp.exp(-jnp.exp(w_log))
    a = jax.nn.sigmoid((p['rwkv_a0'] + jnp.einsum('btdr,drc->btdc', al, p['rwkv_a2'])).astype(f))
    g = jax.nn.sigmoid(gl) @ p['rwkv_g2']

    def heads(z):
        return z.reshape(z.shape[:-1] + (B_HEADS, B_HEAD))

    kk = heads((k * p['rwkv_k_k']).astype(f))
    kk = kk * lax.rsqrt(jnp.sum(kk * kk, axis=-1, keepdims=True) + 1e-12)
    k_d = heads(k.astype(f)[:, :, None] * (1.0 + (a - 1.0) * p['rwkv_k_a'].astype(f)))
    b_d = kk[:, :, None] * heads(a)
    return heads(r.astype(f)), heads(v.astype(f)), kk, heads(decay), k_d, b_d, g


def rwkv_scan(s0, r, v, kk, decay, k_d, b_d, emit):
    b, t = r.shape[:2]

    def both(z):
        return jnp.broadcast_to(z[:, :, None], (b, t, 2) + z.shape[2:])

    def to_dirs(z):
        return jnp.stack([z[:, :, 0], jnp.flip(z[:, :, 1], axis=1)], axis=0).transpose(2, 0, 1, 3, 4)

    xs = tuple(to_dirs(z) for z in (both(r), decay, k_d, both(v), both(-kk), b_d))

    def step(S, inp):
        rt, wt, kt, vt, at, bt = inp
        sa = jnp.einsum('dbhvk,dbhk->dbhv', S, at)
        S = S * wt[..., None, :] + sa[..., :, None] * bt[..., None, :] + vt[..., :, None] * kt[..., None, :]
        y = jnp.einsum('dbhvk,dbhk->dbhv', S, rt) if emit else None
        return S, y

    s_fin, ys = lax.scan(step, s0, xs)
    if not emit:
        return s_fin, None
    y = ys[:, 0] + jnp.flip(ys[:, 1], axis=0)
    return s_fin, y.transpose(1, 0, 2, 3)


def rwkv_branch(y, r, v, k_d, g, p, dtype):
    b, t = y.shape[:2]
    mean = jnp.mean(y, axis=-1, keepdims=True)
    var = jnp.mean(jnp.square(y - mean), axis=-1, keepdims=True)
    gn = ((y - mean) * lax.rsqrt(var + GN_EPS)).reshape(b, t, D_B) * p['rwkv_ln_g'] + p['rwkv_ln_b']
    bonus = jnp.sum(jnp.sum(r[:, :, None] * k_d * p['rwkv_r_k'], axis=-1, keepdims=True) * v[:, :, None], axis=2)
    return ((gn + bonus.reshape(b, t, D_B)) * g).astype(dtype)


def short_conv(bg, cg, xin, conv_w):
    return bg * dwconv3(cg * xin, conv_w)


def merge_branches(ya, yb, yc, gates, p):
    ga, gb, gc = jnp.split(jax.nn.sigmoid(gates), N_BRANCH, axis=-1)
    m = ga * (ya @ p['a_out_w']) + gb * (yb @ p['rwkv_out_w']) + gc * (yc @ p['attn_out_w'])
    return m @ p['w_o']


def conv_ffn(h, p):
    u = dwconv3(h @ p['ffn_up'], p['ffn_conv'])
    ug, uv = jnp.split(u, 2, axis=-1)
    return (jax.nn.silu(ug) * uv) @ p['ffn_down']


def trunk_layer(xc, xl, c, c_ctx, cos, sin, p, ctx_out):
    b = xl.shape[0]
    dt = xl.dtype
    mod_l = jnp.split((jax.nn.silu(c) @ p['ada_w'] + p['ada_b'])[:, None, :], 6, axis=-1)
    mod_c = jnp.split(jax.nn.silu(c_ctx) @ p['ada_w'] + p['ada_b'], 6, axis=-1)
    hl = modulate(xl, p['norm1_g'], mod_l[0], mod_l[1])
    hc = modulate(xc, p['norm1_g'], mod_c[0], mod_c[1])
    (ab_l, ac_l, ax_l, rkv_l, wl_l, al_l, gl_l, q_l, k_l, v_l, gate_l) = split_cols(hl @ p['w_in'])
    (ab_c, ac_c, ax_c, rkv_c, wl_c, al_c, gl_c, q_c, k_c, v_c, gate_c) = split_cols(hc @ p['w_in'])

    def heads_q(z):
        return z.reshape(z.shape[:2] + (C_HEADS, HEAD_DIM))

    def heads_kv(z):
        return z.reshape(z.shape[:2] + (C_KV, HEAD_DIM))

    kc, vc = heads_kv(k_c), heads_kv(v_c)
    ql = apply_rope2d(heads_q(q_l), cos, sin) * ATT_SCALE
    kl = apply_rope2d(heads_kv(k_l), cos, sin)
    att_l = windowed_attention(ql, kl, heads_kv(v_l), kc, vc, p['attn_sink'])

    r_c, vr_c, kk_c, dec_c, kd_c, bd_c, g_c = rwkv_inputs(rkv_c, wl_c, al_c, gl_c, p)
    s0 = jnp.zeros((2, b, B_HEADS, B_HEAD, B_HEAD), jnp.float32)
    s_ctx, y_rc = rwkv_scan(s0, r_c, vr_c, kk_c, dec_c, kd_c, bd_c, ctx_out)
    r_l, vr_l, kk_l, dec_l, kd_l, bd_l, g_l = rwkv_inputs(rkv_l, wl_l, al_l, gl_l, p)
    _, y_rl = rwkv_scan(s_ctx, r_l, vr_l, kk_l, dec_l, kd_l, bd_l, True)
    rwkv_l = rwkv_branch(y_rl, r_l, vr_l, kd_l, g_l, p, dt)

    conv_l = short_conv(ab_l, ac_l, ax_l, p['conv_a_w'])

    xl = xl + mod_l[2] * merge_branches(conv_l, rwkv_l, att_l, gate_l, p)
    xl = xl + mod_l[5] * conv_ffn(modulate(xl, p['norm2_g'], mod_l[3], mod_l[4]), p)

    if ctx_out:
        att_c = context_attention(heads_q(q_c) * ATT_SCALE, kc, vc, p['attn_sink'])
        rwkv_c = rwkv_branch(y_rc, r_c, vr_c, kd_c, g_c, p, dt)
        conv_c = short_conv(ab_c, ac_c, ax_c, p['conv_a_w'])
        xc = xc + mod_c[2] * merge_branches(conv_c, rwkv_c, att_c, gate_c, p)
        xc = xc + mod_c[5] * conv_ffn(modulate(xc, p['norm2_g'], mod_c[3], mod_c[4]), p)
    else:
        xc = None
    return xc, xl


def setup_inputs(seed: int = 0) -> dict:
    key = jax.random.key(seed)
    ks = iter(jax.random.split(key, 40))
    f = jnp.float32
    L = DEPTH

    def nrm(shape, scale):
        return jax.random.normal(next(ks), shape, f) * scale

    def gain(shape):
        return 1.0 + nrm(shape, 0.05)

    return {
        'x': nrm((BATCH, SEQ, D_MODEL), 1.0),
        'c': nrm((BATCH, D_MODEL), 1.0),
        'ctx': nrm((BATCH, CTX_LEN, D_MODEL), 1.0),
        'c_ctx': nrm((D_MODEL,), 1.0),
        'ada_w': nrm((L, D_MODEL, 6 * D_MODEL), 0.5 * D_MODEL ** -0.5),
        'ada_b': nrm((L, 6 * D_MODEL), 0.02),
        'norm1_g': gain((L, D_MODEL)),
        'w_in': nrm((L, D_MODEL, N_IN), D_MODEL ** -0.5),
        'conv_a_w': nrm((L, CONV_W, D_A), CONV_W ** -0.5),
        'a_out_w': nrm((L, D_A, D_MODEL), D_A ** -0.5),
        'rwkv_mu': jax.random.uniform(next(ks), (L, 2, 3 * D_B), f, 0.0, 0.4),
        'rwkv_w0': jax.random.uniform(next(ks), (L, 2, D_B), f, -6.0, -1.0),
        'rwkv_w2': nrm((L, 2, R_W, D_B), 0.5 * R_W ** -0.5),
        'rwkv_a0': nrm((L, 2, D_B), 0.1),
        'rwkv_a2': nrm((L, 2, R_A, D_B), R_A ** -0.5),
        'rwkv_g2': nrm((L, R_G, D_B), R_G ** -0.5),
        'rwkv_k_k': 0.85 + nrm((L, D_B), 0.05),
        'rwkv_k_a': gain((L, D_B)),
        'rwkv_r_k': nrm((L, B_HEADS, B_HEAD), 0.1),
        'rwkv_ln_g': gain((L, D_B)),
        'rwkv_ln_b': nrm((L, D_B), 0.02),
        'rwkv_out_w': nrm((L, D_B, D_MODEL), D_B ** -0.5),
        'attn_sink': nrm((L, C_HEADS), 0.5),
        'attn_out_w': nrm((L, C_HEADS * HEAD_DIM, D_MODEL), (C_HEADS * HEAD_DIM) ** -0.5),
        'w_o': nrm((L, D_MODEL, D_MODEL), D_MODEL ** -0.5),
        'norm2_g': gain((L, D_MODEL)),
        'ffn_up': nrm((L, D_MODEL, 2 * D_FF), D_MODEL ** -0.5),
        'ffn_conv': nrm((L, CONV_W, 2 * D_FF), CONV_W ** -0.5),
        'ffn_down': nrm((L, D_FF, D_MODEL), D_FF ** -0.5),
        'final_norm_g': gain((D_MODEL,)),
    }


def reference(x, c, ctx, c_ctx, ada_w, ada_b, norm1_g, w_in, conv_a_w, a_out_w, rwkv_mu, rwkv_w0, rwkv_w2,
              rwkv_a0, rwkv_a2, rwkv_g2, rwkv_k_k, rwkv_k_a, rwkv_r_k, rwkv_ln_g, rwkv_ln_b, rwkv_out_w,
              attn_sink, attn_out_w, w_o, norm2_g, ffn_up, ffn_conv, ffn_down, final_norm_g):
    cos, sin = axial_rope_tables(x.shape[1], x.dtype)
    xl, xc = x, ctx
    for i in range(DEPTH):
        p = dict(ada_w=ada_w[i], ada_b=ada_b[i], norm1_g=norm1_g[i], w_in=w_in[i], conv_a_w=conv_a_w[i],
                 a_out_w=a_out_w[i], rwkv_mu=rwkv_mu[i], rwkv_w0=rwkv_w0[i], rwkv_w2=rwkv_w2[i],
                 rwkv_a0=rwkv_a0[i], rwkv_a2=rwkv_a2[i], rwkv_g2=rwkv_g2[i], rwkv_k_k=rwkv_k_k[i],
                 rwkv_k_a=rwkv_k_a[i], rwkv_r_k=rwkv_r_k[i], rwkv_ln_g=rwkv_ln_g[i], rwkv_ln_b=rwkv_ln_b[i],
                 rwkv_out_w=rwkv_out_w[i], attn_sink=attn_sink[i], attn_out_w=attn_out_w[i], w_o=w_o[i],
                 norm2_g=norm2_g[i], ffn_up=ffn_up[i], ffn_conv=ffn_conv[i], ffn_down=ffn_down[i])
        xc, xl = trunk_layer(xc, xl, c, c_ctx, cos, sin, p, i < DEPTH - 1)
    return rmsnorm(xl, final_norm_g)
```

```python
import functools

import jax
import jax.numpy as jnp
import numpy as np
from jax import lax
from jax.experimental import pallas as pl
from jax.experimental.pallas import tpu as pltpu

D_MODEL = 1024
GRID_W = 64
N_HEADS = 16
HEAD_DIM = 64
N_KV = 4
R_LORA = 64
R_GATE = 128
GN_EPS = 64e-5
WINDOW = 128
ATT_BLOCK = 128
ATT_SCALE = HEAD_DIM ** -0.5
ROPE_THETA = 10000.0
NEG_INF = -1e30
D_FF = 2816
NORM_EPS = 1e-6
KK_EPS = 1e-12

LANES = 128
HALO = 8
N_PAIR = D_MODEL // LANES
CHUNK = 64
N_LEVELS = 5
VMEM_LIMIT = 52 * 1024 * 1024

BF16 = jnp.bfloat16
F32 = jnp.float32


def _cparams(sem):
    return pltpu.CompilerParams(dimension_semantics=sem, vmem_limit_bytes=VMEM_LIMIT)


def _dot(a, b):
    return jnp.dot(a.astype(BF16), b.astype(BF16), preferred_element_type=F32)


def _dot_nt(a, b):
    return lax.dot_general(a.astype(BF16), b.astype(BF16), (((1,), (1,)), ((), ())),
                           preferred_element_type=F32)


def _dot_tn(a, b):
    return lax.dot_general(a.astype(BF16), b.astype(BF16), (((0,), (0,)), ((), ())),
                           preferred_element_type=F32)


def _sigmoid(x):
    return 1.0 / (1.0 + jnp.exp(-x))


def _row_ids(i, tm):
    return i * tm + lax.broadcasted_iota(jnp.int32, (tm, 1), 0)


def _mod_row(mod_ref, k, is_ctx):
    return jnp.where(is_ctx, mod_ref[0, k:k + 1, :], mod_ref[1, k:k + 1, :])


def _neighbours(z, before, after, gr, tm, ctx, total):
    row = lax.broadcasted_iota(jnp.int32, (tm, 1), 0)
    prev = jnp.where(row == 0, before, pltpu.roll(z, 1, 0))
    nxt = jnp.where(row == tm - 1, after, pltpu.roll(z, tm - 1, 0))
    first = jnp.where(gr == 0, 1, 0) + jnp.where(gr == ctx, 1, 0)
    last = jnp.where(gr == ctx - 1, 1, 0) + jnp.where(gr == total - 1, 1, 0)
    prev = jnp.where(first > 0, 0.0, prev)
    nxt = jnp.where(last > 0, 0.0, nxt)
    return prev, nxt


def _halo_specs(width, tm, total, col=0):
    per = tm // HALO
    nblk = total // HALO
    before = pl.BlockSpec((None, HALO, width), lambda b, i: (b, jnp.maximum(i * per - 1, 0), col))
    after = pl.BlockSpec((None, HALO, width), lambda b, i: (b, jnp.minimum((i + 1) * per, nblk - 1), col))
    return before, after


def _ada_kernel(c_ref, w_ref, b_ref, o_ref):
    c = c_ref[...]
    o_ref[...] = _dot(c * _sigmoid(c), w_ref[...]) + b_ref[...]


def _ada(cvec, ada_w, ada_b):
    depth, d, n = ada_w.shape
    tn = 1536
    return pl.pallas_call(
        _ada_kernel,
        grid=(depth, n // tn),
        in_specs=[pl.BlockSpec((HALO, d), lambda l, j: (0, 0)),
                  pl.BlockSpec((None, d, tn), lambda l, j: (l, 0, j)),
                  pl.BlockSpec((None, 1, tn), lambda l, j: (l, 0, j))],
        out_specs=pl.BlockSpec((None, HALO, tn), lambda l, j: (l, 0, j)),
        out_shape=jax.ShapeDtypeStruct((depth, HALO, n), F32),
        compiler_params=_cparams(("arbitrary", "arbitrary")),
    )(cvec, ada_w, ada_b.reshape(depth, 1, n))


def _nmm_kernel(x_ref, mod_ref, g_ref, w_ref, o_ref, *, k_shift, k_scale, tm, ctx):
    i = pl.program_id(2)
    x = x_ref[...]
    y = x * lax.rsqrt(jnp.mean(x * x, axis=-1, keepdims=True) + NORM_EPS) * g_ref[...]
    is_ctx = _row_ids(i, tm) < ctx
    h = y * (1.0 + _mod_row(mod_ref, k_scale, is_ctx)) + _mod_row(mod_ref, k_shift, is_ctx)
    o_ref[...] = _dot(h, w_ref[...])


def _nmm(x, mod, g, w, *, k_shift, k_scale, ctx, tm, tn):
    bsz, total, d = x.shape
    n = w.shape[1]
    kern = functools.partial(_nmm_kernel, k_shift=k_shift, k_scale=k_scale, tm=tm, ctx=ctx)
    return pl.pallas_call(
        kern,
        grid=(n // tn, bsz, total // tm),
        in_specs=[pl.BlockSpec((None, tm, d), lambda j, b, i: (b, i, 0)),
                  pl.BlockSpec((None, 2, 6, d), lambda j, b, i: (b, 0, 0, 0)),
                  pl.BlockSpec((1, d), lambda j, b, i: (0, 0)),
                  pl.BlockSpec((d, tn), lambda j, b, i: (0, j))],
        out_specs=pl.BlockSpec((None, tm, tn), lambda j, b, i: (b, i, j)),
        out_shape=jax.ShapeDtypeStruct((bsz, total, n), F32),
        compiler_params=_cparams(("arbitrary", "arbitrary", "arbitrary")),
    )(x, mod, g, w)


def _head_sums(z, lane_lo):
    s_lo = jnp.sum(jnp.where(lane_lo, z, 0.0), axis=-1, keepdims=True)
    s_hi = jnp.sum(jnp.where(lane_lo, 0.0, z), axis=-1, keepdims=True)
    return jnp.where(lane_lo, s_lo, s_hi)


def _prep_kernel(rkv_ref, before_ref, after_ref, lora_ref, mu_ref, w0_ref, w2_ref, a0_ref, a2_ref, g2_ref,
                 kk_ref, ka_ref, rk_ref,
                 r_out, v_out, kn_out, lw_out, kd_out, bd_out, bonus_out, g_out, *, tm, ctx, total):
    i = pl.program_id(1)
    gr = _row_ids(i, tm)
    x = rkv_ref[...]
    prev, nxt = _neighbours(x, before_ref[HALO - 1:HALO, :], after_ref[0:1, :], gr, tm, ctx, total)
    mu_prev = mu_ref[0:1, :]
    mu_next = mu_ref[1:2, :]
    s = prev * mu_prev + x * (1.0 - mu_prev - mu_next) + nxt * mu_next
    d = D_MODEL
    r, k, v = s[:, :d], s[:, d:2 * d], s[:, 2 * d:]

    lo = lora_ref[...]
    w_pre = w0_ref[...] + _dot(jnp.tanh(lo[:, :LANES]), w2_ref[...])
    w_log = -(jnp.maximum(-w_pre, 0.0) + jnp.log1p(jnp.exp(-jnp.abs(w_pre)))) - 0.5
    lw = -jnp.exp(w_log)
    a = _sigmoid(a0_ref[...] + _dot(lo[:, LANES:2 * LANES], a2_ref[...]))
    g = _dot(_sigmoid(lo[:, 2 * LANES:]), g2_ref[...])
    kk = k * kk_ref[...]
    ka = ka_ref[...]
    rk = rk_ref[...]

    lane_lo = lax.broadcasted_iota(jnp.int32, (1, LANES), 1) < HEAD_DIM
    for p in range(N_PAIR):
        sl = slice(LANES * p, LANES * (p + 1))
        kkp = kk[:, sl]
        kn = kkp * lax.rsqrt(_head_sums(kkp * kkp, lane_lo) + KK_EPS)
        rp, kp, vp = r[:, sl], k[:, sl], v[:, sl]
        bonus = jnp.zeros_like(rp)
        for dr in range(2):
            dsl = slice(d * dr + LANES * p, d * dr + LANES * (p + 1))
            ad = a[:, dsl]
            kd = kp * (1.0 + (ad - 1.0) * ka[:, sl])
            bonus = bonus + _head_sums(rp * kd * rk[:, sl], lane_lo) * vp
            lw_out[dr, p] = lw[:, dsl]
            kd_out[dr, p] = kd
            bd_out[dr, p] = kn * ad
        r_out[p] = rp
        v_out[p] = vp
        kn_out[p] = kn
        bonus_out[p] = bonus
        g_out[p] = g[:, sl]


def _prep(rkv, lora, mu, w0, w2bd, a0, a2bd, g2, k_k, k_a, r_k, *, ctx, tm):
    bsz, total, _ = rkv.shape
    d = D_MODEL
    kern = functools.partial(_prep_kernel, tm=tm, ctx=ctx, total=total)
    before, after = _halo_specs(3 * d, tm, total)
    full = lambda shp: pl.BlockSpec(shp, lambda b, i: tuple(0 for _ in shp))
    pm = pl.BlockSpec((None, N_PAIR, tm, LANES), lambda b, i: (b, 0, i, 0))
    pm2 = pl.BlockSpec((2, None, N_PAIR, tm, LANES), lambda b, i: (0, b, 0, i, 0))
    sds = jax.ShapeDtypeStruct((bsz, N_PAIR, total, LANES), F32)
    sds2 = jax.ShapeDtypeStruct((2, bsz, N_PAIR, total, LANES), F32)
    return pl.pallas_call(
        kern,
        grid=(bsz, total // tm),
        in_specs=[pl.BlockSpec((None, tm, 3 * d), lambda b, i: (b, i, 0)), before, after,
                  pl.BlockSpec((None, tm, 3 * LANES), lambda b, i: (b, i, 0)),
                  full((2, 3 * d)), full((1, 2 * d)), full((LANES, 2 * d)), full((1, 2 * d)),
                  full((LANES, 2 * d)), full((R_GATE, d)), full((1, d)), full((1, d)), full((1, d))],
        out_specs=[pm, pm, pm, pm2, pm2, pm2, pm, pm],
        out_shape=[sds, sds, sds, sds2, sds2, sds2, sds, sds],
        compiler_params=_cparams(("arbitrary", "arbitrary")),
    )(rkv, rkv, rkv, lora, mu, w0, w2bd, a0, a2bd, g2, k_k, k_a, r_k)


M_STRICT, M_INCL, M_EYE, M_LEVEL0 = 0, 1, 2, 3
N_MASKS = M_LEVEL0 + 1 + N_LEVELS


def _scan_masks():
    c = CHUNK
    t = np.arange(c)[:, None]
    s = np.arange(c)[None, :]
    cum = np.zeros((2, 2 * c, c), np.float32)
    msk = np.zeros((2, N_MASKS, c, 2 * c), np.float32)
    for dr in range(2):
        earlier = (s < t) if dr == 0 else (s > t)
        cum[dr, :c] = earlier | (s == t)
        cum[dr, c:] = earlier
        planes = [earlier, earlier | (s == t), s == t]
        m = 1
        while m < c:
            same = (t // (2 * m)) == (s // (2 * m))
            t_late = (t % (2 * m)) >= m
            s_late = (s % (2 * m)) >= m
            planes.append(same & (t_late & ~s_late if dr == 0 else ~t_late & s_late))
            m *= 2
        for q, plane in enumerate(planes):
            msk[dr, q] = np.concatenate([plane, plane], axis=1)
    bdm = np.kron(np.eye(2, dtype=np.float32), np.ones((c, c), np.float32))
    return jnp.asarray(cum), jnp.asarray(msk), jnp.asarray(bdm)


def _scan_kernel(cum_ref, msk_ref, bdm_ref, r_ref, v_ref, kn_ref, lw_ref, kd_ref, bd_ref, y_ref, s_ref):
    c = CHUNK

    @pl.when(pl.program_id(2) == 0)
    def _():
        s_ref[...] = jnp.zeros_like(s_ref)

    cum = cum_ref[...]
    bdm = bdm_ref[...]
    m_strict = msk_ref[M_STRICT]
    m_incl = msk_ref[M_INCL]

    def stack(z):
        return jnp.concatenate([z, z], axis=0) * bdm

    for p in range(N_PAIR):
        lw = lw_ref[p]
        cl = jnp.dot(cum, lw, precision=lax.Precision.HIGHEST, preferred_element_type=F32)
        cl_incl, cl_excl = cl[:c], cl[c:]
        g_end = jnp.exp(jnp.sum(lw, axis=0, keepdims=True))
        inv_g = jnp.exp(-cl_incl)
        at = -kn_ref[p] * jnp.exp(cl_excl)
        rt = r_ref[p] * jnp.exp(cl_incl)
        bt = bd_ref[p] * inv_g
        kt = kd_ref[p] * inv_g
        v = v_ref[p]
        state = s_ref[p]

        gram = _dot_nt(jnp.concatenate([at, rt], axis=0),
                       jnp.concatenate([stack(bt), stack(kt)], axis=0))
        g_ab, g_ak = gram[:c, :2 * c], gram[:c, 2 * c:]
        g_rb, g_rk = gram[c:, :2 * c], gram[c:, 2 * c:]

        tri = msk_ref[M_EYE] + g_ab * msk_ref[M_LEVEL0]
        for lvl in range(1, N_LEVELS + 1):
            w = _dot(g_ab * msk_ref[M_LEVEL0 + lvl], stack(tri))
            tri = tri + _dot(tri, stack(w))

        sv = stack(v)
        x = _dot_nt(at, state) + _dot(g_ak * m_strict, sv)
        u = _dot(tri, stack(x))
        y_ref[p] = _dot_nt(rt, state) + _dot(g_rb * m_incl, stack(u)) + _dot(g_rk * m_incl, sv)
        upd = _dot_tn(jnp.concatenate([u, v], axis=0), jnp.concatenate([bt * g_end, kt * g_end], axis=0))
        s_ref[p] = state * g_end + upd * bdm


def _scan(r, v, kn, lw, kd, bd, masks, *, ctx):
    bsz, _, total, _ = r.shape
    c = CHUNK
    nc, ncc = total // c, ctx // c
    cum, msk, bdm = masks

    def chunk(dr, j):
        rev = jnp.where(j < ncc, ncc - 1 - j, nc + ncc - 1 - j)
        return jnp.where(dr == 0, j, rev)

    shared = pl.BlockSpec((None, N_PAIR, c, LANES), lambda dr, b, j: (b, 0, chunk(dr, j), 0))
    per_dir = pl.BlockSpec((None, None, N_PAIR, c, LANES), lambda dr, b, j: (dr, b, 0, chunk(dr, j), 0))
    return pl.pallas_call(
        _scan_kernel,
        grid=(2, bsz, nc),
        in_specs=[pl.BlockSpec((None, 2 * c, c), lambda dr, b, j: (dr, 0, 0)),
                  pl.BlockSpec((None, N_MASKS, c, 2 * c), lambda dr, b, j: (dr, 0, 0, 0)),
                  pl.BlockSpec((2 * c, 2 * c), lambda dr, b, j: (0, 0)),
                  shared, shared, shared, per_dir, per_dir, per_dir],
        out_specs=per_dir,
        out_shape=jax.ShapeDtypeStruct((2, bsz, N_PAIR, total, LANES), F32),
        scratch_shapes=[pltpu.VMEM((N_PAIR, 2 * HEAD_DIM, 2 * HEAD_DIM), F32)],
        compiler_params=_cparams(("arbitrary", "arbitrary", "arbitrary")),
    )(cum, msk, bdm, r, v, kn, lw, kd, bd)


def _rope(z, cos, sin):
    lane = lax.broadcasted_iota(jnp.int32, (1, LANES), 1)
    first = jnp.bitwise_and(lane, HEAD_DIM // 2 - 1) < (HEAD_DIM // 4)
    out = []
    for p in range(z.shape[1] // LANES):
        zp = z[:, LANES * p:LANES * (p + 1)]
        partner = jnp.where(first, pltpu.roll(zp, LANES - HEAD_DIM // 4, 1), pltpu.roll(zp, HEAD_DIM // 4, 1))
        out.append(zp * cos + partner * sin)
    return jnp.concatenate(out, axis=1)


def _attn_kernel(sink_ref, q_ref, kp_ref, kc_ref, kn_ref, kx_ref, vp_ref, vc_ref, vn_ref, vx_ref,
                 cq_ref, sq_ref, cp_ref, sp_ref, cn_ref, sn_ref, o_ref, *, n_ctx_blk, n_blk):
    i = pl.program_id(1)
    blk = ATT_BLOCK
    q = _rope(q_ref[...], cq_ref[...], sq_ref[...]) * ATT_SCALE
    k_all = jnp.concatenate([_rope(kp_ref[...], cp_ref[...], sp_ref[...]),
                             _rope(kc_ref[...], cq_ref[...], sq_ref[...]),
                             _rope(kn_ref[...], cn_ref[...], sn_ref[...]),
                             kx_ref[...]], axis=0)
    v_all = jnp.concatenate([vp_ref[...], vc_ref[...], vn_ref[...], vx_ref[...]], axis=0)
    n_keys = k_all.shape[0]

    qi = lax.broadcasted_iota(jnp.int32, (blk, n_keys), 0)
    si = lax.broadcasted_iota(jnp.int32, (blk, n_keys), 1)
    rel = si - blk - qi
    key_blk = i - 1 + jnp.right_shift(si, 7)
    ok = (jnp.where(jnp.abs(rel) <= WINDOW, 1, 0) * jnp.where(key_blk >= n_ctx_blk, 1, 0)
          * jnp.where(key_blk < n_blk, 1, 0) * jnp.where(i >= n_ctx_blk, 1, 0))
    ok = jnp.where(si >= 3 * blk, 1, ok)
    bias = jnp.where(ok > 0, 0.0, NEG_INF)

    lane_lo = lax.broadcasted_iota(jnp.int32, (1, LANES), 1) < HEAD_DIM
    group = N_HEADS // N_KV
    outs = [None] * N_HEADS
    for kvh in range(N_KV):
        kv_slab = slice(LANES * (kvh // 2), LANES * (kvh // 2 + 1))
        kv_lo = kvh % 2 == 0
        rows, sinks = [], []
        for hq in range(group):
            h = kvh * group + hq
            qp = q[:, LANES * (h // 2):LANES * (h // 2 + 1)]
            if (h % 2 == 0) != kv_lo:
                qp = pltpu.roll(qp, HEAD_DIM, 1)
            rows.append(jnp.where(lane_lo, qp, 0.0) if kv_lo else jnp.where(lane_lo, 0.0, qp))
            sinks.append(jnp.full((blk, 1), sink_ref[h], F32))
        qg = jnp.concatenate(rows, axis=0)
        sink = jnp.concatenate(sinks, axis=0)
        s = _dot_nt(qg, k_all[:, kv_slab]) + jnp.concatenate([bias] * group, axis=0)
        m = jnp.maximum(jnp.max(s, axis=-1, keepdims=True), sink)
        e = jnp.exp(s - m)
        denom = jnp.sum(e, axis=-1, keepdims=True) + jnp.exp(sink - m)
        o = _dot(e, v_all[:, kv_slab]) / denom
        for hq in range(group):
            h = kvh * group + hq
            oh = o[hq * blk:(hq + 1) * blk]
            if (h % 2 == 0) != kv_lo:
                oh = pltpu.roll(oh, HEAD_DIM, 1)
            outs[h] = oh
    o_ref[...] = jnp.concatenate(
        [jnp.where(lane_lo, outs[2 * p], outs[2 * p + 1]) for p in range(N_PAIR)], axis=1)


def _attention(qkv, sink, cos, sin, *, ctx):
    bsz, total, _ = qkv.shape
    blk = ATT_BLOCK
    n_blk, n_ctx_blk = total // blk, ctx // blk
    kvw = N_KV * HEAD_DIM
    kcol, vcol = D_MODEL // kvw, D_MODEL // kvw + 1
    kern = functools.partial(_attn_kernel, n_ctx_blk=n_ctx_blk, n_blk=n_blk)
    prev_i = lambda i: jnp.maximum(i - 1, 0)
    next_i = lambda i: jnp.minimum(i + 1, n_blk - 1)

    def kv(col, row):
        return pl.BlockSpec((None, blk, kvw), lambda b, i: (b, row(i), col))

    def kv_ctx(col):
        return pl.BlockSpec((None, ctx, kvw), lambda b, i: (b, 0, col))

    def tab(row):
        return pl.BlockSpec((blk, LANES), lambda b, i: (row(i), 0))

    same = lambda i: i
    return pl.pallas_call(
        kern,
        grid=(bsz, n_blk),
        in_specs=[pl.BlockSpec(memory_space=pltpu.SMEM),
                  pl.BlockSpec((None, blk, D_MODEL), lambda b, i: (b, i, 0)),
                  kv(kcol, prev_i), kv(kcol, same), kv(kcol, next_i), kv_ctx(kcol),
                  kv(vcol, prev_i), kv(vcol, same), kv(vcol, next_i), kv_ctx(vcol),
                  tab(same), tab(same), tab(prev_i), tab(prev_i), tab(next_i), tab(next_i)],
        out_specs=pl.BlockSpec((None, blk, D_MODEL), lambda b, i: (b, i, 0)),
        out_shape=jax.ShapeDtypeStruct((bsz, total, D_MODEL), F32),
        compiler_params=_cparams(("arbitrary", "arbitrary")),
    )(sink, qkv, qkv, qkv, qkv, qkv, qkv, qkv, qkv, qkv, cos, sin, cos, sin, cos, sin)


def _merge_kernel(x_ref, mod_ref, cv_ref, before_ref, after_ref, gate_ref, att_ref, y_ref, bonus_ref, g_ref,
                  cw_ref, lng_ref, lnb_ref, wa_ref, wb_ref, wc_ref, wo_ref, o_ref, *, tm, ctx, total):
    i = pl.program_id(1)
    gr = _row_ids(i, tm)
    d = D_MODEL
    cv = cv_ref[...]
    z = cv[:, d:2 * d] * cv[:, 2 * d:]
    z_before = before_ref[HALO - 1:HALO, d:2 * d] * before_ref[HALO - 1:HALO, 2 * d:]
    z_after = after_ref[0:1, d:2 * d] * after_ref[0:1, 2 * d:]
    prev, nxt = _neighbours(z, z_before, z_after, gr, tm, ctx, total)
    cw = cw_ref[...]
    conv = cv[:, :d] * (prev * cw[0:1] + z * cw[1:2] + nxt * cw[2:3])

    lane_lo = lax.broadcasted_iota(jnp.int32, (1, LANES), 1) < HEAD_DIM
    lng = lng_ref[...]
    lnb = lnb_ref[...]
    slabs = []
    for p in range(N_PAIR):
        sl = slice(LANES * p, LANES * (p + 1))
        y = y_ref[0, p] + y_ref[1, p]
        mean = _head_sums(y, lane_lo) * (1.0 / HEAD_DIM)
        yc = y - mean
        var = _head_sums(yc * yc, lane_lo) * (1.0 / HEAD_DIM)
        gn = yc * lax.rsqrt(var + GN_EPS) * lng[:, sl] + lnb[:, sl]
        slabs.append((gn + bonus_ref[p]) * g_ref[p])
    rwkv = jnp.concatenate(slabs, axis=1)

    gates = gate_ref[...]
    m = (_sigmoid(gates[:, :d]) * _dot(conv, wa_ref[...])
         + _sigmoid(gates[:, d:2 * d]) * _dot(rwkv, wb_ref[...])
         + _sigmoid(gates[:, 2 * d:]) * _dot(att_ref[...], wc_ref[...]))
    is_ctx = gr < ctx
    o_ref[...] = x_ref[...] + _mod_row(mod_ref, 2, is_ctx) * _dot(m, wo_ref[...])


def _merge(x, mod, cv, gates, att, y, bonus, g, conv_w, ln_g, ln_b, wa, wb, wc, wo, *, ctx, tm):
    bsz, total, d = x.shape
    kern = functools.partial(_merge_kernel, tm=tm, ctx=ctx, total=total)
    before, after = _halo_specs(3 * d, tm, total)
    row = lambda w: pl.BlockSpec((None, tm, w), lambda b, i: (b, i, 0))
    full = lambda shp: pl.BlockSpec(shp, lambda b, i: tuple(0 for _ in shp))
    pm = pl.BlockSpec((None, N_PAIR, tm, LANES), lambda b, i: (b, 0, i, 0))
    pm2 = pl.BlockSpec((2, None, N_PAIR, tm, LANES), lambda b, i: (0, b, 0, i, 0))
    return pl.pallas_call(
        kern,
        grid=(bsz, total // tm),
        in_specs=[row(d), pl.BlockSpec((None, 2, 6, d), lambda b, i: (b, 0, 0, 0)),
                  row(3 * d), before, after, row(3 * d), row(d), pm2, pm, pm,
                  full((3, d)), full((1, d)), full((1, d)),
                  full((d, d)), full((d, d)), full((d, d)), full((d, d))],
        out_specs=row(d),
        out_shape=jax.ShapeDtypeStruct((bsz, total, d), F32),
        compiler_params=_cparams(("arbitrary", "arbitrary")),
    )(x, mod, cv, cv, cv, gates, att, y, bonus, g, conv_w, ln_g, ln_b, wa, wb, wc, wo)


def _ffn_down_kernel(x_ref, mod_ref, u_ref, before_ref, after_ref, cw_ref, wd_ref, o_ref, *, tm, ctx, total):
    i = pl.program_id(1)
    gr = _row_ids(i, tm)
    u = u_ref[...]
    prev, nxt = _neighbours(u, before_ref[HALO - 1:HALO, :], after_ref[0:1, :], gr, tm, ctx, total)
    cw = cw_ref[...]
    uc = prev * cw[0:1] + u * cw[1:2] + nxt * cw[2:3]
    ug, uv = uc[:, :D_FF], uc[:, D_FF:]
    act = ug * _sigmoid(ug) * uv
    o_ref[...] = x_ref[...] + _mod_row(mod_ref, 5, gr < ctx) * _dot(act, wd_ref[...])


def _ffn_down(x, mod, u, conv_w, wd, *, ctx, tm):
    bsz, total, d = x.shape
    kern = functools.partial(_ffn_down_kernel, tm=tm, ctx=ctx, total=total)
    before, after = _halo_specs(2 * D_FF, tm, total)
    row = lambda w: pl.BlockSpec((None, tm, w), lambda b, i: (b, i, 0))
    full = lambda shp: pl.BlockSpec(shp, lambda b, i: tuple(0 for _ in shp))
    return pl.pallas_call(
        kern,
        grid=(bsz, total // tm),
        in_specs=[row(d), pl.BlockSpec((None, 2, 6, d), lambda b, i: (b, 0, 0, 0)),
                  row(2 * D_FF), before, after, full((3, 2 * D_FF)), full((D_FF, d))],
        out_specs=row(d),
        out_shape=jax.ShapeDtypeStruct((bsz, total, d), F32),
        compiler_params=_cparams(("arbitrary", "arbitrary")),
    )(x, mod, u, u, u, conv_w, wd)


def _final_norm_kernel(x_ref, g_ref, o_ref):
    x = x_ref[...]
    o_ref[...] = x * lax.rsqrt(jnp.mean(x * x, axis=-1, keepdims=True) + NORM_EPS) * g_ref[...]


def _final_norm(x, g, *, ctx, seq, tm):
    bsz, _, d = x.shape
    off = ctx // tm
    return pl.pallas_call(
        _final_norm_kernel,
        grid=(bsz, seq // tm),
        in_specs=[pl.BlockSpec((None, tm, d), lambda b, i: (b, i + off, 0)),
                  pl.BlockSpec((1, d), lambda b, i: (0, 0))],
        out_specs=pl.BlockSpec((None, tm, d), lambda b, i: (b, i, 0)),
        out_shape=jax.ShapeDtypeStruct((bsz, seq, d), F32),
        compiler_params=_cparams(("arbitrary", "arbitrary")),
    )(x, g)


def _rope_tables(ctx, seq):
    rows = seq // GRID_W
    t_row = jnp.broadcast_to(jnp.arange(rows)[:, None], (rows, GRID_W)).reshape(-1).astype(F32)
    t_col = jnp.broadcast_to(jnp.arange(GRID_W)[None, :], (rows, GRID_W)).reshape(-1).astype(F32)
    n_freq = HEAD_DIM // 4
    inv = ROPE_THETA ** (-jnp.arange(n_freq, dtype=F32) / n_freq)
    ar, ac = t_row[:, None] * inv, t_col[:, None] * inv
    cos = jnp.concatenate([jnp.cos(ar), jnp.cos(ar), jnp.cos(ac), jnp.cos(ac)], axis=1)
    sin = jnp.concatenate([-jnp.sin(ar), jnp.sin(ar), -jnp.sin(ac), jnp.sin(ac)], axis=1)
    cos = jnp.concatenate([jnp.ones((ctx, HEAD_DIM), F32), cos], axis=0)
    sin = jnp.concatenate([jnp.zeros((ctx, HEAD_DIM), F32), sin], axis=0)
    return jnp.tile(cos, (1, LANES // HEAD_DIM)), jnp.tile(sin, (1, LANES // HEAD_DIM))


def _block_diag2(w):
    z = jnp.zeros_like(w[0])
    return jnp.concatenate([jnp.concatenate([w[0], z], axis=1), jnp.concatenate([z, w[1]], axis=1)], axis=0)


def kernel(x, c, ctx, c_ctx, ada_w, ada_b, norm1_g, w_in, conv_a_w, a_out_w, rwkv_mu, rwkv_w0, rwkv_w2,
           rwkv_a0, rwkv_a2, rwkv_g2, rwkv_k_k, rwkv_k_a, rwkv_r_k, rwkv_ln_g, rwkv_ln_b, rwkv_out_w,
           attn_sink, attn_out_w, w_o, norm2_g, ffn_up, ffn_conv, ffn_down, final_norm_g):
    bsz, seq, d = x.shape
    n_ctx = ctx.shape[1]
    depth = ada_w.shape[0]
    total = n_ctx + seq
    assert d == D_MODEL and seq % ATT_BLOCK == 0 and n_ctx % ATT_BLOCK == 0 and bsz + 1 <= HALO
    tm = 256
    assert total % tm == 0 and n_ctx % tm == 0

    cvec = jnp.zeros((HALO, d), F32).at[:bsz].set(c).at[bsz].set(c_ctx)
    mods = _ada(cvec, ada_w.astype(BF16), ada_b)
    mods = mods.reshape(depth, HALO, 6, d)
    mods = jnp.stack([jnp.broadcast_to(mods[:, bsz:bsz + 1], (depth, bsz, 6, d)), mods[:, :bsz]], axis=2)

    cos, sin = _rope_tables(n_ctx, seq)
    masks = _scan_masks()
    xa = jnp.concatenate([ctx, x], axis=1)

    o_rkv, o_lora, o_qkv, o_gate = 3 * d, 6 * d, 6 * d + 3 * LANES, 6 * d + 3 * LANES + d + 2 * N_KV * HEAD_DIM
    for l in range(depth):
        mod = mods[l]
        w = w_in[l].astype(BF16)
        proj = functools.partial(_nmm, xa, mod, norm1_g[l][None], k_shift=0, k_scale=1, ctx=n_ctx, tm=tm)
        cv = proj(w[:, :o_rkv], tn=3 * d)
        rkv = proj(w[:, o_rkv:o_lora], tn=3 * d)
        lora = proj(w[:, o_lora:o_qkv], tn=3 * LANES)
        qkv = proj(w[:, o_qkv:o_gate], tn=o_gate - o_qkv)
        gates = proj(w[:, o_gate:], tn=3 * d)

        r, v, kn, lw, kd, bd, bonus, g = _prep(
            rkv, lora, rwkv_mu[l], rwkv_w0[l].reshape(1, 2 * d), _block_diag2(rwkv_w2[l]).astype(BF16),
            rwkv_a0[l].reshape(1, 2 * d), _block_diag2(rwkv_a2[l]).astype(BF16), rwkv_g2[l].astype(BF16),
            rwkv_k_k[l][None], rwkv_k_a[l][None], rwkv_r_k[l].reshape(1, d), ctx=n_ctx, tm=tm)
        y = _scan(r, v, kn, lw, kd, bd, masks, ctx=n_ctx)
        att = _attention(qkv, attn_sink[l], cos, sin, ctx=n_ctx)
        xa = _merge(xa, mod, cv, gates, att, y, bonus, g, conv_a_w[l], rwkv_ln_g[l][None], rwkv_ln_b[l][None],
                    a_out_w[l].astype(BF16), rwkv_out_w[l].astype(BF16), attn_out_w[l].astype(BF16),
                    w_o[l].astype(BF16), ctx=n_ctx, tm=tm)
        u = _nmm(xa, mod, norm2_g[l][None], ffn_up[l].astype(BF16), k_shift=3, k_scale=4, ctx=n_ctx, tm=tm,
                 tn=D_FF)
        xa = _ffn_down(xa, mod, u, ffn_conv[l], ffn_down[l].astype(BF16), ctx=n_ctx, tm=tm)
    return _final_norm(xa, final_norm_g[None], ctx=n_ctx, seq=seq, tm=tm)
```

```python
import functools

import jax
import jax.numpy as jnp
import numpy as np
from jax import lax
from jax.experimental import pallas as pl
from jax.experimental.pallas import tpu as pltpu

D_MODEL = 1024
GRID_W = 64
N_HEADS = 16
HEAD_DIM = 64
N_KV = 4
R_LORA = 64
R_GATE = 128
GN_EPS = 64e-5
WINDOW = 128
ATT_BLOCK = 128
ATT_SCALE = HEAD_DIM ** -0.5
ROPE_THETA = 10000.0
NEG_INF = -1e30
D_FF = 2816
NORM_EPS = 1e-6
KK_EPS = 1e-12

LANES = 128
HALO = 8
N_PAIR = D_MODEL // LANES
CHUNK = 64
N_LEVELS = 5
VMEM_LIMIT = 52 * 1024 * 1024

BF16 = jnp.bfloat16
F32 = jnp.float32


def _cparams(sem):
    return pltpu.CompilerParams(dimension_semantics=sem, vmem_limit_bytes=VMEM_LIMIT)


def _dot(a, b):
    return jnp.dot(a.astype(BF16), b.astype(BF16), preferred_element_type=F32)


def _dot_nt(a, b):
    return lax.dot_general(a.astype(BF16), b.astype(BF16), (((1,), (1,)), ((), ())),
                           preferred_element_type=F32)


def _bmm(a, b):
    return jnp.einsum('pmk,pkn->pmn', a.astype(BF16), b.astype(BF16), preferred_element_type=F32)


def _bmm_nt(a, b):
    return jnp.einsum('pmk,pnk->pmn', a.astype(BF16), b.astype(BF16), preferred_element_type=F32)


def _bmm_tn(a, b):
    return jnp.einsum('pkm,pkn->pmn', a.astype(BF16), b.astype(BF16), preferred_element_type=F32)


def _sigmoid(x):
    return 1.0 / (1.0 + jnp.exp(-x))


def _row_ids(i, tm):
    return i * tm + lax.broadcasted_iota(jnp.int32, (tm, 1), 0)


def _mod_row(mod_ref, k, is_ctx):
    return jnp.where(is_ctx, mod_ref[0, k:k + 1, :], mod_ref[1, k:k + 1, :])


def _neighbours(z, before, after, gr, tm, ctx, total):
    row = lax.broadcasted_iota(jnp.int32, (tm, 1), 0)
    prev = jnp.where(row == 0, before, pltpu.roll(z, 1, 0))
    nxt = jnp.where(row == tm - 1, after, pltpu.roll(z, tm - 1, 0))
    first = jnp.where(gr == 0, 1, 0) + jnp.where(gr == ctx, 1, 0)
    last = jnp.where(gr == ctx - 1, 1, 0) + jnp.where(gr == total - 1, 1, 0)
    prev = jnp.where(first > 0, 0.0, prev)
    nxt = jnp.where(last > 0, 0.0, nxt)
    return prev, nxt


def _halo_specs(width, tm, total, col=0):
    per = tm // HALO
    nblk = total // HALO
    before = pl.BlockSpec((None, HALO, width), lambda b, i: (b, jnp.maximum(i * per - 1, 0), col))
    after = pl.BlockSpec((None, HALO, width), lambda b, i: (b, jnp.minimum((i + 1) * per, nblk - 1), col))
    return before, after


def _ada_kernel(c_ref, w_ref, b_ref, o_ref):
    c = c_ref[...]
    o_ref[...] = _dot(c * _sigmoid(c), w_ref[...]) + b_ref[...]


def _ada(cvec, ada_w, ada_b):
    depth, d, n = ada_w.shape
    tn = 1536
    return pl.pallas_call(
        _ada_kernel,
        grid=(depth, n // tn),
        in_specs=[pl.BlockSpec((HALO, d), lambda l, j: (0, 0)),
                  pl.BlockSpec((None, d, tn), lambda l, j: (l, 0, j)),
                  pl.BlockSpec((None, 1, tn), lambda l, j: (l, 0, j))],
        out_specs=pl.BlockSpec((None, HALO, tn), lambda l, j: (l, 0, j)),
        out_shape=jax.ShapeDtypeStruct((depth, HALO, n), F32),
        compiler_params=_cparams(("arbitrary", "arbitrary")),
    )(cvec, ada_w, ada_b.reshape(depth, 1, n))


def _nmm_kernel(x_ref, mod_ref, g_ref, w_ref, o_ref, *, k_shift, k_scale, tm, ctx):
    i = pl.program_id(2)
    x = x_ref[...]
    y = x * lax.rsqrt(jnp.mean(x * x, axis=-1, keepdims=True) + NORM_EPS) * g_ref[...]
    is_ctx = _row_ids(i, tm) < ctx
    h = y * (1.0 + _mod_row(mod_ref, k_scale, is_ctx)) + _mod_row(mod_ref, k_shift, is_ctx)
    o_ref[...] = _dot(h, w_ref[...])


def _nmm(x, mod, g, w, *, k_shift, k_scale, ctx, tm, tn):
    bsz, total, d = x.shape
    n = w.shape[1]
    kern = functools.partial(_nmm_kernel, k_shift=k_shift, k_scale=k_scale, tm=tm, ctx=ctx)
    return pl.pallas_call(
        kern,
        grid=(n // tn, bsz, total // tm),
        in_specs=[pl.BlockSpec((None, tm, d), lambda j, b, i: (b, i, 0)),
                  pl.BlockSpec((None, 2, 6, d), lambda j, b, i: (b, 0, 0, 0)),
                  pl.BlockSpec((1, d), lambda j, b, i: (0, 0)),
                  pl.BlockSpec((d, tn), lambda j, b, i: (0, j))],
        out_specs=pl.BlockSpec((None, tm, tn), lambda j, b, i: (b, i, j)),
        out_shape=jax.ShapeDtypeStruct((bsz, total, n), F32),
        compiler_params=_cparams(("arbitrary", "arbitrary", "arbitrary")),
    )(x, mod, g, w)


def _head_sums(z, lane_lo):
    s_lo = jnp.sum(jnp.where(lane_lo, z, 0.0), axis=-1, keepdims=True)
    s_hi = jnp.sum(jnp.where(lane_lo, 0.0, z), axis=-1, keepdims=True)
    return jnp.where(lane_lo, s_lo, s_hi)


def _prep_kernel(rkv_ref, before_ref, after_ref, lora_ref, mu_ref, w0_ref, w2_ref, a0_ref, a2_ref, g2_ref,
                 kk_ref, ka_ref, rk_ref,
                 r_out, v_out, kn_out, lw_out, kd_out, bd_out, bonus_out, g_out, *, tm, ctx, total):
    i = pl.program_id(1)
    gr = _row_ids(i, tm)
    x = rkv_ref[...]
    prev, nxt = _neighbours(x, before_ref[HALO - 1:HALO, :], after_ref[0:1, :], gr, tm, ctx, total)
    mu_prev = mu_ref[0:1, :]
    mu_next = mu_ref[1:2, :]
    s = prev * mu_prev + x * (1.0 - mu_prev - mu_next) + nxt * mu_next
    d = D_MODEL
    r, k, v = s[:, :d], s[:, d:2 * d], s[:, 2 * d:]

    lo = lora_ref[...]
    w_pre = w0_ref[...] + _dot(jnp.tanh(lo[:, :LANES]), w2_ref[...])
    w_log = -(jnp.maximum(-w_pre, 0.0) + jnp.log1p(jnp.exp(-jnp.abs(w_pre)))) - 0.5
    lw = -jnp.exp(w_log)
    a = _sigmoid(a0_ref[...] + _dot(lo[:, LANES:2 * LANES], a2_ref[...]))
    g = _dot(_sigmoid(lo[:, 2 * LANES:]), g2_ref[...])
    kk = k * kk_ref[...]
    ka = ka_ref[...]
    rk = rk_ref[...]

    lane_lo = lax.broadcasted_iota(jnp.int32, (1, LANES), 1) < HEAD_DIM
    for p in range(N_PAIR):
        sl = slice(LANES * p, LANES * (p + 1))
        kkp = kk[:, sl]
        kn = kkp * lax.rsqrt(_head_sums(kkp * kkp, lane_lo) + KK_EPS)
        rp, kp, vp = r[:, sl], k[:, sl], v[:, sl]
        bonus = jnp.zeros_like(rp)
        for dr in range(2):
            dsl = slice(d * dr + LANES * p, d * dr + LANES * (p + 1))
            ad = a[:, dsl]
            kd = kp * (1.0 + (ad - 1.0) * ka[:, sl])
            bonus = bonus + _head_sums(rp * kd * rk[:, sl], lane_lo) * vp
            lw_out[dr, p] = lw[:, dsl]
            kd_out[dr, p] = kd
            bd_out[dr, p] = kn * ad
        r_out[p] = rp
        v_out[p] = vp
        kn_out[p] = kn
        bonus_out[p] = bonus
        g_out[p] = g[:, sl]


def _prep(rkv, lora, mu, w0, w2bd, a0, a2bd, g2, k_k, k_a, r_k, *, ctx, tm):
    bsz, total, _ = rkv.shape
    d = D_MODEL
    kern = functools.partial(_prep_kernel, tm=tm, ctx=ctx, total=total)
    before, after = _halo_specs(3 * d, tm, total)
    full = lambda shp: pl.BlockSpec(shp, lambda b, i: tuple(0 for _ in shp))
    pm = pl.BlockSpec((None, N_PAIR, tm, LANES), lambda b, i: (b, 0, i, 0))
    pm2 = pl.BlockSpec((2, None, N_PAIR, tm, LANES), lambda b, i: (0, b, 0, i, 0))
    sds = jax.ShapeDtypeStruct((bsz, N_PAIR, total, LANES), F32)
    sds2 = jax.ShapeDtypeStruct((2, bsz, N_PAIR, total, LANES), F32)
    return pl.pallas_call(
        kern,
        grid=(bsz, total // tm),
        in_specs=[pl.BlockSpec((None, tm, 3 * d), lambda b, i: (b, i, 0)), before, after,
                  pl.BlockSpec((None, tm, 3 * LANES), lambda b, i: (b, i, 0)),
                  full((2, 3 * d)), full((1, 2 * d)), full((LANES, 2 * d)), full((1, 2 * d)),
                  full((LANES, 2 * d)), full((R_GATE, d)), full((1, d)), full((1, d)), full((1, d))],
        out_specs=[pm, pm, pm, pm2, pm2, pm2, pm, pm],
        out_shape=[sds, sds, sds, sds2, sds2, sds2, sds, sds],
        compiler_params=_cparams(("arbitrary", "arbitrary")),
    )(rkv, rkv, rkv, lora, mu, w0, w2bd, a0, a2bd, g2, k_k, k_a, r_k)


M_STRICT, M_INCL, M_EYE, M_LEVEL0 = 0, 1, 2, 3
N_MASKS = M_LEVEL0 + 1 + N_LEVELS


def _scan_masks():
    c = CHUNK
    t = np.arange(c)[:, None]
    s = np.arange(c)[None, :]
    cum = np.zeros((2, 2 * c, c), np.float32)
    msk = np.zeros((2, N_MASKS, c, 2 * c), np.float32)
    for dr in range(2):
        earlier = (s < t) if dr == 0 else (s > t)
        cum[dr, :c] = earlier | (s == t)
        cum[dr, c:] = earlier
        planes = [earlier, earlier | (s == t), s == t]
        m = 1
        while m < c:
            same = (t // (2 * m)) == (s // (2 * m))
            t_late = (t % (2 * m)) >= m
            s_late = (s % (2 * m)) >= m
            planes.append(same & (t_late & ~s_late if dr == 0 else ~t_late & s_late))
            m *= 2
        for q, plane in enumerate(planes):
            msk[dr, q] = np.concatenate([plane, plane], axis=1)
    bdm = np.kron(np.eye(2, dtype=np.float32), np.ones((c, c), np.float32))
    return jnp.asarray(cum), jnp.asarray(msk), jnp.asarray(bdm)


def _scan_kernel(cum_ref, msk_ref, bdm_ref, r_ref, v_ref, kn_ref, lw_ref, kd_ref, bd_ref, y_ref, s_ref):
    c = CHUNK

    @pl.when(pl.program_id(2) == 0)
    def _():
        s_ref[...] = jnp.zeros_like(s_ref)

    bdm = bdm_ref[...]
    m_strict = msk_ref[M_STRICT]
    m_incl = msk_ref[M_INCL]

    def stack(z):
        return jnp.concatenate([z, z], axis=1) * bdm

    lw = lw_ref[...]
    lw_wide = jnp.concatenate([lw[p] for p in range(N_PAIR)], axis=1)
    cl = jnp.dot(cum_ref[...], lw_wide, precision=lax.Precision.HIGHEST, preferred_element_type=F32)
    cl = jnp.stack([cl[:, LANES * p:LANES * (p + 1)] for p in range(N_PAIR)], axis=0)
    cl_incl, cl_excl = cl[:, :c], cl[:, c:]
    g_end = jnp.exp(jnp.sum(lw, axis=1, keepdims=True))
    inv_g = jnp.exp(-cl_incl)
    at = -kn_ref[...] * jnp.exp(cl_excl)
    rt = r_ref[...] * jnp.exp(cl_incl)
    bt = bd_ref[...] * inv_g
    kt = kd_ref[...] * inv_g
    v = v_ref[...]
    state = s_ref[...]

    gram = _bmm_nt(jnp.concatenate([at, rt], axis=1),
                   jnp.concatenate([stack(bt), stack(kt)], axis=1))
    g_ab, g_ak = gram[:, :c, :2 * c], gram[:, :c, 2 * c:]
    g_rb, g_rk = gram[:, c:, :2 * c], gram[:, c:, 2 * c:]

    tri = msk_ref[M_EYE] + g_ab * msk_ref[M_LEVEL0]
    for lvl in range(1, N_LEVELS + 1):
        w = _bmm(g_ab * msk_ref[M_LEVEL0 + lvl], stack(tri))
        tri = tri + _bmm(tri, stack(w))

    sv = stack(v)
    x = _bmm_nt(at, state) + _bmm(g_ak * m_strict, sv)
    u = _bmm(tri, stack(x))
    y_ref[...] = _bmm_nt(rt, state) + _bmm(g_rb * m_incl, stack(u)) + _bmm(g_rk * m_incl, sv)
    upd = _bmm_tn(jnp.concatenate([u, v], axis=1), jnp.concatenate([bt * g_end, kt * g_end], axis=1))
    s_ref[...] = state * g_end + upd * bdm


def _scan(r, v, kn, lw, kd, bd, masks, *, ctx):
    bsz, _, total, _ = r.shape
    c = CHUNK
    nc, ncc = total // c, ctx // c
    cum, msk, bdm = masks

    def chunk(dr, j):
        rev = jnp.where(j < ncc, ncc - 1 - j, nc + ncc - 1 - j)
        return jnp.where(dr == 0, j, rev)

    shared = pl.BlockSpec((None, N_PAIR, c, LANES), lambda dr, b, j: (b, 0, chunk(dr, j), 0))
    per_dir = pl.BlockSpec((None, None, N_PAIR, c, LANES), lambda dr, b, j: (dr, b, 0, chunk(dr, j), 0))
    return pl.pallas_call(
        _scan_kernel,
        grid=(2, bsz, nc),
        in_specs=[pl.BlockSpec((None, 2 * c, c), lambda dr, b, j: (dr, 0, 0)),
                  pl.BlockSpec((None, N_MASKS, c, 2 * c), lambda dr, b, j: (dr, 0, 0, 0)),
                  pl.BlockSpec((2 * c, 2 * c), lambda dr, b, j: (0, 0)),
                  shared, shared, shared, per_dir, per_dir, per_dir],
        out_specs=per_dir,
        out_shape=jax.ShapeDtypeStruct((2, bsz, N_PAIR, total, LANES), F32),
        scratch_shapes=[pltpu.VMEM((N_PAIR, 2 * HEAD_DIM, 2 * HEAD_DIM), F32)],
        compiler_params=_cparams(("arbitrary", "arbitrary", "arbitrary")),
    )(cum, msk, bdm, r, v, kn, lw, kd, bd)


def _rope(z, cos, sin):
    lane = lax.broadcasted_iota(jnp.int32, (1, LANES), 1)
    first = jnp.bitwise_and(lane, HEAD_DIM // 2 - 1) < (HEAD_DIM // 4)
    out = []
    for p in range(z.shape[1] // LANES):
        zp = z[:, LANES * p:LANES * (p + 1)]
        partner = jnp.where(first, pltpu.roll(zp, LANES - HEAD_DIM // 4, 1), pltpu.roll(zp, HEAD_DIM // 4, 1))
        out.append(zp * cos + partner * sin)
    return jnp.concatenate(out, axis=1)


def _attn_kernel(sink_ref, q_ref, kp_ref, kc_ref, kn_ref, kx_ref, vp_ref, vc_ref, vn_ref, vx_ref,
                 cq_ref, sq_ref, cp_ref, sp_ref, cn_ref, sn_ref, o_ref, *, n_ctx_blk, n_blk):
    i = pl.program_id(1)
    blk = ATT_BLOCK
    q = _rope(q_ref[...], cq_ref[...], sq_ref[...]) * ATT_SCALE
    k_all = jnp.concatenate([_rope(kp_ref[...], cp_ref[...], sp_ref[...]),
                             _rope(kc_ref[...], cq_ref[...], sq_ref[...]),
                             _rope(kn_ref[...], cn_ref[...], sn_ref[...]),
                             kx_ref[...]], axis=0)
    v_all = jnp.concatenate([vp_ref[...], vc_ref[...], vn_ref[...], vx_ref[...]], axis=0)
    n_keys = k_all.shape[0]

    qi = lax.broadcasted_iota(jnp.int32, (blk, n_keys), 0)
    si = lax.broadcasted_iota(jnp.int32, (blk, n_keys), 1)
    rel = si - blk - qi
    key_blk = i - 1 + jnp.right_shift(si, 7)
    ok = (jnp.where(jnp.abs(rel) <= WINDOW, 1, 0) * jnp.where(key_blk >= n_ctx_blk, 1, 0)
          * jnp.where(key_blk < n_blk, 1, 0) * jnp.where(i >= n_ctx_blk, 1, 0))
    ok = jnp.where(si >= 3 * blk, 1, ok)
    bias = jnp.where(ok > 0, 0.0, NEG_INF)

    lane_lo = lax.broadcasted_iota(jnp.int32, (1, LANES), 1) < HEAD_DIM
    group = N_HEADS // N_KV
    outs = [None] * N_HEADS
    for kvh in range(N_KV):
        kv_slab = slice(LANES * (kvh // 2), LANES * (kvh // 2 + 1))
        kv_lo = kvh % 2 == 0
        rows, sinks = [], []
        for hq in range(group):
            h = kvh * group + hq
            qp = q[:, LANES * (h // 2):LANES * (h // 2 + 1)]
            if (h % 2 == 0) != kv_lo:
                qp = pltpu.roll(qp, HEAD_DIM, 1)
            rows.append(jnp.where(lane_lo, qp, 0.0) if kv_lo else jnp.where(lane_lo, 0.0, qp))
            sinks.append(jnp.full((blk, 1), sink_ref[h], F32))
        qg = jnp.concatenate(rows, axis=0)
        sink = jnp.concatenate(sinks, axis=0)
        s = _dot_nt(qg, k_all[:, kv_slab]) + jnp.concatenate([bias] * group, axis=0)
        m = jnp.maximum(jnp.max(s, axis=-1, keepdims=True), sink)
        e = jnp.exp(s - m)
        denom = jnp.sum(e, axis=-1, keepdims=True) + jnp.exp(sink - m)
        o = _dot(e, v_all[:, kv_slab]) / denom
        for hq in range(group):
            h = kvh * group + hq
            oh = o[hq * blk:(hq + 1) * blk]
            if (h % 2 == 0) != kv_lo:
                oh = pltpu.roll(oh, HEAD_DIM, 1)
            outs[h] = oh
    o_ref[...] = jnp.concatenate(
        [jnp.where(lane_lo, outs[2 * p], outs[2 * p + 1]) for p in range(N_PAIR)], axis=1)


def _attention(qkv, sink, cos, sin, *, ctx):
    bsz, total, _ = qkv.shape
    blk = ATT_BLOCK
    n_blk, n_ctx_blk = total // blk, ctx // blk
    kvw = N_KV * HEAD_DIM
    kcol, vcol = D_MODEL // kvw, D_MODEL // kvw + 1
    kern = functools.partial(_attn_kernel, n_ctx_blk=n_ctx_blk, n_blk=n_blk)
    prev_i = lambda i: jnp.maximum(i - 1, 0)
    next_i = lambda i: jnp.minimum(i + 1, n_blk - 1)

    def kv(col, row):
        return pl.BlockSpec((None, blk, kvw), lambda b, i: (b, row(i), col))

    def kv_ctx(col):
        return pl.BlockSpec((None, ctx, kvw), lambda b, i: (b, 0, col))

    def tab(row):
        return pl.BlockSpec((blk, LANES), lambda b, i: (row(i), 0))

    same = lambda i: i
    return pl.pallas_call(
        kern,
        grid=(bsz, n_blk),
        in_specs=[pl.BlockSpec(memory_space=pltpu.SMEM),
                  pl.BlockSpec((None, blk, D_MODEL), lambda b, i: (b, i, 0)),
                  kv(kcol, prev_i), kv(kcol, same), kv(kcol, next_i), kv_ctx(kcol),
                  kv(vcol, prev_i), kv(vcol, same), kv(vcol, next_i), kv_ctx(vcol),
                  tab(same), tab(same), tab(prev_i), tab(prev_i), tab(next_i), tab(next_i)],
        out_specs=pl.BlockSpec((None, blk, D_MODEL), lambda b, i: (b, i, 0)),
        out_shape=jax.ShapeDtypeStruct((bsz, total, D_MODEL), F32),
        compiler_params=_cparams(("arbitrary", "arbitrary")),
    )(sink, qkv, qkv, qkv, qkv, qkv, qkv, qkv, qkv, qkv, cos, sin, cos, sin, cos, sin)


def _merge_kernel(x_ref, mod_ref, cv_ref, before_ref, after_ref, gate_ref, att_ref, y_ref, bonus_ref, g_ref,
                  cw_ref, lng_ref, lnb_ref, wa_ref, wb_ref, wc_ref, wo_ref, o_ref, *, tm, ctx, total):
    i = pl.program_id(1)
    gr = _row_ids(i, tm)
    d = D_MODEL
    cv = cv_ref[...]
    z = cv[:, d:2 * d] * cv[:, 2 * d:]
    z_before = before_ref[HALO - 1:HALO, d:2 * d] * before_ref[HALO - 1:HALO, 2 * d:]
    z_after = after_ref[0:1, d:2 * d] * after_ref[0:1, 2 * d:]
    prev, nxt = _neighbours(z, z_before, z_after, gr, tm, ctx, total)
    cw = cw_ref[...]
    conv = cv[:, :d] * (prev * cw[0:1] + z * cw[1:2] + nxt * cw[2:3])

    lane_lo = lax.broadcasted_iota(jnp.int32, (1, LANES), 1) < HEAD_DIM
    lng = lng_ref[...]
    lnb = lnb_ref[...]
    slabs = []
    for p in range(N_PAIR):
        sl = slice(LANES * p, LANES * (p + 1))
        y = y_ref[0, p] + y_ref[1, p]
        mean = _head_sums(y, lane_lo) * (1.0 / HEAD_DIM)
        yc = y - mean
        var = _head_sums(yc * yc, lane_lo) * (1.0 / HEAD_DIM)
        gn = yc * lax.rsqrt(var + GN_EPS) * lng[:, sl] + lnb[:, sl]
        slabs.append((gn + bonus_ref[p]) * g_ref[p])
    rwkv = jnp.concatenate(slabs, axis=1)

    gates = gate_ref[...]
    m = (_sigmoid(gates[:, :d]) * _dot(conv, wa_ref[...])
         + _sigmoid(gates[:, d:2 * d]) * _dot(rwkv, wb_ref[...])
         + _sigmoid(gates[:, 2 * d:]) * _dot(att_ref[...], wc_ref[...]))
    is_ctx = gr < ctx
    o_ref[...] = x_ref[...] + _mod_row(mod_ref, 2, is_ctx) * _dot(m, wo_ref[...])


def _merge(x, mod, cv, gates, att, y, bonus, g, conv_w, ln_g, ln_b, wa, wb, wc, wo, *, ctx, tm):
    bsz, total, d = x.shape
    kern = functools.partial(_merge_kernel, tm=tm, ctx=ctx, total=total)
    before, after = _halo_specs(3 * d, tm, total)
    row = lambda w: pl.BlockSpec((None, tm, w), lambda b, i: (b, i, 0))
    full = lambda shp: pl.BlockSpec(shp, lambda b, i: tuple(0 for _ in shp))
    pm = pl.BlockSpec((None, N_PAIR, tm, LANES), lambda b, i: (b, 0, i, 0))
    pm2 = pl.BlockSpec((2, None, N_PAIR, tm, LANES), lambda b, i: (0, b, 0, i, 0))
    return pl.pallas_call(
        kern,
        grid=(bsz, total // tm),
        in_specs=[row(d), pl.BlockSpec((None, 2, 6, d), lambda b, i: (b, 0, 0, 0)),
                  row(3 * d), before, after, row(3 * d), row(d), pm2, pm, pm,
                  full((3, d)), full((1, d)), full((1, d)),
                  full((d, d)), full((d, d)), full((d, d)), full((d, d))],
        out_specs=row(d),
        out_shape=jax.ShapeDtypeStruct((bsz, total, d), F32),
        compiler_params=_cparams(("arbitrary", "arbitrary")),
    )(x, mod, cv, cv, cv, gates, att, y, bonus, g, conv_w, ln_g, ln_b, wa, wb, wc, wo)


def _ffn_down_kernel(x_ref, mod_ref, u_ref, before_ref, after_ref, cw_ref, wd_ref, o_ref, *, tm, ctx, total):
    i = pl.program_id(1)
    gr = _row_ids(i, tm)
    u = u_ref[...]
    prev, nxt = _neighbours(u, before_ref[HALO - 1:HALO, :], after_ref[0:1, :], gr, tm, ctx, total)
    cw = cw_ref[...]
    uc = prev * cw[0:1] + u * cw[1:2] + nxt * cw[2:3]
    ug, uv = uc[:, :D_FF], uc[:, D_FF:]
    act = ug * _sigmoid(ug) * uv
    o_ref[...] = x_ref[...] + _mod_row(mod_ref, 5, gr < ctx) * _dot(act, wd_ref[...])


def _ffn_down(x, mod, u, conv_w, wd, *, ctx, tm):
    bsz, total, d = x.shape
    kern = functools.partial(_ffn_down_kernel, tm=tm, ctx=ctx, total=total)
    before, after = _halo_specs(2 * D_FF, tm, total)
    row = lambda w: pl.BlockSpec((None, tm, w), lambda b, i: (b, i, 0))
    full = lambda shp: pl.BlockSpec(shp, lambda b, i: tuple(0 for _ in shp))
    return pl.pallas_call(
        kern,
        grid=(bsz, total // tm),
        in_specs=[row(d), pl.BlockSpec((None, 2, 6, d), lambda b, i: (b, 0, 0, 0)),
                  row(2 * D_FF), before, after, full((3, 2 * D_FF)), full((D_FF, d))],
        out_specs=row(d),
        out_shape=jax.ShapeDtypeStruct((bsz, total, d), F32),
        compiler_params=_cparams(("arbitrary", "arbitrary")),
    )(x, mod, u, u, u, conv_w, wd)


def _final_norm_kernel(x_ref, g_ref, o_ref):
    x = x_ref[...]
    o_ref[...] = x * lax.rsqrt(jnp.mean(x * x, axis=-1, keepdims=True) + NORM_EPS) * g_ref[...]


def _final_norm(x, g, *, ctx, seq, tm):
    bsz, _, d = x.shape
    off = ctx // tm
    return pl.pallas_call(
        _final_norm_kernel,
        grid=(bsz, seq // tm),
        in_specs=[pl.BlockSpec((None, tm, d), lambda b, i: (b, i + off, 0)),
                  pl.BlockSpec((1, d), lambda b, i: (0, 0))],
        out_specs=pl.BlockSpec((None, tm, d), lambda b, i: (b, i, 0)),
        out_shape=jax.ShapeDtypeStruct((bsz, seq, d), F32),
        compiler_params=_cparams(("arbitrary", "arbitrary")),
    )(x, g)


def _rope_tables(ctx, seq):
    rows = seq // GRID_W
    t_row = jnp.broadcast_to(jnp.arange(rows)[:, None], (rows, GRID_W)).reshape(-1).astype(F32)
    t_col = jnp.broadcast_to(jnp.arange(GRID_W)[None, :], (rows, GRID_W)).reshape(-1).astype(F32)
    n_freq = HEAD_DIM // 4
    inv = ROPE_THETA ** (-jnp.arange(n_freq, dtype=F32) / n_freq)
    ar, ac = t_row[:, None] * inv, t_col[:, None] * inv
    cos = jnp.concatenate([jnp.cos(ar), jnp.cos(ar), jnp.cos(ac), jnp.cos(ac)], axis=1)
    sin = jnp.concatenate([-jnp.sin(ar), jnp.sin(ar), -jnp.sin(ac), jnp.sin(ac)], axis=1)
    cos = jnp.concatenate([jnp.ones((ctx, HEAD_DIM), F32), cos], axis=0)
    sin = jnp.concatenate([jnp.zeros((ctx, HEAD_DIM), F32), sin], axis=0)
    return jnp.tile(cos, (1, LANES // HEAD_DIM)), jnp.tile(sin, (1, LANES // HEAD_DIM))


def _block_diag2(w):
    z = jnp.zeros_like(w[0])
    return jnp.concatenate([jnp.concatenate([w[0], z], axis=1), jnp.concatenate([z, w[1]], axis=1)], axis=0)


def kernel(x, c, ctx, c_ctx, ada_w, ada_b, norm1_g, w_in, conv_a_w, a_out_w, rwkv_mu, rwkv_w0, rwkv_w2,
           rwkv_a0, rwkv_a2, rwkv_g2, rwkv_k_k, rwkv_k_a, rwkv_r_k, rwkv_ln_g, rwkv_ln_b, rwkv_out_w,
           attn_sink, attn_out_w, w_o, norm2_g, ffn_up, ffn_conv, ffn_down, final_norm_g):
    bsz, seq, d = x.shape
    n_ctx = ctx.shape[1]
    depth = ada_w.shape[0]
    total = n_ctx + seq
    assert d == D_MODEL and seq % ATT_BLOCK == 0 and n_ctx % ATT_BLOCK == 0 and bsz + 1 <= HALO
    tm = 256
    assert total % tm == 0 and n_ctx % tm == 0

    cvec = jnp.zeros((HALO, d), F32).at[:bsz].set(c).at[bsz].set(c_ctx)
    mods = _ada(cvec, ada_w.astype(BF16), ada_b)
    mods = mods.reshape(depth, HALO, 6, d)
    mods = jnp.stack([jnp.broadcast_to(mods[:, bsz:bsz + 1], (depth, bsz, 6, d)), mods[:, :bsz]], axis=2)

    cos, sin = _rope_tables(n_ctx, seq)
    masks = _scan_masks()
    xa = jnp.concatenate([ctx, x], axis=1)

    o_rkv, o_lora, o_qkv, o_gate = 3 * d, 6 * d, 6 * d + 3 * LANES, 6 * d + 3 * LANES + d + 2 * N_KV * HEAD_DIM
    for l in range(depth):
        mod = mods[l]
        w = w_in[l].astype(BF16)
        proj = functools.partial(_nmm, xa, mod, norm1_g[l][None], k_shift=0, k_scale=1, ctx=n_ctx, tm=tm)
        cv = proj(w[:, :o_rkv], tn=3 * d)
        rkv = proj(w[:, o_rkv:o_lora], tn=3 * d)
        lora = proj(w[:, o_lora:o_qkv], tn=3 * LANES)
        qkv = proj(w[:, o_qkv:o_gate], tn=o_gate - o_qkv)
        gates = proj(w[:, o_gate:], tn=3 * d)

        r, v, kn, lw, kd, bd, bonus, g = _prep(
            rkv, lora, rwkv_mu[l], rwkv_w0[l].reshape(1, 2 * d), _block_diag2(rwkv_w2[l]).astype(BF16),
            rwkv_a0[l].reshape(1, 2 * d), _block_diag2(rwkv_a2[l]).astype(BF16), rwkv_g2[l].astype(BF16),
            rwkv_k_k[l][None], rwkv_k_a[l][None], rwkv_r_k[l].reshape(1, d), ctx=n_ctx, tm=tm)
        y = _scan(r, v, kn, lw, kd, bd, masks, ctx=n_ctx)
        att = _attention(qkv, attn_sink[l], cos, sin, ctx=n_ctx)
        xa = _merge(xa, mod, cv, gates, att, y, bonus, g, conv_a_w[l], rwkv_ln_g[l][None], rwkv_ln_b[l][None],
                    a_out_w[l].astype(BF16), rwkv_out_w[l].astype(BF16), attn_out_w[l].astype(BF16),
                    w_o[l].astype(BF16), ctx=n_ctx, tm=tm)
        u = _nmm(xa, mod, norm2_g[l][None], ffn_up[l].astype(BF16), k_shift=3, k_scale=4, ctx=n_ctx, tm=tm,
                 tn=D_FF)
        xa = _ffn_down(xa, mod, u, ffn_conv[l], ffn_down[l].astype(BF16), ctx=n_ctx, tm=tm)
    return _final_norm(xa, final_norm_g[None], ctx=n_ctx, seq=seq, tm=tm)
```

```python
import functools

import jax
import jax.numpy as jnp
import numpy as np
from jax import lax
from jax.experimental import pallas as pl
from jax.experimental.pallas import tpu as pltpu

D_MODEL = 1024
GRID_W = 64
N_HEADS = 16
HEAD_DIM = 64
N_KV = 4
R_LORA = 64
R_GATE = 128
GN_EPS = 64e-5
WINDOW = 128
ATT_BLOCK = 128
ATT_SCALE = HEAD_DIM ** -0.5
ROPE_THETA = 10000.0
NEG_INF = -1e30
D_FF = 2816
NORM_EPS = 1e-6
KK_EPS = 1e-12

LANES = 128
HALO = 8
N_PAIR = D_MODEL // LANES
CHUNK = 64
N_LEVELS = 5
VMEM_LIMIT = 52 * 1024 * 1024

BF16 = jnp.bfloat16
F32 = jnp.float32

KV_W = N_KV * HEAD_DIM
LORA_W = 2 * R_LORA + 2 * R_LORA + R_GATE
P_CONV, P_RKV, P_GATE = 0, 3 * D_MODEL, 6 * D_MODEL
P_Q = 9 * D_MODEL
P_K = P_Q + D_MODEL
P_V = P_K + KV_W
P_LORA = P_V + KV_W
N_PROJ = P_LORA + LORA_W
assert P_K % KV_W == 0 and P_LORA % LORA_W == 0 and N_PROJ % (3 * LANES) == 0


def _cparams(sem):
    return pltpu.CompilerParams(dimension_semantics=sem, vmem_limit_bytes=VMEM_LIMIT)


def _dot(a, b):
    return jnp.dot(a.astype(BF16), b.astype(BF16), preferred_element_type=F32)


def _dot_nt(a, b):
    return lax.dot_general(a.astype(BF16), b.astype(BF16), (((1,), (1,)), ((), ())),
                           preferred_element_type=F32)


def _bmm(a, b):
    return jnp.einsum('pmk,pkn->pmn', a.astype(BF16), b.astype(BF16), preferred_element_type=F32)


def _bmm_nt(a, b):
    return jnp.einsum('pmk,pnk->pmn', a.astype(BF16), b.astype(BF16), preferred_element_type=F32)


def _bmm_tn(a, b):
    return jnp.einsum('pkm,pkn->pmn', a.astype(BF16), b.astype(BF16), preferred_element_type=F32)


def _sigmoid(x):
    return 1.0 / (1.0 + jnp.exp(-x))


def _row_ids(i, tm):
    return i * tm + lax.broadcasted_iota(jnp.int32, (tm, 1), 0)


def _mod_row(mod_ref, k, is_ctx):
    return jnp.where(is_ctx, mod_ref[0, k:k + 1, :], mod_ref[1, k:k + 1, :])


def _neighbours(z, before, after, gr, tm, ctx, total):
    row = lax.broadcasted_iota(jnp.int32, (tm, 1), 0)
    prev = jnp.where(row == 0, before, pltpu.roll(z, 1, 0))
    nxt = jnp.where(row == tm - 1, after, pltpu.roll(z, tm - 1, 0))
    first = jnp.where(gr == 0, 1, 0) + jnp.where(gr == ctx, 1, 0)
    last = jnp.where(gr == ctx - 1, 1, 0) + jnp.where(gr == total - 1, 1, 0)
    prev = jnp.where(first > 0, 0.0, prev)
    nxt = jnp.where(last > 0, 0.0, nxt)
    return prev, nxt


def _halo_specs(width, tm, total, col=0):
    per = tm // HALO
    nblk = total // HALO
    before = pl.BlockSpec((None, HALO, width), lambda b, i: (b, jnp.maximum(i * per - 1, 0), col))
    after = pl.BlockSpec((None, HALO, width), lambda b, i: (b, jnp.minimum((i + 1) * per, nblk - 1), col))
    return before, after


def _ada_kernel(c_ref, w_ref, b_ref, o_ref):
    c = c_ref[...]
    o_ref[...] = _dot(c * _sigmoid(c), w_ref[...]) + b_ref[...]


def _ada(cvec, ada_w, ada_b):
    depth, d, n = ada_w.shape
    tn = 1536
    return pl.pallas_call(
        _ada_kernel,
        grid=(depth, n // tn),
        in_specs=[pl.BlockSpec((HALO, d), lambda l, j: (0, 0)),
                  pl.BlockSpec((None, d, tn), lambda l, j: (l, 0, j)),
                  pl.BlockSpec((None, 1, tn), lambda l, j: (l, 0, j))],
        out_specs=pl.BlockSpec((None, HALO, tn), lambda l, j: (l, 0, j)),
        out_shape=jax.ShapeDtypeStruct((depth, HALO, n), F32),
        compiler_params=_cparams(("arbitrary", "arbitrary")),
    )(cvec, ada_w, ada_b.reshape(depth, 1, n))


def _nmm_kernel(x_ref, mod_ref, g_ref, w_ref, o_ref, *, k_shift, k_scale, tm, ctx):
    i = pl.program_id(2)
    x = x_ref[...]
    y = x * lax.rsqrt(jnp.mean(x * x, axis=-1, keepdims=True) + NORM_EPS) * g_ref[...]
    is_ctx = _row_ids(i, tm) < ctx
    h = y * (1.0 + _mod_row(mod_ref, k_scale, is_ctx)) + _mod_row(mod_ref, k_shift, is_ctx)
    o_ref[...] = _dot(h, w_ref[...])


def _nmm(x, mod, g, w, *, k_shift, k_scale, ctx, tm, tn):
    bsz, total, d = x.shape
    n = w.shape[1]
    kern = functools.partial(_nmm_kernel, k_shift=k_shift, k_scale=k_scale, tm=tm, ctx=ctx)
    return pl.pallas_call(
        kern,
        grid=(n // tn, bsz, total // tm),
        in_specs=[pl.BlockSpec((None, tm, d), lambda j, b, i: (b, i, 0)),
                  pl.BlockSpec((None, 2, 6, d), lambda j, b, i: (b, 0, 0, 0)),
                  pl.BlockSpec((1, d), lambda j, b, i: (0, 0)),
                  pl.BlockSpec((d, tn), lambda j, b, i: (0, j))],
        out_specs=pl.BlockSpec((None, tm, tn), lambda j, b, i: (b, i, j)),
        out_shape=jax.ShapeDtypeStruct((bsz, total, n), F32),
        compiler_params=_cparams(("arbitrary", "arbitrary", "arbitrary")),
    )(x, mod, g, w)


def _head_sums(z, lane_lo):
    s_lo = jnp.sum(jnp.where(lane_lo, z, 0.0), axis=-1, keepdims=True)
    s_hi = jnp.sum(jnp.where(lane_lo, 0.0, z), axis=-1, keepdims=True)
    return jnp.where(lane_lo, s_lo, s_hi)


def _prep_kernel(rkv_ref, before_ref, after_ref, lora_ref, mu_ref, w0_ref, w2_ref, a0_ref, a2_ref, g2_ref,
                 kk_ref, ka_ref, rk_ref,
                 r_out, v_out, kn_out, lw_out, kd_out, bd_out, bonus_out, g_out, *, tm, ctx, total):
    i = pl.program_id(1)
    gr = _row_ids(i, tm)
    x = rkv_ref[...]
    prev, nxt = _neighbours(x, before_ref[HALO - 1:HALO, :], after_ref[0:1, :], gr, tm, ctx, total)
    mu_prev = mu_ref[0:1, :]
    mu_next = mu_ref[1:2, :]
    s = prev * mu_prev + x * (1.0 - mu_prev - mu_next) + nxt * mu_next
    d = D_MODEL
    r, k, v = s[:, :d], s[:, d:2 * d], s[:, 2 * d:]

    lo = lora_ref[...]
    w_pre = w0_ref[...] + _dot(jnp.tanh(lo[:, :LANES]), w2_ref[...])
    w_log = -(jnp.maximum(-w_pre, 0.0) + jnp.log1p(jnp.exp(-jnp.abs(w_pre)))) - 0.5
    lw = -jnp.exp(w_log)
    a = _sigmoid(a0_ref[...] + _dot(lo[:, LANES:2 * LANES], a2_ref[...]))
    g = _dot(_sigmoid(lo[:, 2 * LANES:]), g2_ref[...])
    kk = k * kk_ref[...]
    ka = ka_ref[...]
    rk = rk_ref[...]

    lane_lo = lax.broadcasted_iota(jnp.int32, (1, LANES), 1) < HEAD_DIM
    for p in range(N_PAIR):
        sl = slice(LANES * p, LANES * (p + 1))
        kkp = kk[:, sl]
        kn = kkp * lax.rsqrt(_head_sums(kkp * kkp, lane_lo) + KK_EPS)
        rp, kp, vp = r[:, sl], k[:, sl], v[:, sl]
        bonus = jnp.zeros_like(rp)
        for dr in range(2):
            dsl = slice(d * dr + LANES * p, d * dr + LANES * (p + 1))
            ad = a[:, dsl]
            kd = kp * (1.0 + (ad - 1.0) * ka[:, sl])
            bonus = bonus + _head_sums(rp * kd * rk[:, sl], lane_lo) * vp
            lw_out[dr, p] = lw[:, dsl]
            kd_out[dr, p] = kd
            bd_out[dr, p] = kn * ad
        r_out[p] = rp
        v_out[p] = vp
        kn_out[p] = kn
        bonus_out[p] = bonus
        g_out[p] = g[:, sl]


def _prep(proj, mu, w0, w2bd, a0, a2bd, g2, k_k, k_a, r_k, *, ctx, tm):
    bsz, total, _ = proj.shape
    d = D_MODEL
    kern = functools.partial(_prep_kernel, tm=tm, ctx=ctx, total=total)
    before, after = _halo_specs(3 * d, tm, total, P_RKV // (3 * d))
    full = lambda shp: pl.BlockSpec(shp, lambda b, i: tuple(0 for _ in shp))
    pm = pl.BlockSpec((None, N_PAIR, tm, LANES), lambda b, i: (b, 0, i, 0))
    pm2 = pl.BlockSpec((2, None, N_PAIR, tm, LANES), lambda b, i: (0, b, 0, i, 0))
    sds = jax.ShapeDtypeStruct((bsz, N_PAIR, total, LANES), F32)
    sds2 = jax.ShapeDtypeStruct((2, bsz, N_PAIR, total, LANES), F32)
    return pl.pallas_call(
        kern,
        grid=(bsz, total // tm),
        in_specs=[pl.BlockSpec((None, tm, 3 * d), lambda b, i: (b, i, P_RKV // (3 * d))), before, after,
                  pl.BlockSpec((None, tm, LORA_W), lambda b, i: (b, i, P_LORA // LORA_W)),
                  full((2, 3 * d)), full((1, 2 * d)), full((LANES, 2 * d)), full((1, 2 * d)),
                  full((LANES, 2 * d)), full((R_GATE, d)), full((1, d)), full((1, d)), full((1, d))],
        out_specs=[pm, pm, pm, pm2, pm2, pm2, pm, pm],
        out_shape=[sds, sds, sds, sds2, sds2, sds2, sds, sds],
        compiler_params=_cparams(("arbitrary", "arbitrary")),
    )(proj, proj, proj, proj, mu, w0, w2bd, a0, a2bd, g2, k_k, k_a, r_k)


M_STRICT, M_INCL, M_EYE, M_LEVEL0 = 0, 1, 2, 3
N_MASKS = M_LEVEL0 + 1 + N_LEVELS


def _scan_masks():
    c = CHUNK
    t = np.arange(c)[:, None]
    s = np.arange(c)[None, :]
    cum = np.zeros((2, 2 * c, c), np.float32)
    msk = np.zeros((2, N_MASKS, c, 2 * c), np.float32)
    for dr in range(2):
        earlier = (s < t) if dr == 0 else (s > t)
        cum[dr, :c] = earlier | (s == t)
        cum[dr, c:] = earlier
        planes = [earlier, earlier | (s == t), s == t]
        m = 1
        while m < c:
            same = (t // (2 * m)) == (s // (2 * m))
            t_late = (t % (2 * m)) >= m
            s_late = (s % (2 * m)) >= m
            planes.append(same & (t_late & ~s_late if dr == 0 else ~t_late & s_late))
            m *= 2
        for q, plane in enumerate(planes):
            msk[dr, q] = np.concatenate([plane, plane], axis=1)
    bdm = np.kron(np.eye(2, dtype=np.float32), np.ones((c, c), np.float32))
    return jnp.asarray(cum), jnp.asarray(msk), jnp.asarray(bdm)


def _scan_kernel(cum_ref, msk_ref, bdm_ref, r_ref, v_ref, kn_ref, lw_ref, kd_ref, bd_ref, y_ref, s_ref):
    c = CHUNK
    bsz = r_ref.shape[0]
    n = bsz * N_PAIR

    @pl.when(pl.program_id(1) == 0)
    def _():
        s_ref[...] = jnp.zeros_like(s_ref)

    bdm = bdm_ref[...]
    m_strict = msk_ref[M_STRICT]
    m_incl = msk_ref[M_INCL]

    def load(ref):
        return ref[...].reshape(n, c, LANES)

    def stack(z):
        return jnp.concatenate([z, z], axis=1) * bdm

    lw = load(lw_ref)
    lw_wide = jnp.concatenate([lw[q] for q in range(n)], axis=1)
    cl = jnp.dot(cum_ref[...], lw_wide, precision=lax.Precision.HIGHEST, preferred_element_type=F32)
    cl = jnp.stack([cl[:, LANES * q:LANES * (q + 1)] for q in range(n)], axis=0)
    cl_incl, cl_excl = cl[:, :c], cl[:, c:]
    g_end = jnp.exp(jnp.sum(lw, axis=1, keepdims=True))
    inv_g = jnp.exp(-cl_incl)
    at = -load(kn_ref) * jnp.exp(cl_excl)
    rt = load(r_ref) * jnp.exp(cl_incl)
    bt = load(bd_ref) * inv_g
    kt = load(kd_ref) * inv_g
    v = load(v_ref)
    state = s_ref[...]

    ar = jnp.concatenate([at, rt], axis=1).astype(BF16)
    gram = _bmm_nt(ar, jnp.concatenate([stack(bt), stack(kt)], axis=1))
    g_ab = gram[:, :c, :2 * c]
    from_v = jnp.concatenate([gram[:, :c, 2 * c:] * m_strict, gram[:, c:, 2 * c:] * m_incl], axis=1)
    partial = _bmm_nt(ar, state) + _bmm(from_v, stack(v))

    tri = msk_ref[M_EYE] + g_ab * msk_ref[M_LEVEL0]
    for lvl in range(1, N_LEVELS + 1):
        w = _bmm(g_ab * msk_ref[M_LEVEL0 + lvl], stack(tri))
        tri = tri + _bmm(tri, stack(w))

    u = _bmm(tri, stack(partial[:, :c]))
    y = partial[:, c:] + _bmm(gram[:, c:, :2 * c] * m_incl, stack(u))
    y_ref[...] = y.reshape(bsz, N_PAIR, c, LANES)
    upd = _bmm_tn(jnp.concatenate([u, v], axis=1), jnp.concatenate([bt * g_end, kt * g_end], axis=1))
    s_ref[...] = state * g_end + upd * bdm


def _scan(r, v, kn, lw, kd, bd, masks, *, ctx):
    bsz, _, total, _ = r.shape
    c = CHUNK
    nc, ncc = total // c, ctx // c
    cum, msk, bdm = masks

    def chunk(dr, j):
        rev = jnp.where(j < ncc, ncc - 1 - j, nc + ncc - 1 - j)
        return jnp.where(dr == 0, j, rev)

    shared = pl.BlockSpec((bsz, N_PAIR, c, LANES), lambda dr, j: (0, 0, chunk(dr, j), 0))
    per_dir = pl.BlockSpec((None, bsz, N_PAIR, c, LANES), lambda dr, j: (dr, 0, 0, chunk(dr, j), 0))
    return pl.pallas_call(
        _scan_kernel,
        grid=(2, nc),
        in_specs=[pl.BlockSpec((None, 2 * c, c), lambda dr, j: (dr, 0, 0)),
                  pl.BlockSpec((None, N_MASKS, c, 2 * c), lambda dr, j: (dr, 0, 0, 0)),
                  pl.BlockSpec((2 * c, 2 * c), lambda dr, j: (0, 0)),
                  shared, shared, shared, per_dir, per_dir, per_dir],
        out_specs=per_dir,
        out_shape=jax.ShapeDtypeStruct((2, bsz, N_PAIR, total, LANES), F32),
        scratch_shapes=[pltpu.VMEM((bsz * N_PAIR, 2 * HEAD_DIM, 2 * HEAD_DIM), F32)],
        compiler_params=_cparams(("arbitrary", "arbitrary")),
    )(cum, msk, bdm, r, v, kn, lw, kd, bd)


def _rope(z, cos, sin):
    lane = lax.broadcasted_iota(jnp.int32, (1, LANES), 1)
    first = jnp.bitwise_and(lane, HEAD_DIM // 2 - 1) < (HEAD_DIM // 4)
    out = []
    for p in range(z.shape[1] // LANES):
        zp = z[:, LANES * p:LANES * (p + 1)]
        partner = jnp.where(first, pltpu.roll(zp, LANES - HEAD_DIM // 4, 1), pltpu.roll(zp, HEAD_DIM // 4, 1))
        out.append(zp * cos + partner * sin)
    return jnp.concatenate(out, axis=1)


def _attn_kernel(sink_ref, q_ref, kp_ref, kc_ref, kn_ref, kx_ref, vp_ref, vc_ref, vn_ref, vx_ref,
                 cq_ref, sq_ref, cp_ref, sp_ref, cn_ref, sn_ref, o_ref, *, n_ctx_blk, n_blk):
    i = pl.program_id(1)
    blk = ATT_BLOCK
    q = _rope(q_ref[...], cq_ref[...], sq_ref[...]) * ATT_SCALE
    k_all = jnp.concatenate([_rope(kp_ref[...], cp_ref[...], sp_ref[...]),
                             _rope(kc_ref[...], cq_ref[...], sq_ref[...]),
                             _rope(kn_ref[...], cn_ref[...], sn_ref[...]),
                             kx_ref[...]], axis=0)
    v_all = jnp.concatenate([vp_ref[...], vc_ref[...], vn_ref[...], vx_ref[...]], axis=0)
    n_keys = k_all.shape[0]

    qi = lax.broadcasted_iota(jnp.int32, (blk, n_keys), 0)
    si = lax.broadcasted_iota(jnp.int32, (blk, n_keys), 1)
    rel = si - blk - qi
    key_blk = i - 1 + jnp.right_shift(si, 7)
    ok = (jnp.where(jnp.abs(rel) <= WINDOW, 1, 0) * jnp.where(key_blk >= n_ctx_blk, 1, 0)
          * jnp.where(key_blk < n_blk, 1, 0) * jnp.where(i >= n_ctx_blk, 1, 0))
    ok = jnp.where(si >= 3 * blk, 1, ok)
    bias = jnp.where(ok > 0, 0.0, NEG_INF)

    lane_lo = lax.broadcasted_iota(jnp.int32, (1, LANES), 1) < HEAD_DIM
    group = N_HEADS // N_KV
    outs = [None] * N_HEADS
    for kvh in range(N_KV):
        kv_slab = slice(LANES * (kvh // 2), LANES * (kvh // 2 + 1))
        kv_lo = kvh % 2 == 0
        rows, sinks = [], []
        for hq in range(group):
            h = kvh * group + hq
            qp = q[:, LANES * (h // 2):LANES * (h // 2 + 1)]
            if (h % 2 == 0) != kv_lo:
                qp = pltpu.roll(qp, HEAD_DIM, 1)
            rows.append(jnp.where(lane_lo, qp, 0.0) if kv_lo else jnp.where(lane_lo, 0.0, qp))
            sinks.append(jnp.full((blk, 1), sink_ref[h], F32))
        qg = jnp.concatenate(rows, axis=0)
        sink = jnp.concatenate(sinks, axis=0)
        s = _dot_nt(qg, k_all[:, kv_slab]) + jnp.concatenate([bias] * group, axis=0)
        m = jnp.maximum(jnp.max(s, axis=-1, keepdims=True), sink)
        e = jnp.exp(s - m)
        denom = jnp.sum(e, axis=-1, keepdims=True) + jnp.exp(sink - m)
        o = _dot(e, v_all[:, kv_slab]) / denom
        for hq in range(group):
            h = kvh * group + hq
            oh = o[hq * blk:(hq + 1) * blk]
            if (h % 2 == 0) != kv_lo:
                oh = pltpu.roll(oh, HEAD_DIM, 1)
            outs[h] = oh
    o_ref[...] = jnp.concatenate(
        [jnp.where(lane_lo, outs[2 * p], outs[2 * p + 1]) for p in range(N_PAIR)], axis=1)


def _attention(qkv, sink, cos, sin, *, ctx):
    bsz, total, _ = qkv.shape
    blk = ATT_BLOCK
    n_blk, n_ctx_blk = total // blk, ctx // blk
    kvw = KV_W
    kcol, vcol = P_K // kvw, P_V // kvw
    kern = functools.partial(_attn_kernel, n_ctx_blk=n_ctx_blk, n_blk=n_blk)
    prev_i = lambda i: jnp.maximum(i - 1, 0)
    next_i = lambda i: jnp.minimum(i + 1, n_blk - 1)

    def kv(col, row):
        return pl.BlockSpec((None, blk, kvw), lambda b, i: (b, row(i), col))

    def kv_ctx(col):
        return pl.BlockSpec((None, ctx, kvw), lambda b, i: (b, 0, col))

    def tab(row):
        return pl.BlockSpec((blk, LANES), lambda b, i: (row(i), 0))

    same = lambda i: i
    return pl.pallas_call(
        kern,
        grid=(bsz, n_blk),
        in_specs=[pl.BlockSpec(memory_space=pltpu.SMEM),
                  pl.BlockSpec((None, blk, D_MODEL), lambda b, i: (b, i, P_Q // D_MODEL)),
                  kv(kcol, prev_i), kv(kcol, same), kv(kcol, next_i), kv_ctx(kcol),
                  kv(vcol, prev_i), kv(vcol, same), kv(vcol, next_i), kv_ctx(vcol),
                  tab(same), tab(same), tab(prev_i), tab(prev_i), tab(next_i), tab(next_i)],
        out_specs=pl.BlockSpec((None, blk, D_MODEL), lambda b, i: (b, i, 0)),
        out_shape=jax.ShapeDtypeStruct((bsz, total, D_MODEL), F32),
        compiler_params=_cparams(("arbitrary", "arbitrary")),
    )(sink, qkv, qkv, qkv, qkv, qkv, qkv, qkv, qkv, qkv, cos, sin, cos, sin, cos, sin)


def _merge_kernel(x_ref, mod_ref, cv_ref, before_ref, after_ref, gate_ref, att_ref, y_ref, bonus_ref, g_ref,
                  cw_ref, lng_ref, lnb_ref, wa_ref, wb_ref, wc_ref, wo_ref, o_ref, *, tm, ctx, total):
    i = pl.program_id(1)
    gr = _row_ids(i, tm)
    d = D_MODEL
    cv = cv_ref[...]
    z = cv[:, d:2 * d] * cv[:, 2 * d:]
    z_before = before_ref[HALO - 1:HALO, d:2 * d] * before_ref[HALO - 1:HALO, 2 * d:]
    z_after = after_ref[0:1, d:2 * d] * after_ref[0:1, 2 * d:]
    prev, nxt = _neighbours(z, z_before, z_after, gr, tm, ctx, total)
    cw = cw_ref[...]
    conv = cv[:, :d] * (prev * cw[0:1] + z * cw[1:2] + nxt * cw[2:3])

    lane_lo = lax.broadcasted_iota(jnp.int32, (1, LANES), 1) < HEAD_DIM
    lng = lng_ref[...]
    lnb = lnb_ref[...]
    slabs = []
    for p in range(N_PAIR):
        sl = slice(LANES * p, LANES * (p + 1))
        y = y_ref[0, p] + y_ref[1, p]
        mean = _head_sums(y, lane_lo) * (1.0 / HEAD_DIM)
        yc = y - mean
        var = _head_sums(yc * yc, lane_lo) * (1.0 / HEAD_DIM)
        gn = yc * lax.rsqrt(var + GN_EPS) * lng[:, sl] + lnb[:, sl]
        slabs.append((gn + bonus_ref[p]) * g_ref[p])
    rwkv = jnp.concatenate(slabs, axis=1)

    gates = gate_ref[...]
    m = (_sigmoid(gates[:, :d]) * _dot(conv, wa_ref[...])
         + _sigmoid(gates[:, d:2 * d]) * _dot(rwkv, wb_ref[...])
         + _sigmoid(gates[:, 2 * d:]) * _dot(att_ref[...], wc_ref[...]))
    is_ctx = gr < ctx
    o_ref[...] = x_ref[...] + _mod_row(mod_ref, 2, is_ctx) * _dot(m, wo_ref[...])


def _merge(x, mod, proj, att, y, bonus, g, conv_w, ln_g, ln_b, wa, wb, wc, wo, *, ctx, tm):
    bsz, total, d = x.shape
    kern = functools.partial(_merge_kernel, tm=tm, ctx=ctx, total=total)
    before, after = _halo_specs(3 * d, tm, total, P_CONV // (3 * d))
    row = lambda w, col=0: pl.BlockSpec((None, tm, w), lambda b, i: (b, i, col))
    full = lambda shp: pl.BlockSpec(shp, lambda b, i: tuple(0 for _ in shp))
    pm = pl.BlockSpec((None, N_PAIR, tm, LANES), lambda b, i: (b, 0, i, 0))
    pm2 = pl.BlockSpec((2, None, N_PAIR, tm, LANES), lambda b, i: (0, b, 0, i, 0))
    return pl.pallas_call(
        kern,
        grid=(bsz, total // tm),
        in_specs=[row(d), pl.BlockSpec((None, 2, 6, d), lambda b, i: (b, 0, 0, 0)),
                  row(3 * d, P_CONV // (3 * d)), before, after, row(3 * d, P_GATE // (3 * d)), row(d), pm2, pm, pm,
                  full((3, d)), full((1, d)), full((1, d)),
                  full((d, d)), full((d, d)), full((d, d)), full((d, d))],
        out_specs=row(d),
        out_shape=jax.ShapeDtypeStruct((bsz, total, d), F32),
        compiler_params=_cparams(("arbitrary", "arbitrary")),
    )(x, mod, proj, proj, proj, proj, att, y, bonus, g, conv_w, ln_g, ln_b, wa, wb, wc, wo)


def _ffn_down_kernel(x_ref, mod_ref, u_ref, before_ref, after_ref, cw_ref, wd_ref, o_ref, *, tm, ctx, total):
    i = pl.program_id(1)
    gr = _row_ids(i, tm)
    u = u_ref[...]
    prev, nxt = _neighbours(u, before_ref[HALO - 1:HALO, :], after_ref[0:1, :], gr, tm, ctx, total)
    cw = cw_ref[...]
    uc = prev * cw[0:1] + u * cw[1:2] + nxt * cw[2:3]
    ug, uv = uc[:, :D_FF], uc[:, D_FF:]
    act = ug * _sigmoid(ug) * uv
    o_ref[...] = x_ref[...] + _mod_row(mod_ref, 5, gr < ctx) * _dot(act, wd_ref[...])


def _ffn_down(x, mod, u, conv_w, wd, *, ctx, tm):
    bsz, total, d = x.shape
    kern = functools.partial(_ffn_down_kernel, tm=tm, ctx=ctx, total=total)
    before, after = _halo_specs(2 * D_FF, tm, total)
    row = lambda w: pl.BlockSpec((None, tm, w), lambda b, i: (b, i, 0))
    full = lambda shp: pl.BlockSpec(shp, lambda b, i: tuple(0 for _ in shp))
    return pl.pallas_call(
        kern,
        grid=(bsz, total // tm),
        in_specs=[row(d), pl.BlockSpec((None, 2, 6, d), lambda b, i: (b, 0, 0, 0)),
                  row(2 * D_FF), before, after, full((3, 2 * D_FF)), full((D_FF, d))],
        out_specs=row(d),
        out_shape=jax.ShapeDtypeStruct((bsz, total, d), F32),
        compiler_params=_cparams(("arbitrary", "arbitrary")),
    )(x, mod, u, u, u, conv_w, wd)


def _final_norm_kernel(x_ref, g_ref, o_ref):
    x = x_ref[...]
    o_ref[...] = x * lax.rsqrt(jnp.mean(x * x, axis=-1, keepdims=True) + NORM_EPS) * g_ref[...]


def _final_norm(x, g, *, ctx, seq, tm):
    bsz, _, d = x.shape
    off = ctx // tm
    return pl.pallas_call(
        _final_norm_kernel,
        grid=(bsz, seq // tm),
        in_specs=[pl.BlockSpec((None, tm, d), lambda b, i: (b, i + off, 0)),
                  pl.BlockSpec((1, d), lambda b, i: (0, 0))],
        out_specs=pl.BlockSpec((None, tm, d), lambda b, i: (b, i, 0)),
        out_shape=jax.ShapeDtypeStruct((bsz, seq, d), F32),
        compiler_params=_cparams(("arbitrary", "arbitrary")),
    )(x, g)


def _rope_tables(ctx, seq):
    rows = seq // GRID_W
    t_row = jnp.broadcast_to(jnp.arange(rows)[:, None], (rows, GRID_W)).reshape(-1).astype(F32)
    t_col = jnp.broadcast_to(jnp.arange(GRID_W)[None, :], (rows, GRID_W)).reshape(-1).astype(F32)
    n_freq = HEAD_DIM // 4
    inv = ROPE_THETA ** (-jnp.arange(n_freq, dtype=F32) / n_freq)
    ar, ac = t_row[:, None] * inv, t_col[:, None] * inv
    cos = jnp.concatenate([jnp.cos(ar), jnp.cos(ar), jnp.cos(ac), jnp.cos(ac)], axis=1)
    sin = jnp.concatenate([-jnp.sin(ar), jnp.sin(ar), -jnp.sin(ac), jnp.sin(ac)], axis=1)
    cos = jnp.concatenate([jnp.ones((ctx, HEAD_DIM), F32), cos], axis=0)
    sin = jnp.concatenate([jnp.zeros((ctx, HEAD_DIM), F32), sin], axis=0)
    return jnp.tile(cos, (1, LANES // HEAD_DIM)), jnp.tile(sin, (1, LANES // HEAD_DIM))


def _block_diag2(w):
    z = jnp.zeros_like(w[0])
    return jnp.concatenate([jnp.concatenate([w[0], z], axis=1), jnp.concatenate([z, w[1]], axis=1)], axis=0)


def kernel(x, c, ctx, c_ctx, ada_w, ada_b, norm1_g, w_in, conv_a_w, a_out_w, rwkv_mu, rwkv_w0, rwkv_w2,
           rwkv_a0, rwkv_a2, rwkv_g2, rwkv_k_k, rwkv_k_a, rwkv_r_k, rwkv_ln_g, rwkv_ln_b, rwkv_out_w,
           attn_sink, attn_out_w, w_o, norm2_g, ffn_up, ffn_conv, ffn_down, final_norm_g):
    bsz, seq, d = x.shape
    n_ctx = ctx.shape[1]
    depth = ada_w.shape[0]
    total = n_ctx + seq
    assert d == D_MODEL and seq % ATT_BLOCK == 0 and n_ctx % ATT_BLOCK == 0 and bsz + 1 <= HALO
    tm = 256
    assert total % tm == 0 and n_ctx % tm == 0

    cvec = jnp.zeros((HALO, d), F32).at[:bsz].set(c).at[bsz].set(c_ctx)
    mods = _ada(cvec, ada_w.astype(BF16), ada_b)
    mods = mods.reshape(depth, HALO, 6, d)
    mods = jnp.stack([jnp.broadcast_to(mods[:, bsz:bsz + 1], (depth, bsz, 6, d)), mods[:, :bsz]], axis=2)

    cos, sin = _rope_tables(n_ctx, seq)
    masks = _scan_masks()
    xa = jnp.concatenate([ctx, x], axis=1)

    o_lora = 6 * d
    o_qkv = o_lora + LORA_W
    o_gate = o_qkv + d + 2 * KV_W
    assert w_in.shape[2] == N_PROJ == o_gate + 3 * d
    for l in range(depth):
        mod = mods[l]
        w = w_in[l].astype(BF16)
        w = jnp.concatenate([w[:, :o_lora], w[:, o_gate:], w[:, o_qkv:o_gate], w[:, o_lora:o_qkv]], axis=1)
        proj = _nmm(xa, mod, norm1_g[l][None], w, k_shift=0, k_scale=1, ctx=n_ctx, tm=tm, tn=N_PROJ // 3)

        r, v, kn, lw, kd, bd, bonus, g = _prep(
            proj, rwkv_mu[l], rwkv_w0[l].reshape(1, 2 * d), _block_diag2(rwkv_w2[l]).astype(BF16),
            rwkv_a0[l].reshape(1, 2 * d), _block_diag2(rwkv_a2[l]).astype(BF16), rwkv_g2[l].astype(BF16),
            rwkv_k_k[l][None], rwkv_k_a[l][None], rwkv_r_k[l].reshape(1, d), ctx=n_ctx, tm=tm)
        y = _scan(r, v, kn, lw, kd, bd, masks, ctx=n_ctx)
        att = _attention(proj, attn_sink[l], cos, sin, ctx=n_ctx)
        xa = _merge(xa, mod, proj, att, y, bonus, g, conv_a_w[l], rwkv_ln_g[l][None], rwkv_ln_b[l][None],
                    a_out_w[l].astype(BF16), rwkv_out_w[l].astype(BF16), attn_out_w[l].astype(BF16),
                    w_o[l].astype(BF16), ctx=n_ctx, tm=tm)
        u = _nmm(xa, mod, norm2_g[l][None], ffn_up[l].astype(BF16), k_shift=3, k_scale=4, ctx=n_ctx, tm=tm,
                 tn=D_FF)
        xa = _ffn_down(xa, mod, u, ffn_conv[l], ffn_down[l].astype(BF16), ctx=n_ctx, tm=tm)
    return _final_norm(xa, final_norm_g[None], ctx=n_ctx, seq=seq, tm=tm)
```

```python
import functools

import jax
import jax.numpy as jnp
import numpy as np
from jax import lax
from jax.experimental import pallas as pl
from jax.experimental.pallas import tpu as pltpu

D_MODEL = 1024
GRID_W = 64
N_HEADS = 16
HEAD_DIM = 64
N_KV = 4
R_LORA = 64
R_GATE = 128
GN_EPS = 64e-5
WINDOW = 128
ATT_BLOCK = 128
ATT_SCALE = HEAD_DIM ** -0.5
ROPE_THETA = 10000.0
NEG_INF = -1e30
D_FF = 2816
NORM_EPS = 1e-6
KK_EPS = 1e-12
DECAY_SCALE = float(np.exp(-0.5))

LANES = 128
HALO = 8
N_PAIR = D_MODEL // LANES
CHUNK = 64
N_LEVELS = 5
VMEM_LIMIT = 52 * 1024 * 1024

BF16 = jnp.bfloat16
F32 = jnp.float32

KV_W = N_KV * HEAD_DIM
LORA_W = 2 * R_LORA + 2 * R_LORA + R_GATE
P_CONV, P_RKV, P_GATE = 0, 3 * D_MODEL, 6 * D_MODEL
P_Q = 9 * D_MODEL
P_K = P_Q + D_MODEL
P_V = P_K + KV_W
P_LORA = P_V + KV_W
N_PROJ = P_LORA + LORA_W
assert P_K % KV_W == 0 and P_LORA % LORA_W == 0 and N_PROJ % (3 * LANES) == 0


def _cparams(sem):
    return pltpu.CompilerParams(dimension_semantics=sem, vmem_limit_bytes=VMEM_LIMIT)


def _dot(a, b):
    return jnp.dot(a.astype(BF16), b.astype(BF16), preferred_element_type=F32)


def _dot_nt(a, b):
    return lax.dot_general(a.astype(BF16), b.astype(BF16), (((1,), (1,)), ((), ())),
                           preferred_element_type=F32)


def _bmm(a, b):
    return jnp.einsum('pmk,pkn->pmn', a.astype(BF16), b.astype(BF16), preferred_element_type=F32)


def _bmm_nt(a, b):
    return jnp.einsum('pmk,pnk->pmn', a.astype(BF16), b.astype(BF16), preferred_element_type=F32)


def _bmm_tn(a, b):
    return jnp.einsum('pkm,pkn->pmn', a.astype(BF16), b.astype(BF16), preferred_element_type=F32)


def _sigmoid(x):
    return 1.0 / (1.0 + jnp.exp(-x))


def _row_ids(i, tm):
    return i * tm + lax.broadcasted_iota(jnp.int32, (tm, 1), 0)


def _mod_row(mod_ref, k, is_ctx):
    return jnp.where(is_ctx, mod_ref[0, k:k + 1, :], mod_ref[1, k:k + 1, :])


def _neighbours(z, before, after, i, tm, ctx, total):
    start = i * tm
    keep_before = jnp.where((start == 0) | (start == ctx), 0.0, 1.0)
    keep_after = jnp.where((start + tm == ctx) | (start + tm == total), 0.0, 1.0)
    row = lax.broadcasted_iota(jnp.int32, (HALO, 1), 0)
    prev = pltpu.roll(z, 1, 0)
    nxt = pltpu.roll(z, tm - 1, 0)
    top = jnp.where(row == 0, before * keep_before, prev[:HALO])
    bottom = jnp.where(row == HALO - 1, after * keep_after, nxt[tm - HALO:])
    return (jnp.concatenate([top, prev[HALO:]], axis=0),
            jnp.concatenate([nxt[:tm - HALO], bottom], axis=0))


def _halo_specs(width, tm, total, col=0):
    per = tm // HALO
    nblk = total // HALO
    before = pl.BlockSpec((None, HALO, width), lambda b, i: (b, jnp.maximum(i * per - 1, 0), col))
    after = pl.BlockSpec((None, HALO, width), lambda b, i: (b, jnp.minimum((i + 1) * per, nblk - 1), col))
    return before, after


def _ada_kernel(c_ref, w_ref, b_ref, o_ref):
    c = c_ref[...]
    o_ref[...] = _dot(c * _sigmoid(c), w_ref[...]) + b_ref[...]


def _ada(cvec, ada_w, ada_b):
    depth, d, n = ada_w.shape
    tn = 1536
    return pl.pallas_call(
        _ada_kernel,
        grid=(depth, n // tn),
        in_specs=[pl.BlockSpec((HALO, d), lambda l, j: (0, 0)),
                  pl.BlockSpec((None, d, tn), lambda l, j: (l, 0, j)),
                  pl.BlockSpec((None, 1, tn), lambda l, j: (l, 0, j))],
        out_specs=pl.BlockSpec((None, HALO, tn), lambda l, j: (l, 0, j)),
        out_shape=jax.ShapeDtypeStruct((depth, HALO, n), F32),
        compiler_params=_cparams(("arbitrary", "arbitrary")),
    )(cvec, ada_w, ada_b.reshape(depth, 1, n))


def _nmm_kernel(x_ref, mod_ref, g_ref, w_ref, o_ref, *, k_shift, k_scale, tm, ctx):
    i = pl.program_id(2)
    x = x_ref[...]
    y = x * lax.rsqrt(jnp.mean(x * x, axis=-1, keepdims=True) + NORM_EPS) * g_ref[...]
    is_ctx = _row_ids(i, tm) < ctx
    h = y * (1.0 + _mod_row(mod_ref, k_scale, is_ctx)) + _mod_row(mod_ref, k_shift, is_ctx)
    o_ref[...] = _dot(h, w_ref[...])


def _nmm(x, mod, g, w, *, k_shift, k_scale, ctx, tm, tn):
    bsz, total, d = x.shape
    n = w.shape[1]
    kern = functools.partial(_nmm_kernel, k_shift=k_shift, k_scale=k_scale, tm=tm, ctx=ctx)
    return pl.pallas_call(
        kern,
        grid=(n // tn, bsz, total // tm),
        in_specs=[pl.BlockSpec((None, tm, d), lambda j, b, i: (b, i, 0)),
                  pl.BlockSpec((None, 2, 6, d), lambda j, b, i: (b, 0, 0, 0)),
                  pl.BlockSpec((1, d), lambda j, b, i: (0, 0)),
                  pl.BlockSpec((d, tn), lambda j, b, i: (0, j))],
        out_specs=pl.BlockSpec((None, tm, tn), lambda j, b, i: (b, i, j)),
        out_shape=jax.ShapeDtypeStruct((bsz, total, n), F32),
        compiler_params=_cparams(("arbitrary", "arbitrary", "arbitrary")),
    )(x, mod, g, w)


def _head_sums(z, lane_lo):
    s_lo = jnp.sum(jnp.where(lane_lo, z, 0.0), axis=-1, keepdims=True)
    s_hi = jnp.sum(jnp.where(lane_lo, 0.0, z), axis=-1, keepdims=True)
    return jnp.where(lane_lo, s_lo, s_hi)


def _chunk_cumsum(tri, z):
    hi = z.astype(BF16)
    rest = z - hi.astype(F32)
    mid = rest.astype(BF16)
    low = (rest - mid.astype(F32)).astype(BF16)
    dot = functools.partial(jnp.dot, preferred_element_type=F32)
    return dot(tri, low) + dot(tri, mid) + dot(tri, hi)


def _prep_kernel(rkv_ref, before_ref, after_ref, lora_ref, tri_ref, mu_ref, w0_ref, w2_ref, a0_ref, a2_ref,
                 g2_ref, kk_ref, ka_ref, rk_ref,
                 at_out, rt_out, bt_out, kt_out, gend_out, v_out, bonus_out, g_out, *, tm, ctx, total):
    i = pl.program_id(1)
    x = rkv_ref[...]
    prev, nxt = _neighbours(x, before_ref[HALO - 1:HALO, :], after_ref[0:1, :], i, tm, ctx, total)
    mu_prev = mu_ref[0:1, :]
    mu_next = mu_ref[1:2, :]
    s = prev * mu_prev + x * (1.0 - mu_prev - mu_next) + nxt * mu_next
    d = D_MODEL
    r, k, v = s[:, :d], s[:, d:2 * d], s[:, 2 * d:]

    lo = lora_ref[...]
    w_pre = w0_ref[...] + _dot(jnp.tanh(lo[:, :LANES]), w2_ref[...])
    lw = -DECAY_SCALE * _sigmoid(w_pre)
    a = _sigmoid(a0_ref[...] + _dot(lo[:, LANES:2 * LANES], a2_ref[...]))
    g = _dot(_sigmoid(lo[:, 2 * LANES:]), g2_ref[...])
    kk = k * kk_ref[...]
    ka = ka_ref[...]
    rk = rk_ref[...]
    cl = [_chunk_cumsum(tri_ref[dr], lw[:, d * dr:d * (dr + 1)]) for dr in range(2)]

    lane_lo = lax.broadcasted_iota(jnp.int32, (1, LANES), 1) < HEAD_DIM
    for p in range(N_PAIR):
        sl = slice(LANES * p, LANES * (p + 1))
        kkp = kk[:, sl]
        kn = kkp * lax.rsqrt(_head_sums(kkp * kkp, lane_lo) + KK_EPS)
        rp, kp, vp = r[:, sl], k[:, sl], v[:, sl]
        bonus = jnp.zeros_like(rp)
        for dr in range(2):
            dsl = slice(d * dr + LANES * p, d * dr + LANES * (p + 1))
            ad = a[:, dsl]
            kd = kp * (1.0 + (ad - 1.0) * ka[:, sl])
            bonus = bonus + _head_sums(rp * kd * rk[:, sl], lane_lo) * vp
            cl_incl = cl[dr][:, sl]
            inv_g = jnp.exp(-cl_incl)
            at_out[dr, p] = (-kn * jnp.exp(cl_incl - lw[:, dsl])).astype(BF16)
            rt_out[dr, p] = (rp * jnp.exp(cl_incl)).astype(BF16)
            bt_out[dr, p] = (kn * ad * inv_g).astype(BF16)
            kt_out[dr, p] = (kd * inv_g).astype(BF16)
            for q in range(tm // CHUNK):
                end = CHUNK * q + (CHUNK - 1 if dr == 0 else 0)
                gend_out[dr, q, p:p + 1, :] = jnp.exp(cl_incl[end:end + 1, :])
        v_out[p] = vp.astype(BF16)
        bonus_out[p] = bonus.astype(BF16)
        g_out[p] = g[:, sl].astype(BF16)


def _prep(proj, tri, mu, w0, w2bd, a0, a2bd, g2, k_k, k_a, r_k, *, ctx, tm):
    bsz, total, _ = proj.shape
    d = D_MODEL
    kern = functools.partial(_prep_kernel, tm=tm, ctx=ctx, total=total)
    before, after = _halo_specs(3 * d, tm, total, P_RKV // (3 * d))
    full = lambda shp: pl.BlockSpec(shp, lambda b, i: tuple(0 for _ in shp))
    pm = pl.BlockSpec((None, N_PAIR, tm, LANES), lambda b, i: (b, 0, i, 0))
    pm2 = pl.BlockSpec((2, None, N_PAIR, tm, LANES), lambda b, i: (0, b, 0, i, 0))
    ge = pl.BlockSpec((2, None, tm // CHUNK, N_PAIR, LANES), lambda b, i: (0, b, i, 0, 0))
    sds = jax.ShapeDtypeStruct((bsz, N_PAIR, total, LANES), BF16)
    sds2 = jax.ShapeDtypeStruct((2, bsz, N_PAIR, total, LANES), BF16)
    sds_ge = jax.ShapeDtypeStruct((2, bsz, total // CHUNK, N_PAIR, LANES), F32)
    return pl.pallas_call(
        kern,
        grid=(bsz, total // tm),
        in_specs=[pl.BlockSpec((None, tm, 3 * d), lambda b, i: (b, i, P_RKV // (3 * d))), before, after,
                  pl.BlockSpec((None, tm, LORA_W), lambda b, i: (b, i, P_LORA // LORA_W)),
                  full((2, tm, tm)), full((2, 3 * d)), full((1, 2 * d)), full((LANES, 2 * d)), full((1, 2 * d)),
                  full((LANES, 2 * d)), full((R_GATE, d)), full((1, d)), full((1, d)), full((1, d))],
        out_specs=[pm2, pm2, pm2, pm2, ge, pm, pm, pm],
        out_shape=[sds2, sds2, sds2, sds2, sds_ge, sds, sds, sds],
        compiler_params=_cparams(("arbitrary", "arbitrary")),
    )(proj, proj, proj, proj, tri, mu, w0, w2bd, a0, a2bd, g2, k_k, k_a, r_k)


M_STRICT, M_INCL, M_EYE, M_LEVEL0 = 0, 1, 2, 3
N_MASKS = M_LEVEL0 + 1 + N_LEVELS


def _scan_masks(tm):
    c = CHUNK
    t = np.arange(c)[:, None]
    s = np.arange(c)[None, :]
    cum = np.zeros((2, tm, tm), np.float32)
    msk = np.zeros((2, N_MASKS, c, 2 * c), np.float32)
    for dr in range(2):
        earlier = (s < t) if dr == 0 else (s > t)
        cum[dr] = np.kron(np.eye(tm // c), earlier | (s == t))
        planes = [earlier, earlier | (s == t), s == t]
        m = 1
        while m < c:
            same = (t // (2 * m)) == (s // (2 * m))
            t_late = (t % (2 * m)) >= m
            s_late = (s % (2 * m)) >= m
            planes.append(same & (t_late & ~s_late if dr == 0 else ~t_late & s_late))
            m *= 2
        for q, plane in enumerate(planes):
            msk[dr, q] = np.concatenate([plane, plane], axis=1)
    bdm = np.kron(np.eye(2, dtype=np.float32), np.ones((c, c), np.float32))
    return jnp.asarray(cum, BF16), jnp.asarray(msk), jnp.asarray(bdm)


def _scan_kernel(msk_ref, bdm_ref, at_ref, rt_ref, bt_ref, kt_ref, gend_ref, v_ref, y_ref, s_ref):
    c = CHUNK
    bsz = v_ref.shape[0]
    n = bsz * N_PAIR

    @pl.when(pl.program_id(1) == 0)
    def _():
        s_ref[...] = jnp.zeros_like(s_ref)

    bdm = bdm_ref[...]
    bdm_bf = bdm.astype(BF16)
    m_strict = msk_ref[M_STRICT]
    m_incl = msk_ref[M_INCL]

    def load(ref):
        return ref[...].reshape(n, c, LANES)

    def stack(z):
        z = z.astype(BF16)
        return jnp.concatenate([z, z], axis=1) * bdm_bf

    bt, kt, v = load(bt_ref), load(kt_ref), load(v_ref)
    ar = jnp.concatenate([load(at_ref), load(rt_ref)], axis=1)
    g_end = jnp.stack([gend_ref[q // N_PAIR, q % N_PAIR:q % N_PAIR + 1, :] for q in range(n)], axis=0)
    state = s_ref[...]

    gram = _bmm_nt(ar, jnp.concatenate([stack(bt), stack(kt)], axis=1))
    g_ab = gram[:, :c, :2 * c]
    from_v = jnp.concatenate([gram[:, :c, 2 * c:] * m_strict, gram[:, c:, 2 * c:] * m_incl], axis=1)
    partial = _bmm_nt(ar, state) + _bmm(from_v, stack(v))

    tri = msk_ref[M_EYE] + g_ab * msk_ref[M_LEVEL0]
    for lvl in range(1, N_LEVELS + 1):
        w = _bmm(g_ab * msk_ref[M_LEVEL0 + lvl], stack(tri))
        tri = tri + _bmm(tri, stack(w))

    u = _bmm(tri, stack(partial[:, :c]))
    y = partial[:, c:] + _bmm(gram[:, c:, :2 * c] * m_incl, stack(u))
    y_ref[...] = y.reshape(bsz, N_PAIR, c, LANES)
    upd = _bmm_tn(jnp.concatenate([u.astype(BF16), v], axis=1), jnp.concatenate([bt, kt], axis=1))
    s_ref[...] = (state + upd * bdm) * g_end


def _scan(at, rt, bt, kt, gend, v, masks, *, ctx):
    bsz, _, total, _ = v.shape
    c = CHUNK
    nc, ncc = total // c, ctx // c
    _, msk, bdm = masks

    def chunk(dr, j):
        rev = jnp.where(j < ncc, ncc - 1 - j, nc + ncc - 1 - j)
        return jnp.where(dr == 0, j, rev)

    shared = pl.BlockSpec((bsz, N_PAIR, c, LANES), lambda dr, j: (0, 0, chunk(dr, j), 0))
    per_dir = pl.BlockSpec((None, bsz, N_PAIR, c, LANES), lambda dr, j: (dr, 0, 0, chunk(dr, j), 0))
    ends = pl.BlockSpec((None, bsz, None, N_PAIR, LANES), lambda dr, j: (dr, 0, chunk(dr, j), 0, 0))
    return pl.pallas_call(
        _scan_kernel,
        grid=(2, nc),
        in_specs=[pl.BlockSpec((None, N_MASKS, c, 2 * c), lambda dr, j: (dr, 0, 0, 0)),
                  pl.BlockSpec((2 * c, 2 * c), lambda dr, j: (0, 0)),
                  per_dir, per_dir, per_dir, per_dir, ends, shared],
        out_specs=per_dir,
        out_shape=jax.ShapeDtypeStruct((2, bsz, N_PAIR, total, LANES), F32),
        scratch_shapes=[pltpu.VMEM((bsz * N_PAIR, 2 * HEAD_DIM, 2 * HEAD_DIM), F32)],
        compiler_params=_cparams(("arbitrary", "arbitrary")),
    )(msk, bdm, at, rt, bt, kt, gend, v)


def _rope(z, cos, sin):
    lane = lax.broadcasted_iota(jnp.int32, (1, LANES), 1)
    first = jnp.bitwise_and(lane, HEAD_DIM // 2 - 1) < (HEAD_DIM // 4)
    out = []
    for p in range(z.shape[1] // LANES):
        zp = z[:, LANES * p:LANES * (p + 1)]
        partner = jnp.where(first, pltpu.roll(zp, LANES - HEAD_DIM // 4, 1), pltpu.roll(zp, HEAD_DIM // 4, 1))
        out.append(zp * cos + partner * sin)
    return jnp.concatenate(out, axis=1)


def _attn_kernel(sink_ref, q_ref, kp_ref, kc_ref, kn_ref, kx_ref, vp_ref, vc_ref, vn_ref, vx_ref,
                 cq_ref, sq_ref, cp_ref, sp_ref, cn_ref, sn_ref, o_ref, *, n_ctx_blk, n_blk):
    i = pl.program_id(1)
    blk = ATT_BLOCK
    q = _rope(q_ref[...], cq_ref[...], sq_ref[...]) * ATT_SCALE
    k_all = jnp.concatenate([_rope(kp_ref[...], cp_ref[...], sp_ref[...]),
                             _rope(kc_ref[...], cq_ref[...], sq_ref[...]),
                             _rope(kn_ref[...], cn_ref[...], sn_ref[...]),
                             kx_ref[...]], axis=0)
    v_all = jnp.concatenate([vp_ref[...], vc_ref[...], vn_ref[...], vx_ref[...]], axis=0)
    n_keys = k_all.shape[0]

    qi = lax.broadcasted_iota(jnp.int32, (blk, n_keys), 0)
    si = lax.broadcasted_iota(jnp.int32, (blk, n_keys), 1)
    rel = si - blk - qi
    key_blk = i - 1 + jnp.right_shift(si, 7)
    ok = (jnp.where(jnp.abs(rel) <= WINDOW, 1, 0) * jnp.where(key_blk >= n_ctx_blk, 1, 0)
          * jnp.where(key_blk < n_blk, 1, 0) * jnp.where(i >= n_ctx_blk, 1, 0))
    ok = jnp.where(si >= 3 * blk, 1, ok)
    bias = jnp.where(ok > 0, 0.0, NEG_INF)

    lane_lo = lax.broadcasted_iota(jnp.int32, (1, LANES), 1) < HEAD_DIM
    group = N_HEADS // N_KV
    outs = [None] * N_HEADS
    for kvh in range(N_KV):
        kv_slab = slice(LANES * (kvh // 2), LANES * (kvh // 2 + 1))
        kv_lo = kvh % 2 == 0
        rows, sinks = [], []
        for hq in range(group):
            h = kvh * group + hq
            qp = q[:, LANES * (h // 2):LANES * (h // 2 + 1)]
            if (h % 2 == 0) != kv_lo:
                qp = pltpu.roll(qp, HEAD_DIM, 1)
            rows.append(jnp.where(lane_lo, qp, 0.0) if kv_lo else jnp.where(lane_lo, 0.0, qp))
            sinks.append(jnp.full((blk, 1), sink_ref[h], F32))
        qg = jnp.concatenate(rows, axis=0)
        sink = jnp.concatenate(sinks, axis=0)
        s = _dot_nt(qg, k_all[:, kv_slab]) + jnp.concatenate([bias] * group, axis=0)
        m = jnp.maximum(jnp.max(s, axis=-1, keepdims=True), sink)
        e = jnp.exp(s - m)
        denom = jnp.sum(e, axis=-1, keepdims=True) + jnp.exp(sink - m)
        o = _dot(e, v_all[:, kv_slab]) / denom
        for hq in range(group):
            h = kvh * group + hq
            oh = o[hq * blk:(hq + 1) * blk]
            if (h % 2 == 0) != kv_lo:
                oh = pltpu.roll(oh, HEAD_DIM, 1)
            outs[h] = oh
    o_ref[...] = jnp.concatenate(
        [jnp.where(lane_lo, outs[2 * p], outs[2 * p + 1]) for p in range(N_PAIR)], axis=1)


def _attention(qkv, sink, cos, sin, *, ctx):
    bsz, total, _ = qkv.shape
    blk = ATT_BLOCK
    n_blk, n_ctx_blk = total // blk, ctx // blk
    kvw = KV_W
    kcol, vcol = P_K // kvw, P_V // kvw
    kern = functools.partial(_attn_kernel, n_ctx_blk=n_ctx_blk, n_blk=n_blk)
    prev_i = lambda i: jnp.maximum(i - 1, 0)
    next_i = lambda i: jnp.minimum(i + 1, n_blk - 1)

    def kv(col, row):
        return pl.BlockSpec((None, blk, kvw), lambda b, i: (b, row(i), col))

    def kv_ctx(col):
        return pl.BlockSpec((None, ctx, kvw), lambda b, i: (b, 0, col))

    def tab(row):
        return pl.BlockSpec((blk, LANES), lambda b, i: (row(i), 0))

    same = lambda i: i
    return pl.pallas_call(
        kern,
        grid=(bsz, n_blk),
        in_specs=[pl.BlockSpec(memory_space=pltpu.SMEM),
                  pl.BlockSpec((None, blk, D_MODEL), lambda b, i: (b, i, P_Q // D_MODEL)),
                  kv(kcol, prev_i), kv(kcol, same), kv(kcol, next_i), kv_ctx(kcol),
                  kv(vcol, prev_i), kv(vcol, same), kv(vcol, next_i), kv_ctx(vcol),
                  tab(same), tab(same), tab(prev_i), tab(prev_i), tab(next_i), tab(next_i)],
        out_specs=pl.BlockSpec((None, blk, D_MODEL), lambda b, i: (b, i, 0)),
        out_shape=jax.ShapeDtypeStruct((bsz, total, D_MODEL), F32),
        compiler_params=_cparams(("arbitrary", "arbitrary")),
    )(sink, qkv, qkv, qkv, qkv, qkv, qkv, qkv, qkv, qkv, cos, sin, cos, sin, cos, sin)


def _merge_kernel(x_ref, mod_ref, cv_ref, before_ref, after_ref, gate_ref, att_ref, y_ref, bonus_ref, g_ref,
                  cw_ref, lng_ref, lnb_ref, wa_ref, wb_ref, wc_ref, wo_ref, o_ref, *, tm, ctx, total):
    i = pl.program_id(1)
    gr = _row_ids(i, tm)
    d = D_MODEL
    cv = cv_ref[...]
    z = cv[:, d:2 * d] * cv[:, 2 * d:]
    z_before = before_ref[HALO - 1:HALO, d:2 * d] * before_ref[HALO - 1:HALO, 2 * d:]
    z_after = after_ref[0:1, d:2 * d] * after_ref[0:1, 2 * d:]
    prev, nxt = _neighbours(z, z_before, z_after, i, tm, ctx, total)
    cw = cw_ref[...]
    conv = cv[:, :d] * (prev * cw[0:1] + z * cw[1:2] + nxt * cw[2:3])

    lane_lo = lax.broadcasted_iota(jnp.int32, (1, LANES), 1) < HEAD_DIM
    lng = lng_ref[...]
    lnb = lnb_ref[...]
    slabs = []
    for p in range(N_PAIR):
        sl = slice(LANES * p, LANES * (p + 1))
        y = y_ref[0, p] + y_ref[1, p]
        mean = _head_sums(y, lane_lo) * (1.0 / HEAD_DIM)
        yc = y - mean
        var = _head_sums(yc * yc, lane_lo) * (1.0 / HEAD_DIM)
        gn = yc * lax.rsqrt(var + GN_EPS) * lng[:, sl] + lnb[:, sl]
        slabs.append((gn + bonus_ref[p]) * g_ref[p])
    rwkv = jnp.concatenate(slabs, axis=1)

    gates = gate_ref[...]
    m = (_sigmoid(gates[:, :d]) * _dot(conv, wa_ref[...])
         + _sigmoid(gates[:, d:2 * d]) * _dot(rwkv, wb_ref[...])
         + _sigmoid(gates[:, 2 * d:]) * _dot(att_ref[...], wc_ref[...]))
    is_ctx = gr < ctx
    o_ref[...] = x_ref[...] + _mod_row(mod_ref, 2, is_ctx) * _dot(m, wo_ref[...])


def _merge(x, mod, proj, att, y, bonus, g, conv_w, ln_g, ln_b, wa, wb, wc, wo, *, ctx, tm):
    bsz, total, d = x.shape
    kern = functools.partial(_merge_kernel, tm=tm, ctx=ctx, total=total)
    before, after = _halo_specs(3 * d, tm, total, P_CONV // (3 * d))
    row = lambda w, col=0: pl.BlockSpec((None, tm, w), lambda b, i: (b, i, col))
    full = lambda shp: pl.BlockSpec(shp, lambda b, i: tuple(0 for _ in shp))
    pm = pl.BlockSpec((None, N_PAIR, tm, LANES), lambda b, i: (b, 0, i, 0))
    pm2 = pl.BlockSpec((2, None, N_PAIR, tm, LANES), lambda b, i: (0, b, 0, i, 0))
    return pl.pallas_call(
        kern,
        grid=(bsz, total // tm),
        in_specs=[row(d), pl.BlockSpec((None, 2, 6, d), lambda b, i: (b, 0, 0, 0)),
                  row(3 * d, P_CONV // (3 * d)), before, after, row(3 * d, P_GATE // (3 * d)), row(d), pm2, pm, pm,
                  full((3, d)), full((1, d)), full((1, d)),
                  full((d, d)), full((d, d)), full((d, d)), full((d, d))],
        out_specs=row(d),
        out_shape=jax.ShapeDtypeStruct((bsz, total, d), F32),
        compiler_params=_cparams(("arbitrary", "arbitrary")),
    )(x, mod, proj, proj, proj, proj, att, y, bonus, g, conv_w, ln_g, ln_b, wa, wb, wc, wo)


def _ffn_down_kernel(x_ref, mod_ref, u_ref, before_ref, after_ref, cw_ref, wd_ref, o_ref, *, tm, ctx, total):
    i = pl.program_id(1)
    gr = _row_ids(i, tm)
    u = u_ref[...]
    prev, nxt = _neighbours(u, before_ref[HALO - 1:HALO, :], after_ref[0:1, :], i, tm, ctx, total)
    cw = cw_ref[...]
    uc = prev * cw[0:1] + u * cw[1:2] + nxt * cw[2:3]
    ug, uv = uc[:, :D_FF], uc[:, D_FF:]
    act = ug * _sigmoid(ug) * uv
    o_ref[...] = x_ref[...] + _mod_row(mod_ref, 5, gr < ctx) * _dot(act, wd_ref[...])


def _ffn_down(x, mod, u, conv_w, wd, *, ctx, tm):
    bsz, total, d = x.shape
    kern = functools.partial(_ffn_down_kernel, tm=tm, ctx=ctx, total=total)
    before, after = _halo_specs(2 * D_FF, tm, total)
    row = lambda w: pl.BlockSpec((None, tm, w), lambda b, i: (b, i, 0))
    full = lambda shp: pl.BlockSpec(shp, lambda b, i: tuple(0 for _ in shp))
    return pl.pallas_call(
        kern,
        grid=(bsz, total // tm),
        in_specs=[row(d), pl.BlockSpec((None, 2, 6, d), lambda b, i: (b, 0, 0, 0)),
                  row(2 * D_FF), before, after, full((3, 2 * D_FF)), full((D_FF, d))],
        out_specs=row(d),
        out_shape=jax.ShapeDtypeStruct((bsz, total, d), F32),
        compiler_params=_cparams(("arbitrary", "arbitrary")),
    )(x, mod, u, u, u, conv_w, wd)


def _final_norm_kernel(x_ref, g_ref, o_ref):
    x = x_ref[...]
    o_ref[...] = x * lax.rsqrt(jnp.mean(x * x, axis=-1, keepdims=True) + NORM_EPS) * g_ref[...]


def _final_norm(x, g, *, ctx, seq, tm):
    bsz, _, d = x.shape
    off = ctx // tm
    return pl.pallas_call(
        _final_norm_kernel,
        grid=(bsz, seq // tm),
        in_specs=[pl.BlockSpec((None, tm, d), lambda b, i: (b, i + off, 0)),
                  pl.BlockSpec((1, d), lambda b, i: (0, 0))],
        out_specs=pl.BlockSpec((None, tm, d), lambda b, i: (b, i, 0)),
        out_shape=jax.ShapeDtypeStruct((bsz, seq, d), F32),
        compiler_params=_cparams(("arbitrary", "arbitrary")),
    )(x, g)


def _rope_tables(ctx, seq):
    rows = seq // GRID_W
    t_row = jnp.broadcast_to(jnp.arange(rows)[:, None], (rows, GRID_W)).reshape(-1).astype(F32)
    t_col = jnp.broadcast_to(jnp.arange(GRID_W)[None, :], (rows, GRID_W)).reshape(-1).astype(F32)
    n_freq = HEAD_DIM // 4
    inv = ROPE_THETA ** (-jnp.arange(n_freq, dtype=F32) / n_freq)
    ar, ac = t_row[:, None] * inv, t_col[:, None] * inv
    cos = jnp.concatenate([jnp.cos(ar), jnp.cos(ar), jnp.cos(ac), jnp.cos(ac)], axis=1)
    sin = jnp.concatenate([-jnp.sin(ar), jnp.sin(ar), -jnp.sin(ac), jnp.sin(ac)], axis=1)
    cos = jnp.concatenate([jnp.ones((ctx, HEAD_DIM), F32), cos], axis=0)
    sin = jnp.concatenate([jnp.zeros((ctx, HEAD_DIM), F32), sin], axis=0)
    return jnp.tile(cos, (1, LANES // HEAD_DIM)), jnp.tile(sin, (1, LANES // HEAD_DIM))


def _block_diag2(w):
    z = jnp.zeros_like(w[0])
    return jnp.concatenate([jnp.concatenate([w[0], z], axis=1), jnp.concatenate([z, w[1]], axis=1)], axis=0)


def kernel(x, c, ctx, c_ctx, ada_w, ada_b, norm1_g, w_in, conv_a_w, a_out_w, rwkv_mu, rwkv_w0, rwkv_w2,
           rwkv_a0, rwkv_a2, rwkv_g2, rwkv_k_k, rwkv_k_a, rwkv_r_k, rwkv_ln_g, rwkv_ln_b, rwkv_out_w,
           attn_sink, attn_out_w, w_o, norm2_g, ffn_up, ffn_conv, ffn_down, final_norm_g):
    bsz, seq, d = x.shape
    n_ctx = ctx.shape[1]
    depth = ada_w.shape[0]
    total = n_ctx + seq
    assert d == D_MODEL and seq % ATT_BLOCK == 0 and n_ctx % ATT_BLOCK == 0 and bsz + 1 <= HALO
    tm = 256
    assert total % tm == 0 and n_ctx % tm == 0

    cvec = jnp.zeros((HALO, d), F32).at[:bsz].set(c).at[bsz].set(c_ctx)
    mods = _ada(cvec, ada_w.astype(BF16), ada_b)
    mods = mods.reshape(depth, HALO, 6, d)
    mods = jnp.stack([jnp.broadcast_to(mods[:, bsz:bsz + 1], (depth, bsz, 6, d)), mods[:, :bsz]], axis=2)

    cos, sin = _rope_tables(n_ctx, seq)
    masks = _scan_masks(tm)
    xa = jnp.concatenate([ctx, x], axis=1)

    o_lora = 6 * d
    o_qkv = o_lora + LORA_W
    o_gate = o_qkv + d + 2 * KV_W
    assert w_in.shape[2] == N_PROJ == o_gate + 3 * d
    for l in range(depth):
        mod = mods[l]
        w = w_in[l].astype(BF16)
        w = jnp.concatenate([w[:, :o_lora], w[:, o_gate:], w[:, o_qkv:o_gate], w[:, o_lora:o_qkv]], axis=1)
        proj = _nmm(xa, mod, norm1_g[l][None], w, k_shift=0, k_scale=1, ctx=n_ctx, tm=tm, tn=N_PROJ // 3)

        at, rt, bt, kt, gend, v, bonus, g = _prep(
            proj, masks[0], rwkv_mu[l], rwkv_w0[l].reshape(1, 2 * d), _block_diag2(rwkv_w2[l]).astype(BF16),
            rwkv_a0[l].reshape(1, 2 * d), _block_diag2(rwkv_a2[l]).astype(BF16), rwkv_g2[l].astype(BF16),
            rwkv_k_k[l][None], rwkv_k_a[l][None], rwkv_r_k[l].reshape(1, d), ctx=n_ctx, tm=tm)
        y = _scan(at, rt, bt, kt, gend, v, masks, ctx=n_ctx)
        att = _attention(proj, attn_sink[l], cos, sin, ctx=n_ctx)
        xa = _merge(xa, mod, proj, att, y, bonus, g, conv_a_w[l], rwkv_ln_g[l][None], rwkv_ln_b[l][None],
                    a_out_w[l].astype(BF16), rwkv_out_w[l].astype(BF16), attn_out_w[l].astype(BF16),
                    w_o[l].astype(BF16), ctx=n_ctx, tm=tm)
        u = _nmm(xa, mod, norm2_g[l][None], ffn_up[l].astype(BF16), k_shift=3, k_scale=4, ctx=n_ctx, tm=tm,
                 tn=D_FF)
        xa = _ffn_down(xa, mod, u, ffn_conv[l], ffn_down[l].astype(BF16), ctx=n_ctx, tm=tm)
    return _final_norm(xa, final_norm_g[None], ctx=n_ctx, seq=seq, tm=tm)
```

```python
import functools

import jax
import jax.numpy as jnp
import numpy as np
from jax import lax
from jax.experimental import pallas as pl
from jax.experimental.pallas import tpu as pltpu

D_MODEL = 1024
GRID_W = 64
N_HEADS = 16
HEAD_DIM = 64
N_KV = 4
R_LORA = 64
R_GATE = 128
GN_EPS = 64e-5
WINDOW = 128
ATT_BLOCK = 128
ATT_SCALE = HEAD_DIM ** -0.5
ROPE_THETA = 10000.0
NEG_INF = -1e30
D_FF = 2816
NORM_EPS = 1e-6
KK_EPS = 1e-12
DECAY_SCALE = float(np.exp(-0.5))

LANES = 128
HALO = 8
HALO_BF16 = 16
N_PAIR = D_MODEL // LANES
CHUNK = 64
N_LEVELS = 5
VMEM_LIMIT = 52 * 1024 * 1024

BF16 = jnp.bfloat16
F32 = jnp.float32

KV_W = N_KV * HEAD_DIM
LORA_W = 2 * R_LORA + 2 * R_LORA + R_GATE
P_CONV, P_RKV, P_GATE = 0, 3 * D_MODEL, 6 * D_MODEL
P_Q = 9 * D_MODEL
P_K = P_Q + D_MODEL
P_V = P_K + KV_W
P_LORA = P_V + KV_W
N_PROJ = P_LORA + LORA_W
assert P_K % KV_W == 0 and P_LORA % LORA_W == 0 and N_PROJ % (3 * LANES) == 0


def _cparams(sem):
    return pltpu.CompilerParams(dimension_semantics=sem, vmem_limit_bytes=VMEM_LIMIT)


def _dot(a, b):
    return jnp.dot(a.astype(BF16), b.astype(BF16), preferred_element_type=F32)


def _dot_nt(a, b):
    return lax.dot_general(a.astype(BF16), b.astype(BF16), (((1,), (1,)), ((), ())),
                           preferred_element_type=F32)


def _bmm(a, b):
    return jnp.einsum('pmk,pkn->pmn', a.astype(BF16), b.astype(BF16), preferred_element_type=F32)


def _bmm_nt(a, b):
    return jnp.einsum('pmk,pnk->pmn', a.astype(BF16), b.astype(BF16), preferred_element_type=F32)


def _bmm_tn(a, b):
    return jnp.einsum('pkm,pkn->pmn', a.astype(BF16), b.astype(BF16), preferred_element_type=F32)


def _sigmoid(x):
    return 1.0 / (1.0 + jnp.exp(-x))


def _row_ids(i, tm):
    return i * tm + lax.broadcasted_iota(jnp.int32, (tm, 1), 0)


def _mod_row(mod_ref, k, is_ctx):
    return jnp.where(is_ctx, mod_ref[0, k:k + 1, :], mod_ref[1, k:k + 1, :])


def _neighbours(z, before, after, i, tm, ctx, total):
    start = i * tm
    keep_before = jnp.where((start == 0) | (start == ctx), 0.0, 1.0)
    keep_after = jnp.where((start + tm == ctx) | (start + tm == total), 0.0, 1.0)
    row = lax.broadcasted_iota(jnp.int32, (HALO, 1), 0)
    prev = pltpu.roll(z, 1, 0)
    nxt = pltpu.roll(z, tm - 1, 0)
    top = jnp.where(row == 0, before * keep_before, prev[:HALO])
    bottom = jnp.where(row == HALO - 1, after * keep_after, nxt[tm - HALO:])
    return (jnp.concatenate([top, prev[HALO:]], axis=0),
            jnp.concatenate([nxt[:tm - HALO], bottom], axis=0))


def _halo_specs(width, tm, total, col=0, rows=HALO):
    per = tm // rows
    nblk = total // rows
    before = pl.BlockSpec((None, rows, width), lambda b, i: (b, jnp.maximum(i * per - 1, 0), col))
    after = pl.BlockSpec((None, rows, width), lambda b, i: (b, jnp.minimum((i + 1) * per, nblk - 1), col))
    return before, after


def _edge_rows(before_ref, after_ref):
    before = before_ref[...].astype(F32)
    return before[before.shape[0] - 1:], after_ref[...].astype(F32)[0:1]


def _ada_kernel(c_ref, w_ref, b_ref, o_ref):
    c = c_ref[...]
    o_ref[...] = _dot(c * _sigmoid(c), w_ref[...]) + b_ref[...]


def _ada(cvec, ada_w, ada_b):
    depth, d, n = ada_w.shape
    tn = 1536
    return pl.pallas_call(
        _ada_kernel,
        grid=(depth, n // tn),
        in_specs=[pl.BlockSpec((HALO, d), lambda l, j: (0, 0)),
                  pl.BlockSpec((None, d, tn), lambda l, j: (l, 0, j)),
                  pl.BlockSpec((None, 1, tn), lambda l, j: (l, 0, j))],
        out_specs=pl.BlockSpec((None, HALO, tn), lambda l, j: (l, 0, j)),
        out_shape=jax.ShapeDtypeStruct((depth, HALO, n), F32),
        compiler_params=_cparams(("arbitrary", "arbitrary")),
    )(cvec, ada_w, ada_b.reshape(depth, 1, n))


def _nmm_kernel(x_ref, mod_ref, g_ref, w_ref, o_ref, *, k_shift, k_scale, tm, ctx):
    i = pl.program_id(1)
    bsz = x_ref.shape[0]
    is_ctx = _row_ids(i, tm) < ctx
    hs = []
    for b in range(bsz):
        x = x_ref[b]
        y = x * lax.rsqrt(jnp.mean(x * x, axis=-1, keepdims=True) + NORM_EPS) * g_ref[...]
        mod_b = mod_ref.at[b]
        hs.append((y * (1.0 + _mod_row(mod_b, k_scale, is_ctx)) + _mod_row(mod_b, k_shift, is_ctx)).astype(BF16))
    out = _dot(jnp.concatenate(hs, axis=0), w_ref[...])
    o_ref[...] = out.reshape(o_ref.shape).astype(o_ref.dtype)


def _nmm(x, mod, g, w, *, k_shift, k_scale, ctx, tm, tn, out_dtype):
    bsz, total, d = x.shape
    n = w.shape[1]
    kern = functools.partial(_nmm_kernel, k_shift=k_shift, k_scale=k_scale, tm=tm, ctx=ctx)
    return pl.pallas_call(
        kern,
        grid=(n // tn, total // tm),
        in_specs=[pl.BlockSpec((bsz, tm, d), lambda j, i: (0, i, 0)),
                  pl.BlockSpec((bsz, 2, 6, d), lambda j, i: (0, 0, 0, 0)),
                  pl.BlockSpec((1, d), lambda j, i: (0, 0)),
                  pl.BlockSpec((d, tn), lambda j, i: (0, j))],
        out_specs=pl.BlockSpec((bsz, tm, tn), lambda j, i: (0, i, j)),
        out_shape=jax.ShapeDtypeStruct((bsz, total, n), out_dtype),
        compiler_params=_cparams(("arbitrary", "arbitrary")),
    )(x, mod, g, w)


def _head_sums(z, lane_lo):
    s_lo = jnp.sum(jnp.where(lane_lo, z, 0.0), axis=-1, keepdims=True)
    s_hi = jnp.sum(jnp.where(lane_lo, 0.0, z), axis=-1, keepdims=True)
    return jnp.where(lane_lo, s_lo, s_hi)


def _chunk_cumsum(tri, z):
    hi = z.astype(BF16)
    rest = z - hi.astype(F32)
    mid = rest.astype(BF16)
    low = (rest - mid.astype(F32)).astype(BF16)
    dot = functools.partial(jnp.dot, preferred_element_type=F32)
    return dot(tri, low) + dot(tri, mid) + dot(tri, hi)


def _prep_kernel(rkv_ref, before_ref, after_ref, lora_ref, tri_ref, mu_ref, w0_ref, w2_ref, a0_ref, a2_ref,
                 g2_ref, kk_ref, ka_ref, rk_ref,
                 at_out, rt_out, bt_out, kt_out, gend_out, v_out, bonus_out, g_out, *, tm, ctx, total):
    i = pl.program_id(1)
    x = rkv_ref[...].astype(F32)
    prev, nxt = _neighbours(x, *_edge_rows(before_ref, after_ref), i, tm, ctx, total)
    mu_prev = mu_ref[0:1, :]
    mu_next = mu_ref[1:2, :]
    s = prev * mu_prev + x * (1.0 - mu_prev - mu_next) + nxt * mu_next
    d = D_MODEL
    r, k, v = s[:, :d], s[:, d:2 * d], s[:, 2 * d:]

    lo = lora_ref[...].astype(F32)
    w_pre = w0_ref[...] + _dot(jnp.tanh(lo[:, :LANES]), w2_ref[...])
    lw = -DECAY_SCALE * _sigmoid(w_pre)
    a = _sigmoid(a0_ref[...] + _dot(lo[:, LANES:2 * LANES], a2_ref[...]))
    g = _dot(_sigmoid(lo[:, 2 * LANES:]), g2_ref[...])
    kk = k * kk_ref[...]
    ka = ka_ref[...]
    rk = rk_ref[...]
    cl = [_chunk_cumsum(tri_ref[dr], lw[:, d * dr:d * (dr + 1)]) for dr in range(2)]

    lane_lo = lax.broadcasted_iota(jnp.int32, (1, LANES), 1) < HEAD_DIM
    for p in range(N_PAIR):
        sl = slice(LANES * p, LANES * (p + 1))
        kkp = kk[:, sl]
        kn = kkp * lax.rsqrt(_head_sums(kkp * kkp, lane_lo) + KK_EPS)
        rp, kp, vp = r[:, sl], k[:, sl], v[:, sl]
        bonus = jnp.zeros_like(rp)
        for dr in range(2):
            dsl = slice(d * dr + LANES * p, d * dr + LANES * (p + 1))
            ad = a[:, dsl]
            kd = kp * (1.0 + (ad - 1.0) * ka[:, sl])
            bonus = bonus + _head_sums(rp * kd * rk[:, sl], lane_lo) * vp
            cl_incl = cl[dr][:, sl]
            inv_g = jnp.exp(-cl_incl)
            at_out[dr, p] = (-kn * jnp.exp(cl_incl - lw[:, dsl])).astype(BF16)
            rt_out[dr, p] = (rp * jnp.exp(cl_incl)).astype(BF16)
            bt_out[dr, p] = (kn * ad * inv_g).astype(BF16)
            kt_out[dr, p] = (kd * inv_g).astype(BF16)
            for q in range(tm // CHUNK):
                end = CHUNK * q + (CHUNK - 1 if dr == 0 else 0)
                gend_out[dr, q, p:p + 1, :] = jnp.exp(cl_incl[end:end + 1, :])
        v_out[p] = vp.astype(BF16)
        bonus_out[p] = bonus.astype(BF16)
        g_out[p] = g[:, sl].astype(BF16)


def _prep(proj, tri, mu, w0, w2bd, a0, a2bd, g2, k_k, k_a, r_k, *, ctx, tm):
    bsz, total, _ = proj.shape
    d = D_MODEL
    kern = functools.partial(_prep_kernel, tm=tm, ctx=ctx, total=total)
    before, after = _halo_specs(3 * d, tm, total, P_RKV // (3 * d), HALO_BF16)
    full = lambda shp: pl.BlockSpec(shp, lambda b, i: tuple(0 for _ in shp))
    pm = pl.BlockSpec((None, N_PAIR, tm, LANES), lambda b, i: (b, 0, i, 0))
    pm2 = pl.BlockSpec((2, None, N_PAIR, tm, LANES), lambda b, i: (0, b, 0, i, 0))
    ge = pl.BlockSpec((2, None, tm // CHUNK, N_PAIR, LANES), lambda b, i: (0, b, i, 0, 0))
    sds = jax.ShapeDtypeStruct((bsz, N_PAIR, total, LANES), BF16)
    sds2 = jax.ShapeDtypeStruct((2, bsz, N_PAIR, total, LANES), BF16)
    sds_ge = jax.ShapeDtypeStruct((2, bsz, total // CHUNK, N_PAIR, LANES), F32)
    return pl.pallas_call(
        kern,
        grid=(bsz, total // tm),
        in_specs=[pl.BlockSpec((None, tm, 3 * d), lambda b, i: (b, i, P_RKV // (3 * d))), before, after,
                  pl.BlockSpec((None, tm, LORA_W), lambda b, i: (b, i, P_LORA // LORA_W)),
                  full((2, tm, tm)), full((2, 3 * d)), full((1, 2 * d)), full((LANES, 2 * d)), full((1, 2 * d)),
                  full((LANES, 2 * d)), full((R_GATE, d)), full((1, d)), full((1, d)), full((1, d))],
        out_specs=[pm2, pm2, pm2, pm2, ge, pm, pm, pm],
        out_shape=[sds2, sds2, sds2, sds2, sds_ge, sds, sds, sds],
        compiler_params=_cparams(("arbitrary", "arbitrary")),
    )(proj, proj, proj, proj, tri, mu, w0, w2bd, a0, a2bd, g2, k_k, k_a, r_k)


M_STRICT, M_INCL, M_EYE, M_LEVEL0 = 0, 1, 2, 3
N_MASKS = M_LEVEL0 + 1 + N_LEVELS


def _scan_masks(tm):
    c = CHUNK
    t = np.arange(c)[:, None]
    s = np.arange(c)[None, :]
    cum = np.zeros((2, tm, tm), np.float32)
    msk = np.zeros((2, N_MASKS, c, 2 * c), np.float32)
    for dr in range(2):
        earlier = (s < t) if dr == 0 else (s > t)
        cum[dr] = np.kron(np.eye(tm // c), earlier | (s == t))
        planes = [earlier, earlier | (s == t), s == t]
        m = 1
        while m < c:
            same = (t // (2 * m)) == (s // (2 * m))
            t_late = (t % (2 * m)) >= m
            s_late = (s % (2 * m)) >= m
            planes.append(same & (t_late & ~s_late if dr == 0 else ~t_late & s_late))
            m *= 2
        for q, plane in enumerate(planes):
            msk[dr, q] = np.concatenate([plane, plane], axis=1)
    bdm = np.kron(np.eye(2, dtype=np.float32), np.ones((c, c), np.float32))
    return jnp.asarray(cum, BF16), jnp.asarray(msk), jnp.asarray(bdm)


def _scan_kernel(msk_ref, bdm_ref, at_ref, rt_ref, bt_ref, kt_ref, gend_ref, v_ref, y_ref, s_ref):
    c = CHUNK
    bsz = v_ref.shape[0]
    n = bsz * N_PAIR

    @pl.when(pl.program_id(1) == 0)
    def _():
        s_ref[...] = jnp.zeros_like(s_ref)

    bdm = bdm_ref[...]
    bdm_bf = bdm.astype(BF16)
    m_strict = msk_ref[M_STRICT]
    m_incl = msk_ref[M_INCL]

    def load(ref):
        return ref[...].reshape(n, c, LANES)

    def stack(z):
        z = z.astype(BF16)
        return jnp.concatenate([z, z], axis=1) * bdm_bf

    bt, kt, v = load(bt_ref), load(kt_ref), load(v_ref)
    ar = jnp.concatenate([load(at_ref), load(rt_ref)], axis=1)
    g_end = jnp.stack([gend_ref[q // N_PAIR, q % N_PAIR:q % N_PAIR + 1, :] for q in range(n)], axis=0)
    state = s_ref[...]

    gram = _bmm_nt(ar, jnp.concatenate([stack(bt), stack(kt)], axis=1))
    g_ab = gram[:, :c, :2 * c]
    from_v = jnp.concatenate([gram[:, :c, 2 * c:] * m_strict, gram[:, c:, 2 * c:] * m_incl], axis=1)
    partial = _bmm_nt(ar, state) + _bmm(from_v, stack(v))

    tri = msk_ref[M_EYE] + g_ab * msk_ref[M_LEVEL0]
    for lvl in range(1, N_LEVELS + 1):
        w = _bmm(g_ab * msk_ref[M_LEVEL0 + lvl], stack(tri))
        tri = tri + _bmm(tri, stack(w))

    u = _bmm(tri, stack(partial[:, :c]))
    y = partial[:, c:] + _bmm(gram[:, c:, :2 * c] * m_incl, stack(u))
    y_ref[...] = y.reshape(bsz, N_PAIR, c, LANES).astype(y_ref.dtype)
    upd = _bmm_tn(jnp.concatenate([u.astype(BF16), v], axis=1), jnp.concatenate([bt, kt], axis=1))
    s_ref[...] = (state + upd * bdm) * g_end


def _scan(at, rt, bt, kt, gend, v, masks, *, ctx):
    bsz, _, total, _ = v.shape
    c = CHUNK
    nc, ncc = total // c, ctx // c
    _, msk, bdm = masks

    def chunk(dr, j):
        rev = jnp.where(j < ncc, ncc - 1 - j, nc + ncc - 1 - j)
        return jnp.where(dr == 0, j, rev)

    shared = pl.BlockSpec((bsz, N_PAIR, c, LANES), lambda dr, j: (0, 0, chunk(dr, j), 0))
    per_dir = pl.BlockSpec((None, bsz, N_PAIR, c, LANES), lambda dr, j: (dr, 0, 0, chunk(dr, j), 0))
    ends = pl.BlockSpec((None, bsz, None, N_PAIR, LANES), lambda dr, j: (dr, 0, chunk(dr, j), 0, 0))
    return pl.pallas_call(
        _scan_kernel,
        grid=(2, nc),
        in_specs=[pl.BlockSpec((None, N_MASKS, c, 2 * c), lambda dr, j: (dr, 0, 0, 0)),
                  pl.BlockSpec((2 * c, 2 * c), lambda dr, j: (0, 0)),
                  per_dir, per_dir, per_dir, per_dir, ends, shared],
        out_specs=per_dir,
        out_shape=jax.ShapeDtypeStruct((2, bsz, N_PAIR, total, LANES), BF16),
        scratch_shapes=[pltpu.VMEM((bsz * N_PAIR, 2 * HEAD_DIM, 2 * HEAD_DIM), F32)],
        compiler_params=_cparams(("arbitrary", "arbitrary")),
    )(msk, bdm, at, rt, bt, kt, gend, v)


def _rope(z, cos, sin):
    lane = lax.broadcasted_iota(jnp.int32, (1, LANES), 1)
    first = jnp.bitwise_and(lane, HEAD_DIM // 2 - 1) < (HEAD_DIM // 4)
    out = []
    for p in range(z.shape[1] // LANES):
        zp = z[:, LANES * p:LANES * (p + 1)]
        partner = jnp.where(first, pltpu.roll(zp, LANES - HEAD_DIM // 4, 1), pltpu.roll(zp, HEAD_DIM // 4, 1))
        out.append(zp * cos + partner * sin)
    return jnp.concatenate(out, axis=1)


def _attn_kernel(sink_ref, q_ref, kp_ref, kc_ref, kn_ref, kx_ref, vp_ref, vc_ref, vn_ref, vx_ref,
                 cq_ref, sq_ref, cp_ref, sp_ref, cn_ref, sn_ref, o_ref, *, n_ctx_blk, n_blk):
    i = pl.program_id(1)
    blk = ATT_BLOCK
    f32 = lambda ref: ref[...].astype(F32)
    q = _rope(f32(q_ref), cq_ref[...], sq_ref[...]) * ATT_SCALE
    k_all = jnp.concatenate([_rope(f32(kp_ref), cp_ref[...], sp_ref[...]),
                             _rope(f32(kc_ref), cq_ref[...], sq_ref[...]),
                             _rope(f32(kn_ref), cn_ref[...], sn_ref[...]),
                             f32(kx_ref)], axis=0)
    v_all = jnp.concatenate([vp_ref[...], vc_ref[...], vn_ref[...], vx_ref[...]], axis=0)
    n_keys = k_all.shape[0]

    qi = lax.broadcasted_iota(jnp.int32, (blk, n_keys), 0)
    si = lax.broadcasted_iota(jnp.int32, (blk, n_keys), 1)
    rel = si - blk - qi
    key_blk = i - 1 + jnp.right_shift(si, 7)
    ok = (jnp.where(jnp.abs(rel) <= WINDOW, 1, 0) * jnp.where(key_blk >= n_ctx_blk, 1, 0)
          * jnp.where(key_blk < n_blk, 1, 0) * jnp.where(i >= n_ctx_blk, 1, 0))
    ok = jnp.where(si >= 3 * blk, 1, ok)
    bias = jnp.where(ok > 0, 0.0, NEG_INF)

    lane_lo = lax.broadcasted_iota(jnp.int32, (1, LANES), 1) < HEAD_DIM
    group = N_HEADS // N_KV
    outs = [None] * N_HEADS
    for kvh in range(N_KV):
        kv_slab = slice(LANES * (kvh // 2), LANES * (kvh // 2 + 1))
        kv_lo = kvh % 2 == 0
        rows, sinks = [], []
        for hq in range(group):
            h = kvh * group + hq
            qp = q[:, LANES * (h // 2):LANES * (h // 2 + 1)]
            if (h % 2 == 0) != kv_lo:
                qp = pltpu.roll(qp, HEAD_DIM, 1)
            rows.append(jnp.where(lane_lo, qp, 0.0) if kv_lo else jnp.where(lane_lo, 0.0, qp))
            sinks.append(jnp.full((blk, 1), sink_ref[h], F32))
        qg = jnp.concatenate(rows, axis=0)
        sink = jnp.concatenate(sinks, axis=0)
        s = _dot_nt(qg, k_all[:, kv_slab]) + jnp.concatenate([bias] * group, axis=0)
        m = jnp.maximum(jnp.max(s, axis=-1, keepdims=True), sink)
        e = jnp.exp(s - m)
        denom = jnp.sum(e, axis=-1, keepdims=True) + jnp.exp(sink - m)
        o = _dot(e, v_all[:, kv_slab]) / denom
        for hq in range(group):
            h = kvh * group + hq
            oh = o[hq * blk:(hq + 1) * blk]
            if (h % 2 == 0) != kv_lo:
                oh = pltpu.roll(oh, HEAD_DIM, 1)
            outs[h] = oh
    o_ref[...] = jnp.concatenate(
        [jnp.where(lane_lo, outs[2 * p], outs[2 * p + 1]) for p in range(N_PAIR)], axis=1).astype(o_ref.dtype)


def _attention(qkv, sink, cos, sin, *, ctx):
    bsz, total, _ = qkv.shape
    blk = ATT_BLOCK
    n_blk, n_ctx_blk = total // blk, ctx // blk
    kvw = KV_W
    kcol, vcol = P_K // kvw, P_V // kvw
    kern = functools.partial(_attn_kernel, n_ctx_blk=n_ctx_blk, n_blk=n_blk)
    prev_i = lambda i: jnp.maximum(i - 1, 0)
    next_i = lambda i: jnp.minimum(i + 1, n_blk - 1)

    def kv(col, row):
        return pl.BlockSpec((None, blk, kvw), lambda b, i: (b, row(i), col))

    def kv_ctx(col):
        return pl.BlockSpec((None, ctx, kvw), lambda b, i: (b, 0, col))

    def tab(row):
        return pl.BlockSpec((blk, LANES), lambda b, i: (row(i), 0))

    same = lambda i: i
    return pl.pallas_call(
        kern,
        grid=(bsz, n_blk),
        in_specs=[pl.BlockSpec(memory_space=pltpu.SMEM),
                  pl.BlockSpec((None, blk, D_MODEL), lambda b, i: (b, i, P_Q // D_MODEL)),
                  kv(kcol, prev_i), kv(kcol, same), kv(kcol, next_i), kv_ctx(kcol),
                  kv(vcol, prev_i), kv(vcol, same), kv(vcol, next_i), kv_ctx(vcol),
                  tab(same), tab(same), tab(prev_i), tab(prev_i), tab(next_i), tab(next_i)],
        out_specs=pl.BlockSpec((None, blk, D_MODEL), lambda b, i: (b, i, 0)),
        out_shape=jax.ShapeDtypeStruct((bsz, total, D_MODEL), BF16),
        compiler_params=_cparams(("arbitrary", "arbitrary")),
    )(sink, qkv, qkv, qkv, qkv, qkv, qkv, qkv, qkv, qkv, cos, sin, cos, sin, cos, sin)


def _merge_kernel(x_ref, mod_ref, cv_ref, before_ref, after_ref, gate_ref, att_ref, y_ref, bonus_ref, g_ref,
                  cw_ref, lng_ref, lnb_ref, wa_ref, wb_ref, wc_ref, wo_ref, o_ref, *, tm, ctx, total):
    i = pl.program_id(0)
    bsz = x_ref.shape[0]
    d = D_MODEL
    cw = cw_ref[...]
    lng = lng_ref[...]
    lnb = lnb_ref[...]
    lane_lo = lax.broadcasted_iota(jnp.int32, (1, LANES), 1) < HEAD_DIM
    convs, rwkvs = [], []
    for b in range(bsz):
        cv = cv_ref[b].astype(F32)
        z = cv[:, d:2 * d] * cv[:, 2 * d:]
        before, after = _edge_rows(before_ref.at[b], after_ref.at[b])
        prev, nxt = _neighbours(z, before[:, d:2 * d] * before[:, 2 * d:], after[:, d:2 * d] * after[:, 2 * d:],
                                i, tm, ctx, total)
        convs.append((cv[:, :d] * (prev * cw[0:1] + z * cw[1:2] + nxt * cw[2:3])).astype(BF16))
        slabs = []
        for p in range(N_PAIR):
            sl = slice(LANES * p, LANES * (p + 1))
            y = y_ref[0, b, p].astype(F32) + y_ref[1, b, p].astype(F32)
            mean = _head_sums(y, lane_lo) * (1.0 / HEAD_DIM)
            yc = y - mean
            var = _head_sums(yc * yc, lane_lo) * (1.0 / HEAD_DIM)
            gn = yc * lax.rsqrt(var + GN_EPS) * lng[:, sl] + lnb[:, sl]
            slabs.append(((gn + bonus_ref[b, p]) * g_ref[b, p]).astype(BF16))
        rwkvs.append(jnp.concatenate(slabs, axis=1))

    rows = bsz * tm
    gates = gate_ref[...].reshape(rows, 3 * d).astype(F32)
    m = (_sigmoid(gates[:, :d]) * _dot(jnp.concatenate(convs, axis=0), wa_ref[...])
         + _sigmoid(gates[:, d:2 * d]) * _dot(jnp.concatenate(rwkvs, axis=0), wb_ref[...])
         + _sigmoid(gates[:, 2 * d:]) * _dot(att_ref[...].reshape(rows, d), wc_ref[...]))
    out = _dot(m, wo_ref[...])
    is_ctx = _row_ids(i, tm) < ctx
    for b in range(bsz):
        o_ref[b] = x_ref[b] + _mod_row(mod_ref.at[b], 2, is_ctx) * out[b * tm:(b + 1) * tm]


def _merge(x, mod, proj, att, y, bonus, g, conv_w, ln_g, ln_b, wa, wb, wc, wo, *, ctx, tm):
    bsz, total, d = x.shape
    kern = functools.partial(_merge_kernel, tm=tm, ctx=ctx, total=total)
    per, nblk = tm // HALO_BF16, total // HALO_BF16
    col_cv, col_gate = P_CONV // (3 * d), P_GATE // (3 * d)
    before = pl.BlockSpec((bsz, HALO_BF16, 3 * d), lambda i: (0, jnp.maximum(i * per - 1, 0), col_cv))
    after = pl.BlockSpec((bsz, HALO_BF16, 3 * d), lambda i: (0, jnp.minimum((i + 1) * per, nblk - 1), col_cv))
    row = lambda w, col=0: pl.BlockSpec((bsz, tm, w), lambda i: (0, i, col))
    once = lambda shp: pl.BlockSpec(shp, lambda i: tuple(0 for _ in shp), pipeline_mode=pl.Buffered(1))
    pm = pl.BlockSpec((bsz, N_PAIR, tm, LANES), lambda i: (0, 0, i, 0))
    pm2 = pl.BlockSpec((2, bsz, N_PAIR, tm, LANES), lambda i: (0, 0, 0, i, 0))
    return pl.pallas_call(
        kern,
        grid=(total // tm,),
        in_specs=[row(d), pl.BlockSpec((bsz, 2, 6, d), lambda i: (0, 0, 0, 0)),
                  row(3 * d, col_cv), before, after, row(3 * d, col_gate), row(d), pm2, pm, pm,
                  once((3, d)), once((1, d)), once((1, d)),
                  once((d, d)), once((d, d)), once((d, d)), once((d, d))],
        out_specs=row(d),
        out_shape=jax.ShapeDtypeStruct((bsz, total, d), F32),
        compiler_params=_cparams(("arbitrary",)),
    )(x, mod, proj, proj, proj, proj, att, y, bonus, g, conv_w, ln_g, ln_b, wa, wb, wc, wo)


FF_CHUNKS = 2
FF_CHUNK = D_FF // FF_CHUNKS
assert FF_CHUNK * FF_CHUNKS == D_FF and FF_CHUNK % LANES == 0


def _ffn_kernel(x_ref, before_ref, after_ref, mod_ref, g_ref, wu_ref, cw_ref, wd_ref, o_ref, *, tm, ctx, total):
    i = pl.program_id(0)
    bsz = x_ref.shape[0]
    start = i * tm
    ext = tm + 2 * HALO
    is_ctx = (start - HALO + lax.broadcasted_iota(jnp.int32, (ext, 1), 0)) < ctx
    hs = []
    for b in range(bsz):
        xe = jnp.concatenate([before_ref[b], x_ref[b], after_ref[b]], axis=0)
        y = xe * lax.rsqrt(jnp.mean(xe * xe, axis=-1, keepdims=True) + NORM_EPS) * g_ref[...]
        mod_b = mod_ref.at[b]
        hs.append((y * (1.0 + _mod_row(mod_b, 4, is_ctx)) + _mod_row(mod_b, 3, is_ctx)).astype(BF16))
    h = jnp.concatenate(hs, axis=0)

    keep_before = jnp.where((start == 0) | (start == ctx), 0.0, 1.0)
    keep_after = jnp.where((start + tm == ctx) | (start + tm == total), 0.0, 1.0)
    row = lax.broadcasted_iota(jnp.int32, (HALO, 1), 0)
    first_row = row == 0
    last_row = row == HALO - 1

    def conv(u, cw):
        prev_all = pltpu.roll(u, 1, 0)
        nxt_all = pltpu.roll(u, bsz * ext - 1, 0)
        out = []
        for b in range(bsz):
            rows = slice(b * ext + HALO, b * ext + HALO + tm)
            prev, nxt = prev_all[rows], nxt_all[rows]
            top = jnp.where(first_row, prev[:HALO] * keep_before, prev[:HALO])
            bottom = jnp.where(last_row, nxt[tm - HALO:] * keep_after, nxt[tm - HALO:])
            prev = jnp.concatenate([top, prev[HALO:]], axis=0)
            nxt = jnp.concatenate([nxt[:tm - HALO], bottom], axis=0)
            out.append(prev * cw[0:1] + u[rows] * cw[1:2] + nxt * cw[2:3])
        return jnp.concatenate(out, axis=0)

    def cols(j):
        return slice(FF_CHUNK * j, FF_CHUNK * (j + 1)), slice(D_FF + FF_CHUNK * j, D_FF + FF_CHUNK * (j + 1))

    def up(j):
        gate_cols, val_cols = cols(j)
        return _dot(h, wu_ref[:, gate_cols]), _dot(h, wu_ref[:, val_cols])

    acts = []
    u_next = up(0)
    for j in range(FF_CHUNKS):
        gate_cols, val_cols = cols(j)
        u_gate, u_val = u_next
        if j + 1 < FF_CHUNKS:
            u_next = up(j + 1)
        ug = conv(u_gate, cw_ref[:, gate_cols])
        uv = conv(u_val, cw_ref[:, val_cols])
        acts.append((ug * _sigmoid(ug) * uv).astype(BF16))
    acc = _dot(jnp.concatenate(acts, axis=1), wd_ref[...])
    tile_is_ctx = _row_ids(i, tm) < ctx
    for b in range(bsz):
        o_ref[b] = x_ref[b] + _mod_row(mod_ref.at[b], 5, tile_is_ctx) * acc[b * tm:(b + 1) * tm]


def _ffn(x, mod, g, wu, conv_w, wd, *, ctx, tm):
    bsz, total, d = x.shape
    kern = functools.partial(_ffn_kernel, tm=tm, ctx=ctx, total=total)
    per, nblk = tm // HALO, total // HALO
    before = pl.BlockSpec((bsz, HALO, d), lambda i: (0, jnp.maximum(i * per - 1, 0), 0))
    after = pl.BlockSpec((bsz, HALO, d), lambda i: (0, jnp.minimum((i + 1) * per, nblk - 1), 0))
    row = pl.BlockSpec((bsz, tm, d), lambda i: (0, i, 0))
    once = lambda shp: pl.BlockSpec(shp, lambda i: tuple(0 for _ in shp), pipeline_mode=pl.Buffered(1))
    return pl.pallas_call(
        kern,
        grid=(total // tm,),
        in_specs=[row, before, after, pl.BlockSpec((bsz, 2, 6, d), lambda i: (0, 0, 0, 0)),
                  once((1, d)), once((d, 2 * D_FF)), once((3, 2 * D_FF)), once((D_FF, d))],
        out_specs=row,
        out_shape=jax.ShapeDtypeStruct((bsz, total, d), F32),
        compiler_params=_cparams(("arbitrary",)),
    )(x, x, x, mod, g, wu, conv_w, wd)


def _final_norm_kernel(x_ref, g_ref, o_ref):
    x = x_ref[...]
    o_ref[...] = x * lax.rsqrt(jnp.mean(x * x, axis=-1, keepdims=True) + NORM_EPS) * g_ref[...]


def _final_norm(x, g, *, ctx, seq, tm):
    bsz, _, d = x.shape
    off = ctx // tm
    return pl.pallas_call(
        _final_norm_kernel,
        grid=(bsz, seq // tm),
        in_specs=[pl.BlockSpec((None, tm, d), lambda b, i: (b, i + off, 0)),
                  pl.BlockSpec((1, d), lambda b, i: (0, 0))],
        out_specs=pl.BlockSpec((None, tm, d), lambda b, i: (b, i, 0)),
        out_shape=jax.ShapeDtypeStruct((bsz, seq, d), F32),
        compiler_params=_cparams(("arbitrary", "arbitrary")),
    )(x, g)


def _rope_tables(ctx, seq):
    rows = seq // GRID_W
    t_row = jnp.broadcast_to(jnp.arange(rows)[:, None], (rows, GRID_W)).reshape(-1).astype(F32)
    t_col = jnp.broadcast_to(jnp.arange(GRID_W)[None, :], (rows, GRID_W)).reshape(-1).astype(F32)
    n_freq = HEAD_DIM // 4
    inv = ROPE_THETA ** (-jnp.arange(n_freq, dtype=F32) / n_freq)
    ar, ac = t_row[:, None] * inv, t_col[:, None] * inv
    cos = jnp.concatenate([jnp.cos(ar), jnp.cos(ar), jnp.cos(ac), jnp.cos(ac)], axis=1)
    sin = jnp.concatenate([-jnp.sin(ar), jnp.sin(ar), -jnp.sin(ac), jnp.sin(ac)], axis=1)
    cos = jnp.concatenate([jnp.ones((ctx, HEAD_DIM), F32), cos], axis=0)
    sin = jnp.concatenate([jnp.zeros((ctx, HEAD_DIM), F32), sin], axis=0)
    return jnp.tile(cos, (1, LANES // HEAD_DIM)), jnp.tile(sin, (1, LANES // HEAD_DIM))


def _block_diag2(w):
    z = jnp.zeros_like(w[0])
    return jnp.concatenate([jnp.concatenate([w[0], z], axis=1), jnp.concatenate([z, w[1]], axis=1)], axis=0)


def kernel(x, c, ctx, c_ctx, ada_w, ada_b, norm1_g, w_in, conv_a_w, a_out_w, rwkv_mu, rwkv_w0, rwkv_w2,
           rwkv_a0, rwkv_a2, rwkv_g2, rwkv_k_k, rwkv_k_a, rwkv_r_k, rwkv_ln_g, rwkv_ln_b, rwkv_out_w,
           attn_sink, attn_out_w, w_o, norm2_g, ffn_up, ffn_conv, ffn_down, final_norm_g):
    bsz, seq, d = x.shape
    n_ctx = ctx.shape[1]
    depth = ada_w.shape[0]
    total = n_ctx + seq
    assert d == D_MODEL and seq % ATT_BLOCK == 0 and n_ctx % ATT_BLOCK == 0 and bsz + 1 <= HALO
    tm = 256
    assert total % tm == 0 and n_ctx % tm == 0

    cvec = jnp.zeros((HALO, d), F32).at[:bsz].set(c).at[bsz].set(c_ctx)
    mods = _ada(cvec, ada_w.astype(BF16), ada_b)
    mods = mods.reshape(depth, HALO, 6, d)
    mods = jnp.stack([jnp.broadcast_to(mods[:, bsz:bsz + 1], (depth, bsz, 6, d)), mods[:, :bsz]], axis=2)

    cos, sin = _rope_tables(n_ctx, seq)
    masks = _scan_masks(tm)
    xa = jnp.concatenate([ctx, x], axis=1)

    o_lora = 6 * d
    o_qkv = o_lora + LORA_W
    o_gate = o_qkv + d + 2 * KV_W
    assert w_in.shape[2] == N_PROJ == o_gate + 3 * d
    for l in range(depth):
        mod = mods[l]
        w = w_in[l].astype(BF16)
        w = jnp.concatenate([w[:, :o_lora], w[:, o_gate:], w[:, o_qkv:o_gate], w[:, o_lora:o_qkv]], axis=1)
        proj = _nmm(xa, mod, norm1_g[l][None], w, k_shift=0, k_scale=1, ctx=n_ctx, tm=tm, tn=N_PROJ // 3,
                    out_dtype=BF16)

        at, rt, bt, kt, gend, v, bonus, g = _prep(
            proj, masks[0], rwkv_mu[l], rwkv_w0[l].reshape(1, 2 * d), _block_diag2(rwkv_w2[l]).astype(BF16),
            rwkv_a0[l].reshape(1, 2 * d), _block_diag2(rwkv_a2[l]).astype(BF16), rwkv_g2[l].astype(BF16),
            rwkv_k_k[l][None], rwkv_k_a[l][None], rwkv_r_k[l].reshape(1, d), ctx=n_ctx, tm=tm)
        y = _scan(at, rt, bt, kt, gend, v, masks, ctx=n_ctx)
        att = _attention(proj, attn_sink[l], cos, sin, ctx=n_ctx)
        xa = _merge(xa, mod, proj, att, y, bonus, g, conv_a_w[l], rwkv_ln_g[l][None], rwkv_ln_b[l][None],
                    a_out_w[l].astype(BF16), rwkv_out_w[l].astype(BF16), attn_out_w[l].astype(BF16),
                    w_o[l].astype(BF16), ctx=n_ctx, tm=tm)
        xa = _ffn(xa, mod, norm2_g[l][None], ffn_up[l].astype(BF16), ffn_conv[l], ffn_down[l].astype(BF16),
                  ctx=n_ctx, tm=tm)
    return _final_norm(xa, final_norm_g[None], ctx=n_ctx, seq=seq, tm=tm)
```

```python
import functools

import jax
import jax.numpy as jnp
import numpy as np
from jax import lax
from jax.experimental import pallas as pl
from jax.experimental.pallas import tpu as pltpu

D_MODEL = 1024
GRID_W = 64
N_HEADS = 16
HEAD_DIM = 64
N_KV = 4
R_LORA = 64
R_GATE = 128
GN_EPS = 64e-5
WINDOW = 128
ATT_BLOCK = 128
LOG2_ATT_BLOCK = 7
ATT_SCALE = HEAD_DIM ** -0.5
ROPE_THETA = 10000.0
NEG_INF = -1e30
D_FF = 2816
NORM_EPS = 1e-6
KK_EPS = 1e-12
DECAY_SCALE = float(np.exp(-0.5))

LANES = 128
HALO = 8
HALO_BF16 = 16
N_PAIR = D_MODEL // LANES
CHUNK = 64
N_LEVELS = 5
VMEM_LIMIT = 52 * 1024 * 1024

BF16 = jnp.bfloat16
F32 = jnp.float32

KV_W = N_KV * HEAD_DIM
LORA_W = 2 * R_LORA + 2 * R_LORA + R_GATE
N_PROJ_A = 6 * D_MODEL + LORA_W
A_CONV, A_RKV, A_LORA = 0, 3 * D_MODEL, 6 * D_MODEL
N_PROJ_B = D_MODEL + 2 * KV_W + 3 * D_MODEL
B_Q, B_K, B_V, B_GATE = 0, D_MODEL, D_MODEL + KV_W, D_MODEL + 2 * KV_W
GATE_BLK = B_GATE
assert A_LORA % LORA_W == 0 and B_K % KV_W == 0 and 2 * GATE_BLK == 3 * D_MODEL and GATE_BLK % LANES == 0


def _cparams(sem):
    return pltpu.CompilerParams(dimension_semantics=sem, vmem_limit_bytes=VMEM_LIMIT)


def _dot(a, b):
    return jnp.dot(a.astype(BF16), b.astype(BF16), preferred_element_type=F32)


def _dot_nt(a, b):
    return lax.dot_general(a.astype(BF16), b.astype(BF16), (((1,), (1,)), ((), ())),
                           preferred_element_type=F32)


def _bmm(a, b):
    return jnp.einsum('pmk,pkn->pmn', a.astype(BF16), b.astype(BF16), preferred_element_type=F32)


def _bmm_nt(a, b):
    return jnp.einsum('pmk,pnk->pmn', a.astype(BF16), b.astype(BF16), preferred_element_type=F32)


def _bmm_tn(a, b):
    return jnp.einsum('pkm,pkn->pmn', a.astype(BF16), b.astype(BF16), preferred_element_type=F32)


def _sigmoid(x):
    return 0.5 + 0.5 * jnp.tanh(0.5 * x)


def _row_ids(i, tm):
    return i * tm + lax.broadcasted_iota(jnp.int32, (tm, 1), 0)


def _mod_row(mod_ref, k, is_ctx):
    return jnp.where(is_ctx, mod_ref[0, k:k + 1, :], mod_ref[1, k:k + 1, :])


def _neighbours(z, before, after, i, tm, ctx, total):
    start = i * tm
    keep_before = jnp.where((start == 0) | (start == ctx), 0.0, 1.0)
    keep_after = jnp.where((start + tm == ctx) | (start + tm == total), 0.0, 1.0)
    row = lax.broadcasted_iota(jnp.int32, (HALO, 1), 0)
    prev = pltpu.roll(z, 1, 0)
    nxt = pltpu.roll(z, tm - 1, 0)
    top = jnp.where(row == 0, before * keep_before, prev[:HALO])
    bottom = jnp.where(row == HALO - 1, after * keep_after, nxt[tm - HALO:])
    return (jnp.concatenate([top, prev[HALO:]], axis=0),
            jnp.concatenate([nxt[:tm - HALO], bottom], axis=0))


def _halo_specs(width, tm, total, col=0, rows=HALO):
    per = tm // rows
    nblk = total // rows
    before = pl.BlockSpec((None, rows, width), lambda b, i: (b, jnp.maximum(i * per - 1, 0), col))
    after = pl.BlockSpec((None, rows, width), lambda b, i: (b, jnp.minimum((i + 1) * per, nblk - 1), col))
    return before, after


def _edge_rows(before_ref, after_ref):
    before = before_ref[...].astype(F32)
    return before[before.shape[0] - 1:], after_ref[...].astype(F32)[0:1]


def _ada_kernel(c_ref, w_ref, b_ref, o_ref):
    c = c_ref[...]
    o_ref[...] = _dot(c * _sigmoid(c), w_ref[...]) + b_ref[...]


def _ada(cvec, ada_w, ada_b):
    depth, d, n = ada_w.shape
    tn = 1536
    return pl.pallas_call(
        _ada_kernel,
        grid=(depth, n // tn),
        in_specs=[pl.BlockSpec((HALO, d), lambda l, j: (0, 0)),
                  pl.BlockSpec((None, d, tn), lambda l, j: (l, 0, j)),
                  pl.BlockSpec((None, 1, tn), lambda l, j: (l, 0, j))],
        out_specs=pl.BlockSpec((None, HALO, tn), lambda l, j: (l, 0, j)),
        out_shape=jax.ShapeDtypeStruct((depth, HALO, n), F32),
        compiler_params=_cparams(("arbitrary", "arbitrary")),
    )(cvec, ada_w, ada_b.reshape(depth, 1, n))


PROJ_SPLITS = ((0, 2176), (2176, 4352), (4352, N_PROJ_A))
PROJ_SPLITS_B = ((0, 2304), (2304, N_PROJ_B))


def _proj_kernel(x_ref, mod_ref, g_ref, w_ref, oa_ref, ob_ref, *, tm, ctx):
    i = pl.program_id(0)
    bsz = x_ref.shape[0]
    is_ctx = _row_ids(i, tm) < ctx
    hs = []
    for b in range(bsz):
        x = x_ref[b]
        y = x * lax.rsqrt(jnp.mean(x * x, axis=-1, keepdims=True) + NORM_EPS) * g_ref[...]
        mod_b = mod_ref.at[b]
        hs.append((y * (1.0 + _mod_row(mod_b, 1, is_ctx)) + _mod_row(mod_b, 0, is_ctx)).astype(BF16))
    h = jnp.concatenate(hs, axis=0)
    for o_ref, base, splits in ((oa_ref, 0, PROJ_SPLITS), (ob_ref, N_PROJ_A, PROJ_SPLITS_B)):
        for lo, hi in splits:
            out = _dot(h, w_ref[:, base + lo:base + hi])
            o_ref[:, :, lo:hi] = out.reshape(bsz, tm, hi - lo).astype(o_ref.dtype)


def _proj(x, mod, g, w, *, ctx, tm):
    bsz, total, d = x.shape
    assert w.shape[1] == N_PROJ_A + N_PROJ_B
    kern = functools.partial(_proj_kernel, tm=tm, ctx=ctx)
    once = lambda shp: pl.BlockSpec(shp, lambda i: tuple(0 for _ in shp), pipeline_mode=pl.Buffered(1))
    out = lambda n: pl.BlockSpec((bsz, tm, n), lambda i: (0, i, 0))
    return pl.pallas_call(
        kern,
        grid=(total // tm,),
        in_specs=[pl.BlockSpec((bsz, tm, d), lambda i: (0, i, 0)),
                  pl.BlockSpec((bsz, 2, 6, d), lambda i: (0, 0, 0, 0)),
                  once((1, d)), once((d, N_PROJ_A + N_PROJ_B))],
        out_specs=[out(N_PROJ_A), out(N_PROJ_B)],
        out_shape=[jax.ShapeDtypeStruct((bsz, total, N_PROJ_A), BF16),
                   jax.ShapeDtypeStruct((bsz, total, N_PROJ_B), BF16)],
        compiler_params=_cparams(("arbitrary",)),
    )(x, mod, g, w)


def _head_sums(z, head_ones):
    return jnp.dot(z.astype(BF16), head_ones, preferred_element_type=F32)


def _head_sums_exact(z, lane_lo):
    s_lo = jnp.sum(jnp.where(lane_lo, z, 0.0), axis=-1, keepdims=True)
    s_hi = jnp.sum(jnp.where(lane_lo, 0.0, z), axis=-1, keepdims=True)
    return jnp.where(lane_lo, s_lo, s_hi)


def _chunk_cumsum(tri, z):
    hi = z.astype(BF16)
    low = (z - hi.astype(F32)).astype(BF16)
    dot = functools.partial(jnp.dot, preferred_element_type=F32)
    return dot(tri, low) + dot(tri, hi)


def _prep_kernel(rkv_ref, before_ref, after_ref, lora_ref, tri_ref, ones_ref, mu_ref, w0_ref, w2_ref, a0_ref,
                 a2_ref, g2_ref, kk_ref, ka_ref, rk_ref,
                 at_out, rt_out, bt_out, kt_out, gend_out, v_out, bonus_out, g_out, *, tm, ctx, total):
    i = pl.program_id(1)
    x = rkv_ref[...].astype(F32)
    prev, nxt = _neighbours(x, *_edge_rows(before_ref, after_ref), i, tm, ctx, total)
    mu_prev = mu_ref[0:1, :]
    mu_next = mu_ref[1:2, :]
    s = prev * mu_prev + x * (1.0 - mu_prev - mu_next) + nxt * mu_next
    d = D_MODEL
    r, k, v = s[:, :d], s[:, d:2 * d], s[:, 2 * d:]

    lo = lora_ref[...].astype(F32)
    w_pre = w0_ref[...] + _dot(jnp.tanh(lo[:, :LANES]), w2_ref[...])
    lw = -DECAY_SCALE * _sigmoid(w_pre)
    a = _sigmoid(a0_ref[...] + _dot(lo[:, LANES:2 * LANES], a2_ref[...]))
    g = _dot(_sigmoid(lo[:, 2 * LANES:]), g2_ref[...])
    kk = k * kk_ref[...]
    ka = ka_ref[...]
    rk = rk_ref[...]
    cl = [_chunk_cumsum(tri_ref[dr], lw[:, d * dr:d * (dr + 1)]) for dr in range(2)]

    head_ones = ones_ref[...]
    for p in range(N_PAIR):
        sl = slice(LANES * p, LANES * (p + 1))
        kkp = kk[:, sl]
        kn = kkp * lax.rsqrt(_head_sums(kkp * kkp, head_ones) + KK_EPS)
        rp, kp, vp = r[:, sl], k[:, sl], v[:, sl]
        bonus = jnp.zeros_like(rp)
        for dr in range(2):
            dsl = slice(d * dr + LANES * p, d * dr + LANES * (p + 1))
            ad = a[:, dsl]
            kd = kp * (1.0 + (ad - 1.0) * ka[:, sl])
            bonus = bonus + _head_sums(rp * kd * rk[:, sl], head_ones) * vp
            cl_incl = cl[dr][:, sl]
            inv_g = jnp.exp(-cl_incl)
            at_out[dr, p] = (-kn * jnp.exp(cl_incl - lw[:, dsl])).astype(BF16)
            rt_out[dr, p] = (rp * jnp.exp(cl_incl)).astype(BF16)
            bt_out[dr, p] = (kn * ad * inv_g).astype(BF16)
            kt_out[dr, p] = (kd * inv_g).astype(BF16)
            for q in range(tm // CHUNK):
                end = CHUNK * q + (CHUNK - 1 if dr == 0 else 0)
                gend_out[dr, q, p:p + 1, :] = jnp.exp(cl_incl[end:end + 1, :])
        v_out[p] = vp.astype(BF16)
        bonus_out[p] = bonus.astype(BF16)
        g_out[p] = g[:, sl].astype(BF16)


def _prep(proj, tri, head_ones, mu, w0, w2bd, a0, a2bd, g2, k_k, k_a, r_k, *, ctx, tm):
    bsz, total, _ = proj.shape
    d = D_MODEL
    kern = functools.partial(_prep_kernel, tm=tm, ctx=ctx, total=total)
    before, after = _halo_specs(3 * d, tm, total, A_RKV // (3 * d), HALO_BF16)
    full = lambda shp: pl.BlockSpec(shp, lambda b, i: tuple(0 for _ in shp))
    pm = pl.BlockSpec((None, N_PAIR, tm, LANES), lambda b, i: (b, 0, i, 0))
    pm2 = pl.BlockSpec((2, None, N_PAIR, tm, LANES), lambda b, i: (0, b, 0, i, 0))
    ge = pl.BlockSpec((2, None, tm // CHUNK, N_PAIR, LANES), lambda b, i: (0, b, i, 0, 0))
    sds = jax.ShapeDtypeStruct((bsz, N_PAIR, total, LANES), BF16)
    sds2 = jax.ShapeDtypeStruct((2, bsz, N_PAIR, total, LANES), BF16)
    sds_ge = jax.ShapeDtypeStruct((2, bsz, total // CHUNK, N_PAIR, LANES), F32)
    return pl.pallas_call(
        kern,
        grid=(bsz, total // tm),
        in_specs=[pl.BlockSpec((None, tm, 3 * d), lambda b, i: (b, i, A_RKV // (3 * d))), before, after,
                  pl.BlockSpec((None, tm, LORA_W), lambda b, i: (b, i, A_LORA // LORA_W)),
                  full((2, tm, tm)), full((LANES, LANES)), full((2, 3 * d)), full((1, 2 * d)), full((LANES, 2 * d)),
                  full((1, 2 * d)),
                  full((LANES, 2 * d)), full((R_GATE, d)), full((1, d)), full((1, d)), full((1, d))],
        out_specs=[pm2, pm2, pm2, pm2, ge, pm, pm, pm],
        out_shape=[sds2, sds2, sds2, sds2, sds_ge, sds, sds, sds],
        compiler_params=_cparams(("arbitrary", "arbitrary")),
    )(proj, proj, proj, proj, tri, head_ones, mu, w0, w2bd, a0, a2bd, g2, k_k, k_a, r_k)


M_STRICT, M_INCL, M_EYE, M_LEVEL0 = 0, 1, 2, 3
N_MASKS = M_LEVEL0 + 1 + N_LEVELS


def _scan_masks(tm):
    c = CHUNK
    t = np.arange(c)[:, None]
    s = np.arange(c)[None, :]
    cum = np.zeros((2, tm, tm), np.float32)
    msk = np.zeros((2, N_MASKS, c, 2 * c), np.float32)
    for dr in range(2):
        earlier = (s < t) if dr == 0 else (s > t)
        cum[dr] = np.kron(np.eye(tm // c), earlier | (s == t))
        planes = [earlier, earlier | (s == t), s == t]
        m = 1
        while m < c:
            same = (t // (2 * m)) == (s // (2 * m))
            t_late = (t % (2 * m)) >= m
            s_late = (s % (2 * m)) >= m
            planes.append(same & (t_late & ~s_late if dr == 0 else ~t_late & s_late))
            m *= 2
        for q, plane in enumerate(planes):
            msk[dr, q] = np.concatenate([plane, plane], axis=1)
    bdm = np.kron(np.eye(2, dtype=np.float32), np.ones((c, c), np.float32))
    return jnp.asarray(cum, BF16), jnp.asarray(msk), jnp.asarray(bdm)


def _scan_kernel(msk_ref, bdm_ref, at_ref, rt_ref, bt_ref, kt_ref, gend_ref, v_ref, y_ref, s_ref):
    c = CHUNK
    bsz = v_ref.shape[0]
    n = bsz * N_PAIR

    @pl.when(pl.program_id(1) == 0)
    def _():
        s_ref[...] = jnp.zeros_like(s_ref)

    bdm = bdm_ref[...]
    bdm_bf = bdm.astype(BF16)
    m_strict = msk_ref[M_STRICT]
    m_incl = msk_ref[M_INCL]

    def load(ref):
        return ref[...].reshape(n, c, LANES)

    def stack(z):
        z = z.astype(BF16)
        return jnp.concatenate([z, z], axis=1) * bdm_bf

    bt, kt, v = load(bt_ref), load(kt_ref), load(v_ref)
    ar = jnp.concatenate([load(at_ref), load(rt_ref)], axis=1)
    g_end = jnp.stack([gend_ref[q // N_PAIR, q % N_PAIR:q % N_PAIR + 1, :] for q in range(n)], axis=0)
    state = s_ref[...]

    gram = _bmm_nt(ar, jnp.concatenate([stack(bt), stack(kt)], axis=1))
    g_ab = gram[:, :c, :2 * c]
    from_v = jnp.concatenate([gram[:, :c, 2 * c:] * m_strict, gram[:, c:, 2 * c:] * m_incl], axis=1)
    partial = _bmm_nt(ar, state) + _bmm(from_v, stack(v))

    tri = msk_ref[M_EYE] + g_ab * msk_ref[M_LEVEL0]
    for lvl in range(1, N_LEVELS + 1):
        w = _bmm(g_ab * msk_ref[M_LEVEL0 + lvl], stack(tri))
        tri = tri + _bmm(tri, stack(w))

    u = _bmm(tri, stack(partial[:, :c]))
    y = partial[:, c:] + _bmm(gram[:, c:, :2 * c] * m_incl, stack(u))
    y_ref[...] = y.reshape(bsz, N_PAIR, c, LANES).astype(y_ref.dtype)
    upd = _bmm_tn(jnp.concatenate([u.astype(BF16), v], axis=1), jnp.concatenate([bt, kt], axis=1))
    s_ref[...] = (state + upd * bdm) * g_end


def _scan(at, rt, bt, kt, gend, v, masks, *, ctx):
    bsz, _, total, _ = v.shape
    c = CHUNK
    nc, ncc = total // c, ctx // c
    _, msk, bdm = masks

    def chunk(dr, j):
        rev = jnp.where(j < ncc, ncc - 1 - j, nc + ncc - 1 - j)
        return jnp.where(dr == 0, j, rev)

    shared = pl.BlockSpec((bsz, N_PAIR, c, LANES), lambda dr, j: (0, 0, chunk(dr, j), 0))
    per_dir = pl.BlockSpec((None, bsz, N_PAIR, c, LANES), lambda dr, j: (dr, 0, 0, chunk(dr, j), 0))
    ends = pl.BlockSpec((None, bsz, None, N_PAIR, LANES), lambda dr, j: (dr, 0, chunk(dr, j), 0, 0))
    return pl.pallas_call(
        _scan_kernel,
        grid=(2, nc),
        in_specs=[pl.BlockSpec((None, N_MASKS, c, 2 * c), lambda dr, j: (dr, 0, 0, 0)),
                  pl.BlockSpec((2 * c, 2 * c), lambda dr, j: (0, 0)),
                  per_dir, per_dir, per_dir, per_dir, ends, shared],
        out_specs=per_dir,
        out_shape=jax.ShapeDtypeStruct((2, bsz, N_PAIR, total, LANES), BF16),
        scratch_shapes=[pltpu.VMEM((bsz * N_PAIR, 2 * HEAD_DIM, 2 * HEAD_DIM), F32)],
        compiler_params=_cparams(("arbitrary", "arbitrary")),
    )(msk, bdm, at, rt, bt, kt, gend, v)


def _rope(z, cos, sin):
    lane = lax.broadcasted_iota(jnp.int32, (1, LANES), 1)
    first = jnp.bitwise_and(lane, HEAD_DIM // 2 - 1) < (HEAD_DIM // 4)
    out = []
    for p in range(z.shape[1] // LANES):
        zp = z[:, LANES * p:LANES * (p + 1)]
        partner = jnp.where(first, pltpu.roll(zp, LANES - HEAD_DIM // 4, 1), pltpu.roll(zp, HEAD_DIM // 4, 1))
        out.append(zp * cos + partner * sin)
    return jnp.concatenate(out, axis=1)


def _attn_kernel(sink_ref, q_ref, kp_ref, kc_ref, kn_ref, kx_ref, vp_ref, vc_ref, vn_ref, vx_ref,
                 cq_ref, sq_ref, cp_ref, sp_ref, cn_ref, sn_ref, o_ref, *, n_ctx_blk, n_blk):
    i = pl.program_id(1)
    blk = ATT_BLOCK
    f32 = lambda ref: ref[...].astype(F32)
    q = _rope(f32(q_ref), cq_ref[...], sq_ref[...]) * ATT_SCALE
    k_all = jnp.concatenate([_rope(f32(kp_ref), cp_ref[...], sp_ref[...]),
                             _rope(f32(kc_ref), cq_ref[...], sq_ref[...]),
                             _rope(f32(kn_ref), cn_ref[...], sn_ref[...]),
                             f32(kx_ref)], axis=0)
    v_all = jnp.concatenate([vp_ref[...], vc_ref[...], vn_ref[...], vx_ref[...]], axis=0)

    qi = lax.broadcasted_iota(jnp.int32, (blk, 3 * blk), 0)
    si = lax.broadcasted_iota(jnp.int32, (blk, 3 * blk), 1)
    rel = si - blk - qi
    key_blk = i - 1 + jnp.right_shift(si, LOG2_ATT_BLOCK)
    ok = (jnp.where(jnp.abs(rel) <= WINDOW, 1, 0) * jnp.where(key_blk >= n_ctx_blk, 1, 0)
          * jnp.where(key_blk < n_blk, 1, 0) * jnp.where(i >= n_ctx_blk, 1, 0))
    bias = jnp.where(ok > 0, 0.0, NEG_INF)

    lane_lo = lax.broadcasted_iota(jnp.int32, (1, LANES), 1) < HEAD_DIM
    group = N_HEADS // N_KV
    outs = [None] * N_HEADS
    bias_g = jnp.concatenate([bias] * group, axis=0)

    def scores(kvh):
        kv_slab = slice(LANES * (kvh // 2), LANES * (kvh // 2 + 1))
        kv_lo = kvh % 2 == 0
        rows = []
        for hq in range(group):
            h = kvh * group + hq
            qp = q[:, LANES * (h // 2):LANES * (h // 2 + 1)]
            if (h % 2 == 0) != kv_lo:
                qp = pltpu.roll(qp, HEAD_DIM, 1)
            rows.append(jnp.where(lane_lo, qp, 0.0) if kv_lo else jnp.where(lane_lo, 0.0, qp))
        s = _dot_nt(jnp.concatenate(rows, axis=0), k_all[:, kv_slab])
        return jnp.concatenate([s[:, :3 * blk] + bias_g, s[:, 3 * blk:]], axis=1)

    for kvh in range(N_KV):
        kv_slab = slice(LANES * (kvh // 2), LANES * (kvh // 2 + 1))
        kv_lo = kvh % 2 == 0
        s = scores(kvh)
        sink = jnp.concatenate([jnp.full((blk, 1), sink_ref[kvh * group + hq], F32) for hq in range(group)],
                               axis=0)
        m = jnp.maximum(jnp.max(s, axis=-1, keepdims=True), sink)
        e = jnp.exp(s - m)
        denom = jnp.sum(e, axis=-1, keepdims=True) + jnp.exp(sink - m)
        o = _dot(e, v_all[:, kv_slab]) / denom
        for hq in range(group):
            h = kvh * group + hq
            oh = o[hq * blk:(hq + 1) * blk]
            if (h % 2 == 0) != kv_lo:
                oh = pltpu.roll(oh, HEAD_DIM, 1)
            outs[h] = oh
    o_ref[...] = jnp.concatenate(
        [jnp.where(lane_lo, outs[2 * p], outs[2 * p + 1]) for p in range(N_PAIR)], axis=1).astype(o_ref.dtype)


def _attention(qkv, sink, cos, sin, *, ctx):
    bsz, total, _ = qkv.shape
    blk = ATT_BLOCK
    n_blk, n_ctx_blk = total // blk, ctx // blk
    kvw = KV_W
    kcol, vcol = B_K // kvw, B_V // kvw
    kern = functools.partial(_attn_kernel, n_ctx_blk=n_ctx_blk, n_blk=n_blk)
    prev_i = lambda i: jnp.maximum(i - 1, 0)
    next_i = lambda i: jnp.minimum(i + 1, n_blk - 1)

    def kv(col, row):
        return pl.BlockSpec((None, blk, kvw), lambda b, i: (b, row(i), col))

    def kv_ctx(col):
        return pl.BlockSpec((None, ctx, kvw), lambda b, i: (b, 0, col))

    def tab(row):
        return pl.BlockSpec((blk, LANES), lambda b, i: (row(i), 0))

    same = lambda i: i
    return pl.pallas_call(
        kern,
        grid=(bsz, n_blk),
        in_specs=[pl.BlockSpec(memory_space=pltpu.SMEM),
                  pl.BlockSpec((None, blk, D_MODEL), lambda b, i: (b, i, B_Q // D_MODEL)),
                  kv(kcol, prev_i), kv(kcol, same), kv(kcol, next_i), kv_ctx(kcol),
                  kv(vcol, prev_i), kv(vcol, same), kv(vcol, next_i), kv_ctx(vcol),
                  tab(same), tab(same), tab(prev_i), tab(prev_i), tab(next_i), tab(next_i)],
        out_specs=pl.BlockSpec((None, blk, D_MODEL), lambda b, i: (b, i, 0)),
        out_shape=jax.ShapeDtypeStruct((bsz, total, D_MODEL), BF16),
        compiler_params=_cparams(("arbitrary", "arbitrary")),
    )(sink, qkv, qkv, qkv, qkv, qkv, qkv, qkv, qkv, qkv, cos, sin, cos, sin, cos, sin)


def _merge_kernel(x_ref, mod_ref, cv_ref, before_ref, after_ref, gate_lo_ref, gate_hi_ref, att_ref, y_ref,
                  bonus_ref, g_ref,
                  cw_ref, lng_ref, lnb_ref, wa_ref, wb_ref, wc_ref, wo_ref, o_ref, *, tm, ctx, total):
    i = pl.program_id(0)
    bsz = x_ref.shape[0]
    d = D_MODEL
    cw = cw_ref[...]
    lng = lng_ref[...]
    lnb = lnb_ref[...]
    lane_lo = lax.broadcasted_iota(jnp.int32, (1, LANES), 1) < HEAD_DIM
    convs, rwkvs = [], []
    for b in range(bsz):
        cv = cv_ref[b].astype(F32)
        z = cv[:, d:2 * d] * cv[:, 2 * d:]
        before, after = _edge_rows(before_ref.at[b], after_ref.at[b])
        prev, nxt = _neighbours(z, before[:, d:2 * d] * before[:, 2 * d:], after[:, d:2 * d] * after[:, 2 * d:],
                                i, tm, ctx, total)
        convs.append((cv[:, :d] * (prev * cw[0:1] + z * cw[1:2] + nxt * cw[2:3])).astype(BF16))
        slabs = []
        for p in range(N_PAIR):
            sl = slice(LANES * p, LANES * (p + 1))
            y = y_ref[0, b, p].astype(F32) + y_ref[1, b, p].astype(F32)
            mean = _head_sums_exact(y, lane_lo) * (1.0 / HEAD_DIM)
            yc = y - mean
            var = _head_sums_exact(yc * yc, lane_lo) * (1.0 / HEAD_DIM)
            gn = yc * lax.rsqrt(var + GN_EPS) * lng[:, sl] + lnb[:, sl]
            slabs.append(((gn + bonus_ref[b, p]) * g_ref[b, p]).astype(BF16))
        rwkvs.append(jnp.concatenate(slabs, axis=1))

    rows = bsz * tm
    gates = jnp.concatenate([gate_lo_ref[...].reshape(rows, GATE_BLK), gate_hi_ref[...].reshape(rows, GATE_BLK)],
                            axis=1).astype(F32)
    m = (_sigmoid(gates[:, :d]) * _dot(jnp.concatenate(convs, axis=0), wa_ref[...])
         + _sigmoid(gates[:, d:2 * d]) * _dot(jnp.concatenate(rwkvs, axis=0), wb_ref[...])
         + _sigmoid(gates[:, 2 * d:]) * _dot(att_ref[...].reshape(rows, d), wc_ref[...]))
    out = _dot(m, wo_ref[...])
    is_ctx = _row_ids(i, tm) < ctx
    for b in range(bsz):
        o_ref[b] = x_ref[b] + _mod_row(mod_ref.at[b], 2, is_ctx) * out[b * tm:(b + 1) * tm]


def _merge(x, mod, proj_a, proj_b, att, y, bonus, g, conv_w, ln_g, ln_b, wa, wb, wc, wo, *, ctx, tm):
    bsz, total, d = x.shape
    kern = functools.partial(_merge_kernel, tm=tm, ctx=ctx, total=total)
    per, nblk = tm // HALO_BF16, total // HALO_BF16
    col_cv, col_gate = A_CONV // (3 * d), B_GATE // GATE_BLK
    before = pl.BlockSpec((bsz, HALO_BF16, 3 * d), lambda i: (0, jnp.maximum(i * per - 1, 0), col_cv))
    after = pl.BlockSpec((bsz, HALO_BF16, 3 * d), lambda i: (0, jnp.minimum((i + 1) * per, nblk - 1), col_cv))
    row = lambda w, col=0: pl.BlockSpec((bsz, tm, w), lambda i: (0, i, col))
    once = lambda shp: pl.BlockSpec(shp, lambda i: tuple(0 for _ in shp), pipeline_mode=pl.Buffered(1))
    pm = pl.BlockSpec((bsz, N_PAIR, tm, LANES), lambda i: (0, 0, i, 0))
    pm2 = pl.BlockSpec((2, bsz, N_PAIR, tm, LANES), lambda i: (0, 0, 0, i, 0))
    return pl.pallas_call(
        kern,
        grid=(total // tm,),
        in_specs=[row(d), pl.BlockSpec((bsz, 2, 6, d), lambda i: (0, 0, 0, 0)),
                  row(3 * d, col_cv), before, after, row(GATE_BLK, col_gate), row(GATE_BLK, col_gate + 1),
                  row(d), pm2, pm, pm,
                  once((3, d)), once((1, d)), once((1, d)),
                  once((d, d)), once((d, d)), once((d, d)), once((d, d))],
        out_specs=row(d),
        out_shape=jax.ShapeDtypeStruct((bsz, total, d), F32),
        compiler_params=_cparams(("arbitrary",)),
    )(x, mod, proj_a, proj_a, proj_a, proj_b, proj_b, att, y, bonus, g, conv_w, ln_g, ln_b, wa, wb, wc, wo)


FF_CHUNKS = 2
FF_CHUNK = D_FF // FF_CHUNKS
assert FF_CHUNK * FF_CHUNKS == D_FF and FF_CHUNK % LANES == 0


def _ffn_kernel(x_ref, before_ref, after_ref, mod_ref, g_ref, wu_ref, cw_ref, wd_ref, o_ref, *, tm, ctx, total):
    i = pl.program_id(0)
    bsz = x_ref.shape[0]
    start = i * tm
    ext = tm + 2 * HALO
    is_ctx = (start - HALO + lax.broadcasted_iota(jnp.int32, (ext, 1), 0)) < ctx
    hs = []
    for b in range(bsz):
        xe = jnp.concatenate([before_ref[b], x_ref[b], after_ref[b]], axis=0)
        y = xe * lax.rsqrt(jnp.mean(xe * xe, axis=-1, keepdims=True) + NORM_EPS) * g_ref[...]
        mod_b = mod_ref.at[b]
        hs.append((y * (1.0 + _mod_row(mod_b, 4, is_ctx)) + _mod_row(mod_b, 3, is_ctx)).astype(BF16))
    h = jnp.concatenate(hs, axis=0)

    keep_before = jnp.where((start == 0) | (start == ctx), 0.0, 1.0)
    keep_after = jnp.where((start + tm == ctx) | (start + tm == total), 0.0, 1.0)
    row = lax.broadcasted_iota(jnp.int32, (HALO, 1), 0)
    first_row = row == 0
    last_row = row == HALO - 1

    def conv(u, cw):
        prev_all = pltpu.roll(u, 1, 0)
        nxt_all = pltpu.roll(u, bsz * ext - 1, 0)
        out = []
        for b in range(bsz):
            rows = slice(b * ext + HALO, b * ext + HALO + tm)
            prev, nxt = prev_all[rows], nxt_all[rows]
            top = jnp.where(first_row, prev[:HALO] * keep_before, prev[:HALO])
            bottom = jnp.where(last_row, nxt[tm - HALO:] * keep_after, nxt[tm - HALO:])
            prev = jnp.concatenate([top, prev[HALO:]], axis=0)
            nxt = jnp.concatenate([nxt[:tm - HALO], bottom], axis=0)
            out.append(prev * cw[0:1] + u[rows] * cw[1:2] + nxt * cw[2:3])
        return jnp.concatenate(out, axis=0)

    def cols(j):
        return slice(FF_CHUNK * j, FF_CHUNK * (j + 1)), slice(D_FF + FF_CHUNK * j, D_FF + FF_CHUNK * (j + 1))

    def up(j):
        gate_cols, val_cols = cols(j)
        return _dot(h, wu_ref[:, gate_cols]), _dot(h, wu_ref[:, val_cols])

    acts = []
    u_next = up(0)
    for j in range(FF_CHUNKS):
        gate_cols, val_cols = cols(j)
        u_gate, u_val = u_next
        if j + 1 < FF_CHUNKS:
            u_next = up(j + 1)
        ug = conv(u_gate, cw_ref[:, gate_cols])
        uv = conv(u_val, cw_ref[:, val_cols])
        acts.append((ug * _sigmoid(ug) * uv).astype(BF16))
    acc = _dot(jnp.concatenate(acts, axis=1), wd_ref[...])
    tile_is_ctx = _row_ids(i, tm) < ctx
    for b in range(bsz):
        o_ref[b] = x_ref[b] + _mod_row(mod_ref.at[b], 5, tile_is_ctx) * acc[b * tm:(b + 1) * tm]


def _ffn(x, mod, g, wu, conv_w, wd, *, ctx, tm):
    bsz, total, d = x.shape
    kern = functools.partial(_ffn_kernel, tm=tm, ctx=ctx, total=total)
    per, nblk = tm // HALO, total // HALO
    before = pl.BlockSpec((bsz, HALO, d), lambda i: (0, jnp.maximum(i * per - 1, 0), 0))
    after = pl.BlockSpec((bsz, HALO, d), lambda i: (0, jnp.minimum((i + 1) * per, nblk - 1), 0))
    row = pl.BlockSpec((bsz, tm, d), lambda i: (0, i, 0))
    once = lambda shp: pl.BlockSpec(shp, lambda i: tuple(0 for _ in shp), pipeline_mode=pl.Buffered(1))
    return pl.pallas_call(
        kern,
        grid=(total // tm,),
        in_specs=[row, before, after, pl.BlockSpec((bsz, 2, 6, d), lambda i: (0, 0, 0, 0)),
                  once((1, d)), once((d, 2 * D_FF)), once((3, 2 * D_FF)), once((D_FF, d))],
        out_specs=row,
        out_shape=jax.ShapeDtypeStruct((bsz, total, d), F32),
        compiler_params=_cparams(("arbitrary",)),
    )(x, x, x, mod, g, wu, conv_w, wd)


def _final_norm_kernel(x_ref, g_ref, o_ref):
    x = x_ref[...]
    o_ref[...] = x * lax.rsqrt(jnp.mean(x * x, axis=-1, keepdims=True) + NORM_EPS) * g_ref[...]


def _final_norm(x, g, *, ctx, seq, tm):
    bsz, _, d = x.shape
    off = ctx // tm
    return pl.pallas_call(
        _final_norm_kernel,
        grid=(bsz, seq // tm),
        in_specs=[pl.BlockSpec((None, tm, d), lambda b, i: (b, i + off, 0)),
                  pl.BlockSpec((1, d), lambda b, i: (0, 0))],
        out_specs=pl.BlockSpec((None, tm, d), lambda b, i: (b, i, 0)),
        out_shape=jax.ShapeDtypeStruct((bsz, seq, d), F32),
        compiler_params=_cparams(("arbitrary", "arbitrary")),
    )(x, g)


def _rope_tables(ctx, seq):
    rows = seq // GRID_W
    t_row = jnp.broadcast_to(jnp.arange(rows)[:, None], (rows, GRID_W)).reshape(-1).astype(F32)
    t_col = jnp.broadcast_to(jnp.arange(GRID_W)[None, :], (rows, GRID_W)).reshape(-1).astype(F32)
    n_freq = HEAD_DIM // 4
    inv = ROPE_THETA ** (-jnp.arange(n_freq, dtype=F32) / n_freq)
    ar, ac = t_row[:, None] * inv, t_col[:, None] * inv
    cos = jnp.concatenate([jnp.cos(ar), jnp.cos(ar), jnp.cos(ac), jnp.cos(ac)], axis=1)
    sin = jnp.concatenate([-jnp.sin(ar), jnp.sin(ar), -jnp.sin(ac), jnp.sin(ac)], axis=1)
    cos = jnp.concatenate([jnp.ones((ctx, HEAD_DIM), F32), cos], axis=0)
    sin = jnp.concatenate([jnp.zeros((ctx, HEAD_DIM), F32), sin], axis=0)
    return jnp.tile(cos, (1, LANES // HEAD_DIM)), jnp.tile(sin, (1, LANES // HEAD_DIM))


def _block_diag2(w):
    z = jnp.zeros_like(w[0])
    return jnp.concatenate([jnp.concatenate([w[0], z], axis=1), jnp.concatenate([z, w[1]], axis=1)], axis=0)


def kernel(x, c, ctx, c_ctx, ada_w, ada_b, norm1_g, w_in, conv_a_w, a_out_w, rwkv_mu, rwkv_w0, rwkv_w2,
           rwkv_a0, rwkv_a2, rwkv_g2, rwkv_k_k, rwkv_k_a, rwkv_r_k, rwkv_ln_g, rwkv_ln_b, rwkv_out_w,
           attn_sink, attn_out_w, w_o, norm2_g, ffn_up, ffn_conv, ffn_down, final_norm_g):
    bsz, seq, d = x.shape
    n_ctx = ctx.shape[1]
    depth = ada_w.shape[0]
    total = n_ctx + seq
    assert d == D_MODEL and seq % ATT_BLOCK == 0 and n_ctx % ATT_BLOCK == 0 and bsz + 1 <= HALO
    tm = 256
    assert total % tm == 0 and n_ctx % tm == 0

    cvec = jnp.zeros((HALO, d), F32).at[:bsz].set(c).at[bsz].set(c_ctx)
    mods = _ada(cvec, ada_w.astype(BF16), ada_b)
    mods = mods.reshape(depth, HALO, 6, d)
    mods = jnp.stack([jnp.broadcast_to(mods[:, bsz:bsz + 1], (depth, bsz, 6, d)), mods[:, :bsz]], axis=2)

    cos, sin = _rope_tables(n_ctx, seq)
    masks = _scan_masks(tm)
    head_ones = masks[2].astype(BF16)
    xa =jnp.concatenate([ctx, x], axis=1)

    for l in range(depth):
        mod = mods[l]
        proj_a, proj_b = _proj(xa, mod, norm1_g[l][None], w_in[l].astype(BF16), ctx=n_ctx, tm=tm // 2)

        at, rt, bt, kt, gend, v, bonus, g = _prep(
            proj_a, masks[0], head_ones, rwkv_mu[l], rwkv_w0[l].reshape(1, 2 * d), _block_diag2(rwkv_w2[l]).astype(BF16),
            rwkv_a0[l].reshape(1, 2 * d), _block_diag2(rwkv_a2[l]).astype(BF16), rwkv_g2[l].astype(BF16),
            rwkv_k_k[l][None], rwkv_k_a[l][None], rwkv_r_k[l].reshape(1, d), ctx=n_ctx, tm=tm)
        y = _scan(at, rt, bt, kt, gend, v, masks, ctx=n_ctx)
        att = _attention(proj_b, attn_sink[l], cos, sin, ctx=n_ctx)
        xa = _merge(xa, mod, proj_a, proj_b, att, y, bonus, g, conv_a_w[l], rwkv_ln_g[l][None], rwkv_ln_b[l][None],
                    a_out_w[l].astype(BF16), rwkv_out_w[l].astype(BF16), attn_out_w[l].astype(BF16),
                    w_o[l].astype(BF16), ctx=n_ctx, tm=tm)
        xa = _ffn(xa, mod, norm2_g[l][None], ffn_up[l].astype(BF16), ffn_conv[l], ffn_down[l].astype(BF16),
                  ctx=n_ctx, tm=tm)
    return _final_norm(xa, final_norm_g[None], ctx=n_ctx, seq=seq, tm=tm)
```

```python
import functools

import jax
import jax.numpy as jnp
import numpy as np
from jax import lax
from jax.experimental import pallas as pl
from jax.experimental.pallas import tpu as pltpu

D_MODEL = 1024
GRID_W = 64
N_HEADS = 16
HEAD_DIM = 64
N_KV = 4
R_LORA = 64
R_GATE = 128
GN_EPS = 64e-5
WINDOW = 128
ATT_BLOCK = 128
LOG2_ATT_BLOCK = 7
ATT_STACK = N_HEADS // N_KV
ATT_SCALE = HEAD_DIM ** -0.5
ROPE_THETA = 10000.0
NEG_INF = -1e30
D_FF = 2816
NORM_EPS = 1e-6
KK_EPS = 1e-12
DECAY_SCALE = float(np.exp(-0.5))

LANES = 128
HALO = 8
HALO_BF16 = 16
N_PAIR = D_MODEL // LANES
CHUNK = 64
N_LEVELS = 5
VMEM_LIMIT = 52 * 1024 * 1024

BF16 = jnp.bfloat16
F32 = jnp.float32

KV_W = N_KV * HEAD_DIM
LORA_W = 2 * R_LORA + 2 * R_LORA + R_GATE
N_PROJ_A = 6 * D_MODEL + LORA_W
A_CONV, A_RKV, A_LORA = 0, 3 * D_MODEL, 6 * D_MODEL
N_PROJ_B = D_MODEL + 2 * KV_W + 3 * D_MODEL
B_Q, B_K, B_V, B_GATE = 0, D_MODEL, D_MODEL + KV_W, D_MODEL + 2 * KV_W
GATE_BLK = B_GATE
assert A_LORA % LORA_W == 0 and B_K % KV_W == 0 and 2 * GATE_BLK == 3 * D_MODEL and GATE_BLK % LANES == 0


def _cparams(sem):
    return pltpu.CompilerParams(dimension_semantics=sem, vmem_limit_bytes=VMEM_LIMIT)


def _dot(a, b):
    return jnp.dot(a.astype(BF16), b.astype(BF16), preferred_element_type=F32)


def _dot_nt(a, b):
    return lax.dot_general(a.astype(BF16), b.astype(BF16), (((1,), (1,)), ((), ())),
                           preferred_element_type=F32)


def _bmm(a, b):
    return jnp.einsum('pmk,pkn->pmn', a.astype(BF16), b.astype(BF16), preferred_element_type=F32)


def _bmm_nt(a, b):
    return jnp.einsum('pmk,pnk->pmn', a.astype(BF16), b.astype(BF16), preferred_element_type=F32)


def _bmm_tn(a, b):
    return jnp.einsum('pkm,pkn->pmn', a.astype(BF16), b.astype(BF16), preferred_element_type=F32)


def _sigmoid(x):
    return 0.5 + 0.5 * jnp.tanh(0.5 * x)


def _row_ids(i, tm):
    return i * tm + lax.broadcasted_iota(jnp.int32, (tm, 1), 0)


def _mod_row(mod_ref, k, is_ctx):
    return jnp.where(is_ctx, mod_ref[0, k:k + 1, :], mod_ref[1, k:k + 1, :])


def _neighbours(z, before, after, i, tm, ctx, total):
    start = i * tm
    keep_before = jnp.where((start == 0) | (start == ctx), 0.0, 1.0)
    keep_after = jnp.where((start + tm == ctx) | (start + tm == total), 0.0, 1.0)
    row = lax.broadcasted_iota(jnp.int32, (HALO, 1), 0)
    prev = pltpu.roll(z, 1, 0)
    nxt = pltpu.roll(z, tm - 1, 0)
    top = jnp.where(row == 0, before * keep_before, prev[:HALO])
    bottom = jnp.where(row == HALO - 1, after * keep_after, nxt[tm - HALO:])
    return (jnp.concatenate([top, prev[HALO:]], axis=0),
            jnp.concatenate([nxt[:tm - HALO], bottom], axis=0))


def _layer_weight(shape, layer):
    return pl.BlockSpec((None,) + shape, lambda *_: (layer,) + tuple(0 for _ in shape),
                        pipeline_mode=pl.Buffered(1))


def _halo_specs(width, tm, total, col=0, rows=HALO):
    per = tm // rows
    nblk = total // rows
    before = pl.BlockSpec((None, rows, width), lambda b, i: (b, jnp.maximum(i * per - 1, 0), col))
    after = pl.BlockSpec((None, rows, width), lambda b, i: (b, jnp.minimum((i + 1) * per, nblk - 1), col))
    return before, after


def _edge_rows(before_ref, after_ref):
    before = before_ref[...].astype(F32)
    return before[before.shape[0] - 1:], after_ref[...].astype(F32)[0:1]


def _ada_kernel(c_ref, w_ref, b_ref, o_ref):
    c = c_ref[...]
    o_ref[...] = _dot(c * _sigmoid(c), w_ref[...]) + b_ref[...]


def _ada(cvec, ada_w, ada_b):
    depth, d, n = ada_w.shape
    tn = 1536
    return pl.pallas_call(
        _ada_kernel,
        grid=(depth, n // tn),
        in_specs=[pl.BlockSpec((HALO, d), lambda l, j: (0, 0)),
                  pl.BlockSpec((None, d, tn), lambda l, j: (l, 0, j)),
                  pl.BlockSpec((None, 1, tn), lambda l, j: (l, 0, j))],
        out_specs=pl.BlockSpec((None, HALO, tn), lambda l, j: (l, 0, j)),
        out_shape=jax.ShapeDtypeStruct((depth, HALO, n), F32),
        compiler_params=_cparams(("arbitrary", "arbitrary")),
    )(cvec, ada_w, ada_b.reshape(depth, 1, n))


PROJ_SPLITS = ((0, 2176), (2176, 4352), (4352, N_PROJ_A))
PROJ_SPLITS_B = ((0, 2304), (2304, N_PROJ_B))


def _proj_kernel(x_ref, mod_ref, g_ref, w_ref, oa_ref, ob_ref, *, tm, ctx):
    i = pl.program_id(0)
    bsz = x_ref.shape[0]
    is_ctx = _row_ids(i, tm) < ctx
    hs = []
    for b in range(bsz):
        x = x_ref[b]
        y = x * lax.rsqrt(jnp.mean(x * x, axis=-1, keepdims=True) + NORM_EPS) * g_ref[...]
        mod_b = mod_ref.at[b]
        hs.append((y * (1.0 + _mod_row(mod_b, 1, is_ctx)) + _mod_row(mod_b, 0, is_ctx)).astype(BF16))
    h = jnp.concatenate(hs, axis=0)
    for o_ref, base, splits in ((oa_ref, 0, PROJ_SPLITS), (ob_ref, N_PROJ_A, PROJ_SPLITS_B)):
        for lo, hi in splits:
            out = _dot(h, w_ref[:, base + lo:base + hi])
            o_ref[:, :, lo:hi] = out.reshape(bsz, tm, hi - lo).astype(o_ref.dtype)


def _proj(x, mod, g, w, layer, *, ctx, tm):
    bsz, total, d = x.shape
    assert w.shape[2] == N_PROJ_A + N_PROJ_B
    kern = functools.partial(_proj_kernel, tm=tm, ctx=ctx)
    once = lambda shp: pl.BlockSpec(shp, lambda i: tuple(0 for _ in shp), pipeline_mode=pl.Buffered(1))
    out = lambda n: pl.BlockSpec((bsz, tm, n), lambda i: (0, i, 0))
    return pl.pallas_call(
        kern,
        grid=(total // tm,),
        in_specs=[pl.BlockSpec((bsz, tm, d), lambda i: (0, i, 0)),
                  pl.BlockSpec((bsz, 2, 6, d), lambda i: (0, 0, 0, 0)),
                  once((1, d)), _layer_weight((d, N_PROJ_A + N_PROJ_B), layer)],
        out_specs=[out(N_PROJ_A), out(N_PROJ_B)],
        out_shape=[jax.ShapeDtypeStruct((bsz, total, N_PROJ_A), BF16),
                   jax.ShapeDtypeStruct((bsz, total, N_PROJ_B), BF16)],
        compiler_params=_cparams(("arbitrary",)),
    )(x, mod, g, w)


def _head_sums(z, head_ones):
    return jnp.dot(z.astype(BF16), head_ones, preferred_element_type=F32)


def _head_sums_exact(z, lane_lo):
    s_lo = jnp.sum(jnp.where(lane_lo, z, 0.0), axis=-1, keepdims=True)
    s_hi = jnp.sum(jnp.where(lane_lo, 0.0, z), axis=-1, keepdims=True)
    return jnp.where(lane_lo, s_lo, s_hi)


def _chunk_cumsum(tri, z):
    hi = z.astype(BF16)
    low = (z - hi.astype(F32)).astype(BF16)
    dot = functools.partial(jnp.dot, preferred_element_type=F32)
    return dot(tri, low) + dot(tri, hi)


def _prep_kernel(rkv_ref, before_ref, after_ref, lora_ref, tri_ref, ones_ref, mu_ref, w0_ref, w2_ref, a0_ref,
                 a2_ref, g2_ref, kk_ref, ka_ref, rk_ref,
                 at_out, rt_out, bt_out, kt_out, gend_out, v_out, bonus_out, g_out, *, tm, ctx, total):
    i = pl.program_id(1)
    x = rkv_ref[...].astype(F32)
    prev, nxt = _neighbours(x, *_edge_rows(before_ref, after_ref), i, tm, ctx, total)
    mu_prev = mu_ref[0:1, :]
    mu_next = mu_ref[1:2, :]
    s = prev * mu_prev + x * (1.0 - mu_prev - mu_next) + nxt * mu_next
    d = D_MODEL
    r, k, v = s[:, :d], s[:, d:2 * d], s[:, 2 * d:]

    lo = lora_ref[...].astype(F32)
    w_pre = w0_ref[...] + _dot(jnp.tanh(lo[:, :LANES]), w2_ref[...])
    lw = -DECAY_SCALE * _sigmoid(w_pre)
    a = _sigmoid(a0_ref[...] + _dot(lo[:, LANES:2 * LANES], a2_ref[...]))
    g = _dot(_sigmoid(lo[:, 2 * LANES:]), g2_ref[...])
    kk = k * kk_ref[...]
    ka = ka_ref[...]
    rk = rk_ref[...]
    cl = [_chunk_cumsum(tri_ref[dr], lw[:, d * dr:d * (dr + 1)]) for dr in range(2)]

    head_ones = ones_ref[...]
    for p in range(N_PAIR):
        sl = slice(LANES * p, LANES * (p + 1))
        kkp = kk[:, sl]
        kn = kkp * lax.rsqrt(_head_sums(kkp * kkp, head_ones) + KK_EPS)
        rp, kp, vp = r[:, sl], k[:, sl], v[:, sl]
        bonus = jnp.zeros_like(rp)
        for dr in range(2):
            dsl = slice(d * dr + LANES * p, d * dr + LANES * (p + 1))
            ad = a[:, dsl]
            kd = kp * (1.0 + (ad - 1.0) * ka[:, sl])
            bonus = bonus + _head_sums(rp * kd * rk[:, sl], head_ones) * vp
            cl_incl = cl[dr][:, sl]
            inv_g = jnp.exp(-cl_incl)
            at_out[dr, p] = (-kn * jnp.exp(cl_incl - lw[:, dsl])).astype(BF16)
            rt_out[dr, p] = (rp * jnp.exp(cl_incl)).astype(BF16)
            bt_out[dr, p] = (kn * ad * inv_g).astype(BF16)
            kt_out[dr, p] = (kd * inv_g).astype(BF16)
            for q in range(tm // CHUNK):
                end = CHUNK * q + (CHUNK - 1 if dr == 0 else 0)
                gend_out[dr, q, p:p + 1, :] = jnp.exp(cl_incl[end:end + 1, :])
        v_out[p] = vp.astype(BF16)
        bonus_out[p] = bonus.astype(BF16)
        g_out[p] = g[:, sl].astype(BF16)


def _prep(proj, tri, head_ones, mu, w0, w2bd, a0, a2bd, g2, k_k, k_a, r_k, *, ctx, tm):
    bsz, total, _ = proj.shape
    d = D_MODEL
    kern = functools.partial(_prep_kernel, tm=tm, ctx=ctx, total=total)
    before, after = _halo_specs(3 * d, tm, total, A_RKV // (3 * d), HALO_BF16)
    full = lambda shp: pl.BlockSpec(shp, lambda b, i: tuple(0 for _ in shp))
    pm = pl.BlockSpec((None, N_PAIR, tm, LANES), lambda b, i: (b, 0, i, 0))
    pm2 = pl.BlockSpec((2, None, N_PAIR, tm, LANES), lambda b, i: (0, b, 0, i, 0))
    ge = pl.BlockSpec((2, None, tm // CHUNK, N_PAIR, LANES), lambda b, i: (0, b, i, 0, 0))
    sds = jax.ShapeDtypeStruct((bsz, N_PAIR, total, LANES), BF16)
    sds2 = jax.ShapeDtypeStruct((2, bsz, N_PAIR, total, LANES), BF16)
    sds_ge = jax.ShapeDtypeStruct((2, bsz, total // CHUNK, N_PAIR, LANES), F32)
    return pl.pallas_call(
        kern,
        grid=(bsz, total // tm),
        in_specs=[pl.BlockSpec((None, tm, 3 * d), lambda b, i: (b, i, A_RKV // (3 * d))), before, after,
                  pl.BlockSpec((None, tm, LORA_W), lambda b, i: (b, i, A_LORA // LORA_W)),
                  full((2, tm, tm)), full((LANES, LANES)), full((2, 3 * d)), full((1, 2 * d)), full((LANES, 2 * d)),
                  full((1, 2 * d)),
                  full((LANES, 2 * d)), full((R_GATE, d)), full((1, d)), full((1, d)), full((1, d))],
        out_specs=[pm2, pm2, pm2, pm2, ge, pm, pm, pm],
        out_shape=[sds2, sds2, sds2, sds2, sds_ge, sds, sds, sds],
        compiler_params=_cparams(("arbitrary", "arbitrary")),
    )(proj, proj, proj, proj, tri, head_ones, mu, w0, w2bd, a0, a2bd, g2, k_k, k_a, r_k)


M_STRICT, M_INCL, M_EYE, M_LEVEL0 = 0, 1, 2, 3
N_MASKS = M_LEVEL0 + 1 + N_LEVELS


def _scan_masks(tm):
    c = CHUNK
    t = np.arange(c)[:, None]
    s = np.arange(c)[None, :]
    cum = np.zeros((2, tm, tm), np.float32)
    msk = np.zeros((2, N_MASKS, c, 2 * c), np.float32)
    for dr in range(2):
        earlier = (s < t) if dr == 0 else (s > t)
        cum[dr] = np.kron(np.eye(tm // c), earlier | (s == t))
        planes = [earlier, earlier | (s == t), s == t]
        m = 1
        while m < c:
            same = (t // (2 * m)) == (s // (2 * m))
            t_late = (t % (2 * m)) >= m
            s_late = (s % (2 * m)) >= m
            planes.append(same & (t_late & ~s_late if dr == 0 else ~t_late & s_late))
            m *= 2
        for q, plane in enumerate(planes):
            msk[dr, q] = np.concatenate([plane, plane], axis=1)
    bdm = np.kron(np.eye(2, dtype=np.float32), np.ones((c, c), np.float32))
    return jnp.asarray(cum, BF16), jnp.asarray(msk), jnp.asarray(bdm)


SCAN_GROUP = 4


def _scan_kernel(msk_ref, bdm_ref, at_ref, rt_ref, bt_ref, kt_ref, gend_ref, v_ref, y_ref, s_ref, *, reverse):
    c = CHUNK
    bsz = v_ref.shape[0]
    n = bsz * N_PAIR
    order = tuple(reversed(range(SCAN_GROUP))) if reverse else tuple(range(SCAN_GROUP))

    @pl.when(pl.program_id(0) == 0)
    def _():
        s_ref[...] = jnp.zeros_like(s_ref)

    bdm = bdm_ref[...]
    bdm_bf = bdm.astype(BF16)
    m_strict = msk_ref[M_STRICT]
    m_incl = msk_ref[M_INCL]

    def load(ref):
        z = ref[...].reshape(n, SCAN_GROUP * c, LANES)
        return jnp.concatenate([z[:, c * k:c * (k + 1)] for k in range(SCAN_GROUP)], axis=0)

    def stack(z):
        z = z.astype(BF16)
        return jnp.concatenate([z, z], axis=1) * bdm_bf

    bt, kt, v = load(bt_ref), load(kt_ref), load(v_ref)
    ar = jnp.concatenate([load(at_ref), load(rt_ref)], axis=1)
    gram = _bmm_nt(ar, jnp.concatenate([stack(bt), stack(kt)], axis=1))
    g_ab = gram[:, :c, :2 * c]
    from_v = jnp.concatenate([gram[:, :c, 2 * c:] * m_strict, gram[:, c:, 2 * c:] * m_incl], axis=1)
    from_v = _bmm(from_v, stack(v))
    y_from_u = gram[:, c:, :2 * c] * m_incl

    tri = msk_ref[M_EYE] + g_ab * msk_ref[M_LEVEL0]
    for lvl in range(1, N_LEVELS + 1):
        w = _bmm(g_ab * msk_ref[M_LEVEL0 + lvl], stack(tri))
        tri = tri + _bmm(tri, stack(w))

    state = s_ref[...]
    ys = [None] * SCAN_GROUP
    for k in order:
        sel = slice(n * k, n * (k + 1))
        g_end = jnp.stack([gend_ref[q // N_PAIR, k, q % N_PAIR:q % N_PAIR + 1, :] for q in range(n)], axis=0)
        partial = _bmm_nt(ar[sel], state) + from_v[sel]
        u = _bmm(tri[sel], stack(partial[:, :c]))
        ys[k] = partial[:, c:] + _bmm(y_from_u[sel], stack(u))
        upd = _bmm_tn(jnp.concatenate([u.astype(BF16), v[sel]], axis=1), jnp.concatenate([bt[sel], kt[sel]], axis=1))
        state = (state + upd * bdm) * g_end
    s_ref[...] = state
    y_ref[...] = jnp.concatenate(ys, axis=1).reshape(bsz, N_PAIR, SCAN_GROUP * c, LANES).astype(y_ref.dtype)


def _scan(at, rt, bt, kt, gend, v, masks, *, ctx, direction):
    bsz, _, total, _ = v.shape
    rows = SCAN_GROUP * CHUNK
    steps, ctx_steps = total // rows, ctx // rows
    assert steps * rows == total and ctx_steps * rows == ctx
    _, msk, bdm = masks

    def blk(j):
        return j if direction == 0 else jnp.where(j < ctx_steps, ctx_steps - 1 - j, steps + ctx_steps - 1 - j)

    shared = pl.BlockSpec((bsz, N_PAIR, rows, LANES), lambda j: (0, 0, blk(j), 0))
    per_dir = pl.BlockSpec((None, bsz, N_PAIR, rows, LANES), lambda j: (direction, 0, 0, blk(j), 0))
    ends = pl.BlockSpec((None, bsz, SCAN_GROUP, N_PAIR, LANES), lambda j: (direction, 0, blk(j), 0, 0))
    return pl.pallas_call(
        functools.partial(_scan_kernel, reverse=direction == 1),
        grid=(steps,),
        in_specs=[pl.BlockSpec((None, N_MASKS, CHUNK, 2 * CHUNK), lambda j: (direction, 0, 0, 0)),
                  pl.BlockSpec((2 * CHUNK, 2 * CHUNK), lambda j: (0, 0)),
                  per_dir, per_dir, per_dir, per_dir, ends, shared],
        out_specs=shared,
        out_shape=jax.ShapeDtypeStruct((bsz, N_PAIR, total, LANES), BF16),
        scratch_shapes=[pltpu.VMEM((bsz * N_PAIR, 2 * HEAD_DIM, 2 * HEAD_DIM), F32)],
        compiler_params=_cparams(("arbitrary",)),
    )(msk, bdm, at, rt, bt, kt, gend, v)


def _rope(z, cos, sin):
    lane = lax.broadcasted_iota(jnp.int32, (1, LANES), 1)
    first = jnp.bitwise_and(lane, HEAD_DIM // 2 - 1) < (HEAD_DIM // 4)
    out = []
    for p in range(z.shape[1] // LANES):
        zp = z[:, LANES * p:LANES * (p + 1)]
        partner = jnp.where(first, pltpu.roll(zp, LANES - HEAD_DIM // 4, 1), pltpu.roll(zp, HEAD_DIM // 4, 1))
        out.append(zp * cos + partner * sin)
    return jnp.concatenate(out, axis=1)


def _attn_kernel(sink_ref, q_ref, kp_ref, kc_ref, kn_ref, kx_ref, vp_ref, vc_ref, vn_ref, vx_ref,
                 cq_ref, sq_ref, cp_ref, sp_ref, cn_ref, sn_ref, o_ref, *, n_ctx_blk, n_blk):
    i = pl.program_id(1)
    blk = ATT_BLOCK
    f32 = lambda ref: ref[...].astype(F32)
    q = _rope(f32(q_ref), cq_ref[...], sq_ref[...]) * ATT_SCALE
    k_all = jnp.concatenate([_rope(f32(kp_ref), cp_ref[...], sp_ref[...]),
                             _rope(f32(kc_ref), cq_ref[...], sq_ref[...]),
                             _rope(f32(kn_ref), cn_ref[...], sn_ref[...]),
                             f32(kx_ref)], axis=0)
    v_all = jnp.concatenate([vp_ref[...], vc_ref[...], vn_ref[...], vx_ref[...]], axis=0)

    qi = lax.broadcasted_iota(jnp.int32, (blk, 3 * blk), 0)
    si = lax.broadcasted_iota(jnp.int32, (blk, 3 * blk), 1)
    rel = si - blk - qi
    key_blk = i - 1 + jnp.right_shift(si, LOG2_ATT_BLOCK)
    ok = (jnp.where(jnp.abs(rel) <= WINDOW, 1, 0) * jnp.where(key_blk >= n_ctx_blk, 1, 0)
          * jnp.where(key_blk < n_blk, 1, 0) * jnp.where(i >= n_ctx_blk, 1, 0))
    bias = jnp.where(ok > 0, 0.0, NEG_INF)

    lane_lo = lax.broadcasted_iota(jnp.int32, (1, LANES), 1) < HEAD_DIM
    stack = ATT_STACK
    outs = [None] * N_HEADS
    bias_g = jnp.concatenate([bias] * stack, axis=0)

    for h0 in range(0, N_HEADS, stack):
        kvh = h0 // (N_HEADS // N_KV)
        kv_slab = slice(LANES * (kvh // 2), LANES * (kvh // 2 + 1))
        kv_lo = kvh % 2 == 0
        rows = []
        for h in range(h0, h0 + stack):
            qp = q[:, LANES * (h // 2):LANES * (h // 2 + 1)]
            if (h % 2 == 0) != kv_lo:
                qp = pltpu.roll(qp, HEAD_DIM, 1)
            rows.append(jnp.where(lane_lo, qp, 0.0) if kv_lo else jnp.where(lane_lo, 0.0, qp))
        s = _dot_nt(jnp.concatenate(rows, axis=0), k_all[:, kv_slab])
        s = jnp.concatenate([s[:, :3 * blk] + bias_g, s[:, 3 * blk:]], axis=1)
        sink = jnp.concatenate([jnp.full((blk, 1), sink_ref[h], F32) for h in range(h0, h0 + stack)], axis=0)
        m = jnp.maximum(jnp.max(s, axis=-1, keepdims=True), sink)
        e = jnp.exp(s - m)
        denom = jnp.sum(e, axis=-1, keepdims=True) + jnp.exp(sink - m)
        o = _dot(e, v_all[:, kv_slab]) / denom
        for j, h in enumerate(range(h0, h0 + stack)):
            oh = o[j * blk:(j + 1) * blk]
            if (h % 2 == 0) != kv_lo:
                oh = pltpu.roll(oh, HEAD_DIM, 1)
            outs[h] = oh
    o_ref[...] = jnp.concatenate(
        [jnp.where(lane_lo, outs[2 * p], outs[2 * p + 1]) for p in range(N_PAIR)], axis=1).astype(o_ref.dtype)


def _attention(qkv, sink, cos, sin, *, ctx):
    bsz, total, _ = qkv.shape
    blk = ATT_BLOCK
    n_blk, n_ctx_blk = total // blk, ctx // blk
    kvw = KV_W
    kcol, vcol = B_K // kvw, B_V // kvw
    kern = functools.partial(_attn_kernel, n_ctx_blk=n_ctx_blk, n_blk=n_blk)
    prev_i = lambda i: jnp.maximum(i - 1, 0)
    next_i = lambda i: jnp.minimum(i + 1, n_blk - 1)

    def kv(col, row):
        return pl.BlockSpec((None, blk, kvw), lambda b, i: (b, row(i), col))

    def kv_ctx(col):
        return pl.BlockSpec((None, ctx, kvw), lambda b, i: (b, 0, col))

    def tab(row):
        return pl.BlockSpec((blk, LANES), lambda b, i: (row(i), 0))

    same = lambda i: i
    return pl.pallas_call(
        kern,
        grid=(bsz, n_blk),
        in_specs=[pl.BlockSpec(memory_space=pltpu.SMEM),
                  pl.BlockSpec((None, blk, D_MODEL), lambda b, i: (b, i, B_Q // D_MODEL)),
                  kv(kcol, prev_i), kv(kcol, same), kv(kcol, next_i), kv_ctx(kcol),
                  kv(vcol, prev_i), kv(vcol, same), kv(vcol, next_i), kv_ctx(vcol),
                  tab(same), tab(same), tab(prev_i), tab(prev_i), tab(next_i), tab(next_i)],
        out_specs=pl.BlockSpec((None, blk, D_MODEL), lambda b, i: (b, i, 0)),
        out_shape=jax.ShapeDtypeStruct((bsz, total, D_MODEL), BF16),
        compiler_params=_cparams(("arbitrary", "arbitrary")),
    )(sink, qkv, qkv, qkv, qkv, qkv, qkv, qkv, qkv, qkv, cos, sin, cos, sin, cos, sin)


def _merge_kernel(x_ref, mod_ref, cv_ref, before_ref, after_ref, gate_lo_ref, gate_hi_ref, att_ref, yf_ref, yb_ref,
                  bonus_ref, g_ref,
                  cw_ref, lng_ref, lnb_ref, wa_ref, wb_ref, wc_ref, wo_ref, o_ref, *, tm, ctx, total):
    i = pl.program_id(0)
    bsz = x_ref.shape[0]
    d = D_MODEL
    cw = cw_ref[...]
    lng = lng_ref[...]
    lnb = lnb_ref[...]
    lane_lo = lax.broadcasted_iota(jnp.int32, (1, LANES), 1) < HEAD_DIM
    convs, rwkvs = [], []
    for b in range(bsz):
        cv = cv_ref[b].astype(F32)
        z = cv[:, d:2 * d] * cv[:, 2 * d:]
        before, after = _edge_rows(before_ref.at[b], after_ref.at[b])
        prev, nxt = _neighbours(z, before[:, d:2 * d] * before[:, 2 * d:], after[:, d:2 * d] * after[:, 2 * d:],
                                i, tm, ctx, total)
        convs.append((cv[:, :d] * (prev * cw[0:1] + z * cw[1:2] + nxt * cw[2:3])).astype(BF16))
        slabs = []
        for p in range(N_PAIR):
            sl = slice(LANES * p, LANES * (p + 1))
            y = yf_ref[b, p].astype(F32) + yb_ref[b, p].astype(F32)
            mean = _head_sums_exact(y, lane_lo) * (1.0 / HEAD_DIM)
            yc = y - mean
            var = _head_sums_exact(yc * yc, lane_lo) * (1.0 / HEAD_DIM)
            gn = yc * lax.rsqrt(var + GN_EPS) * lng[:, sl] + lnb[:, sl]
            slabs.append(((gn + bonus_ref[b, p]) * g_ref[b, p]).astype(BF16))
        rwkvs.append(jnp.concatenate(slabs, axis=1))

    rows = bsz * tm
    gates = jnp.concatenate([gate_lo_ref[...].reshape(rows, GATE_BLK), gate_hi_ref[...].reshape(rows, GATE_BLK)],
                            axis=1).astype(F32)
    m = (_sigmoid(gates[:, :d]) * _dot(jnp.concatenate(convs, axis=0), wa_ref[...])
         + _sigmoid(gates[:, d:2 * d]) * _dot(jnp.concatenate(rwkvs, axis=0), wb_ref[...])
         + _sigmoid(gates[:, 2 * d:]) * _dot(att_ref[...].reshape(rows, d), wc_ref[...]))
    out = _dot(m, wo_ref[...])
    is_ctx = _row_ids(i, tm) < ctx
    for b in range(bsz):
        o_ref[b] = x_ref[b] + _mod_row(mod_ref.at[b], 2, is_ctx) * out[b * tm:(b + 1) * tm]


def _merge(x, mod, proj_a, proj_b, att, y, bonus, g, conv_w, ln_g, ln_b, wa, wb, wc, wo, layer, *, ctx, tm):
    bsz, total, d = x.shape
    kern = functools.partial(_merge_kernel, tm=tm, ctx=ctx, total=total)
    per, nblk = tm // HALO_BF16, total // HALO_BF16
    col_cv, col_gate = A_CONV // (3 * d), B_GATE // GATE_BLK
    before = pl.BlockSpec((bsz, HALO_BF16, 3 * d), lambda i: (0, jnp.maximum(i * per - 1, 0), col_cv))
    after = pl.BlockSpec((bsz, HALO_BF16, 3 * d), lambda i: (0, jnp.minimum((i + 1) * per, nblk - 1), col_cv))
    row = lambda w, col=0: pl.BlockSpec((bsz, tm, w), lambda i: (0, i, col))
    once = lambda shp: pl.BlockSpec(shp, lambda i: tuple(0 for _ in shp), pipeline_mode=pl.Buffered(1))
    pm = pl.BlockSpec((bsz, N_PAIR, tm, LANES), lambda i: (0, 0, i, 0))
    return pl.pallas_call(
        kern,
        grid=(total // tm,),
        in_specs=[row(d), pl.BlockSpec((bsz, 2, 6, d), lambda i: (0, 0, 0, 0)),
                  row(3 * d, col_cv), before, after, row(GATE_BLK, col_gate), row(GATE_BLK, col_gate + 1),
                  row(d), pm, pm, pm, pm,
                  once((3, d)), once((1, d)), once((1, d)),
                  _layer_weight((d, d), layer), _layer_weight((d, d), layer), _layer_weight((d, d), layer),
                  _layer_weight((d, d), layer)],
        out_specs=row(d),
        out_shape=jax.ShapeDtypeStruct((bsz, total, d), F32),
        compiler_params=_cparams(("arbitrary",)),
    )(x, mod, proj_a, proj_a, proj_a, proj_b, proj_b, att, y[0], y[1], bonus, g, conv_w, ln_g, ln_b,
      wa, wb, wc, wo)


FF_CHUNKS = 2
FF_CHUNK = D_FF // FF_CHUNKS
assert FF_CHUNK * FF_CHUNKS == D_FF and FF_CHUNK % LANES == 0


def _ffn_kernel(x_ref, before_ref, after_ref, mod_ref, g_ref, wu_ref, cw_ref, wd_ref, fin_ref, o_ref, *,
                tm, ctx, total):
    i = pl.program_id(0)
    bsz = x_ref.shape[0]
    start = i * tm
    ext = tm + 2 * HALO
    is_ctx = (start - HALO + lax.broadcasted_iota(jnp.int32, (ext, 1), 0)) < ctx
    hs = []
    for b in range(bsz):
        xe = jnp.concatenate([before_ref[b], x_ref[b], after_ref[b]], axis=0)
        y = xe * lax.rsqrt(jnp.mean(xe * xe, axis=-1, keepdims=True) + NORM_EPS) * g_ref[...]
        mod_b = mod_ref.at[b]
        hs.append((y * (1.0 + _mod_row(mod_b, 4, is_ctx)) + _mod_row(mod_b, 3, is_ctx)).astype(BF16))
    h = jnp.concatenate(hs, axis=0)

    keep_before = jnp.where((start == 0) | (start == ctx), 0.0, 1.0)
    keep_after = jnp.where((start + tm == ctx) | (start + tm == total), 0.0, 1.0)
    row = lax.broadcasted_iota(jnp.int32, (HALO, 1), 0)
    first_row = row == 0
    last_row = row == HALO - 1

    def conv(u, cw):
        prev_all = pltpu.roll(u, 1, 0)
        nxt_all = pltpu.roll(u, bsz * ext - 1, 0)
        out = []
        for b in range(bsz):
            rows = slice(b * ext + HALO, b * ext + HALO + tm)
            prev, nxt = prev_all[rows], nxt_all[rows]
            top = jnp.where(first_row, prev[:HALO] * keep_before, prev[:HALO])
            bottom = jnp.where(last_row, nxt[tm - HALO:] * keep_after, nxt[tm - HALO:])
            prev = jnp.concatenate([top, prev[HALO:]], axis=0)
            nxt = jnp.concatenate([nxt[:tm - HALO], bottom], axis=0)
            out.append(prev * cw[0:1] + u[rows] * cw[1:2] + nxt * cw[2:3])
        return jnp.concatenate(out, axis=0)

    def cols(j):
        return slice(FF_CHUNK * j, FF_CHUNK * (j + 1)), slice(D_FF + FF_CHUNK * j, D_FF + FF_CHUNK * (j + 1))

    def up(j):
        gate_cols, val_cols = cols(j)
        return _dot(h, wu_ref[:, gate_cols]), _dot(h, wu_ref[:, val_cols])

    acts = []
    u_next = up(0)
    for j in range(FF_CHUNKS):
        gate_cols, val_cols = cols(j)
        u_gate, u_val = u_next
        if j + 1 < FF_CHUNKS:
            u_next = up(j + 1)
        ug = conv(u_gate, cw_ref[:, gate_cols])
        uv = conv(u_val, cw_ref[:, val_cols])
        acts.append((ug * _sigmoid(ug) * uv).astype(BF16))
    acc = _dot(jnp.concatenate(acts, axis=1), wd_ref[...])
    tile_is_ctx = _row_ids(i, tm) < ctx
    for b in range(bsz):
        out = x_ref[b] + _mod_row(mod_ref.at[b], 5, tile_is_ctx) * acc[b * tm:(b + 1) * tm]
        if fin_ref is not None:
            out = out * lax.rsqrt(jnp.mean(out * out, axis=-1, keepdims=True) + NORM_EPS) * fin_ref[...]
        o_ref[b] = out


def _ffn_mid_kernel(x_ref, before_ref, after_ref, mod_ref, g_ref, wu_ref, cw_ref, wd_ref, o_ref, **kw):
    _ffn_kernel(x_ref, before_ref, after_ref, mod_ref, g_ref, wu_ref, cw_ref, wd_ref, None, o_ref, **kw)


def _ffn(x, mod, g, wu, conv_w, wd, layer, final_g, *, ctx, tm):
    bsz, total, d = x.shape
    per, nblk = tm // HALO, total // HALO
    before = pl.BlockSpec((bsz, HALO, d), lambda i: (0, jnp.maximum(i * per - 1, 0), 0))
    after = pl.BlockSpec((bsz, HALO, d), lambda i: (0, jnp.minimum((i + 1) * per, nblk - 1), 0))
    row = pl.BlockSpec((bsz, tm, d), lambda i: (0, i, 0))
    once = lambda shp: pl.BlockSpec(shp, lambda i: tuple(0 for _ in shp), pipeline_mode=pl.Buffered(1))
    in_specs = [row, before, after, pl.BlockSpec((bsz, 2, 6, d), lambda i: (0, 0, 0, 0)),
                once((1, d)), _layer_weight((d, 2 * D_FF), layer), once((3, 2 * D_FF)),
                _layer_weight((D_FF, d), layer)]
    args = [x, x, x, mod, g, wu, conv_w, wd]
    kw = dict(tm=tm, ctx=ctx, total=total)
    if final_g is None:
        kern, out_spec, out_rows = functools.partial(_ffn_mid_kernel, **kw), row, total
    else:
        off = ctx // tm
        kern = functools.partial(_ffn_kernel, **kw)
        out_spec = pl.BlockSpec((bsz, tm, d), lambda i: (0, jnp.maximum(i - off, 0), 0))
        out_rows = total - ctx
        in_specs.append(once((1, d)))
        args.append(final_g)
    return pl.pallas_call(
        kern,
        grid=(total // tm,),
        in_specs=in_specs,
        out_specs=out_spec,
        out_shape=jax.ShapeDtypeStruct((bsz, out_rows, d), F32),
        compiler_params=_cparams(("arbitrary",)),
    )(*args)


def _rope_tables(ctx, seq):
    rows = seq // GRID_W
    t_row = jnp.broadcast_to(jnp.arange(rows)[:, None], (rows, GRID_W)).reshape(-1).astype(F32)
    t_col = jnp.broadcast_to(jnp.arange(GRID_W)[None, :], (rows, GRID_W)).reshape(-1).astype(F32)
    n_freq = HEAD_DIM // 4
    inv = ROPE_THETA ** (-jnp.arange(n_freq, dtype=F32) / n_freq)
    ar, ac = t_row[:, None] * inv, t_col[:, None] * inv
    cos = jnp.concatenate([jnp.cos(ar), jnp.cos(ar), jnp.cos(ac), jnp.cos(ac)], axis=1)
    sin = jnp.concatenate([-jnp.sin(ar), jnp.sin(ar), -jnp.sin(ac), jnp.sin(ac)], axis=1)
    cos = jnp.concatenate([jnp.ones((ctx, HEAD_DIM), F32), cos], axis=0)
    sin = jnp.concatenate([jnp.zeros((ctx, HEAD_DIM), F32), sin], axis=0)
    return jnp.tile(cos, (1, LANES // HEAD_DIM)), jnp.tile(sin, (1, LANES // HEAD_DIM))


def _block_diag2(w):
    z = jnp.zeros_like(w[0])
    return jnp.concatenate([jnp.concatenate([w[0], z], axis=1), jnp.concatenate([z, w[1]], axis=1)], axis=0)


def kernel(x, c, ctx, c_ctx, ada_w, ada_b, norm1_g, w_in, conv_a_w, a_out_w, rwkv_mu, rwkv_w0, rwkv_w2,
           rwkv_a0, rwkv_a2, rwkv_g2, rwkv_k_k, rwkv_k_a, rwkv_r_k, rwkv_ln_g, rwkv_ln_b, rwkv_out_w,
           attn_sink, attn_out_w, w_o, norm2_g, ffn_up, ffn_conv, ffn_down, final_norm_g):
    bsz, seq, d = x.shape
    n_ctx = ctx.shape[1]
    depth = ada_w.shape[0]
    total = n_ctx + seq
    assert d == D_MODEL and seq % ATT_BLOCK == 0 and n_ctx % ATT_BLOCK == 0 and bsz + 1 <= HALO
    tm = 256
    assert total % tm == 0 and n_ctx % tm == 0

    cvec = jnp.zeros((HALO, d), F32).at[:bsz].set(c).at[bsz].set(c_ctx)
    mods = _ada(cvec, ada_w.astype(BF16), ada_b)
    mods = mods.reshape(depth, HALO, 6, d)
    mods = jnp.stack([jnp.broadcast_to(mods[:, bsz:bsz + 1], (depth, bsz, 6, d)), mods[:, :bsz]], axis=2)

    cos, sin = _rope_tables(n_ctx, seq)
    masks = _scan_masks(tm)
    head_ones = masks[2].astype(BF16)
    xa = jnp.concatenate([ctx, x], axis=1)
    w_in, a_out_w, rwkv_out_w, attn_out_w, w_o, ffn_up, ffn_down = (
        w.astype(BF16) for w in (w_in, a_out_w, rwkv_out_w, attn_out_w, w_o, ffn_up, ffn_down))

    for l in range(depth):
        mod = mods[l]
        proj_a, proj_b = _proj(xa, mod, norm1_g[l][None], w_in, l, ctx=n_ctx, tm=tm // 2)

        at, rt, bt, kt, gend, v, bonus, g = _prep(
            proj_a, masks[0], head_ones, rwkv_mu[l], rwkv_w0[l].reshape(1, 2 * d), _block_diag2(rwkv_w2[l]).astype(BF16),
            rwkv_a0[l].reshape(1, 2 * d), _block_diag2(rwkv_a2[l]).astype(BF16), rwkv_g2[l].astype(BF16),
            rwkv_k_k[l][None], rwkv_k_a[l][None], rwkv_r_k[l].reshape(1, d), ctx=n_ctx, tm=tm)
        y = [_scan(at, rt, bt, kt, gend, v, masks, ctx=n_ctx, direction=dr) for dr in range(2)]
        att = _attention(proj_b, attn_sink[l], cos, sin, ctx=n_ctx)
        xa = _merge(xa, mod, proj_a, proj_b, att, y, bonus, g, conv_a_w[l], rwkv_ln_g[l][None], rwkv_ln_b[l][None],
                    a_out_w, rwkv_out_w, attn_out_w, w_o, l, ctx=n_ctx, tm=tm)
        xa = _ffn(xa, mod, norm2_g[l][None], ffn_up, ffn_conv[l], ffn_down, l,
                  final_norm_g[None] if l == depth - 1 else None, ctx=n_ctx, tm=tm)
    return xa
```

```python
import functools

import jax
import jax.numpy as jnp
import numpy as np
from jax import lax
from jax.experimental import pallas as pl
from jax.experimental.pallas import tpu as pltpu

D_MODEL = 1024
GRID_W = 64
N_HEADS = 16
HEAD_DIM = 64
N_KV = 4
R_LORA = 64
R_GATE = 128
GN_EPS = 64e-5
WINDOW = 128
ATT_BLOCK = 128
LOG2_ATT_BLOCK = 7
ATT_STACK = N_HEADS // N_KV
ATT_SCALE = HEAD_DIM ** -0.5
ROPE_THETA = 10000.0
NEG_INF = -1e30
D_FF = 2816
NORM_EPS = 1e-6
KK_EPS = 1e-12
DECAY_SCALE = float(np.exp(-0.5))

LANES = 128
HALO = 8
HALO_BF16 = 16
N_PAIR = D_MODEL // LANES
CHUNK = 64
N_LEVELS = 5
VMEM_LIMIT = 52 * 1024 * 1024

BF16 = jnp.bfloat16
F32 = jnp.float32

KV_W = N_KV * HEAD_DIM
LORA_W = 2 * R_LORA + 2 * R_LORA + R_GATE
N_PROJ_A = 6 * D_MODEL + LORA_W
A_CONV, A_RKV, A_LORA = 0, 3 * D_MODEL, 6 * D_MODEL
N_PROJ_B = D_MODEL + 2 * KV_W + 3 * D_MODEL
B_Q, B_K, B_V, B_GATE = 0, D_MODEL, D_MODEL + KV_W, D_MODEL + 2 * KV_W
GATE_BLK = B_GATE
assert A_LORA % LORA_W == 0 and B_K % KV_W == 0 and 2 * GATE_BLK == 3 * D_MODEL and GATE_BLK % LANES == 0


def _cparams(sem):
    return pltpu.CompilerParams(dimension_semantics=sem, vmem_limit_bytes=VMEM_LIMIT)


def _dot(a, b):
    return jnp.dot(a.astype(BF16), b.astype(BF16), preferred_element_type=F32)


def _dot_nt(a, b):
    return lax.dot_general(a.astype(BF16), b.astype(BF16), (((1,), (1,)), ((), ())),
                           preferred_element_type=F32)


def _bmm(a, b):
    return jnp.einsum('pmk,pkn->pmn', a.astype(BF16), b.astype(BF16), preferred_element_type=F32)


def _bmm_nt(a, b):
    return jnp.einsum('pmk,pnk->pmn', a.astype(BF16), b.astype(BF16), preferred_element_type=F32)


def _bmm_tn(a, b):
    return jnp.einsum('pkm,pkn->pmn', a.astype(BF16), b.astype(BF16), preferred_element_type=F32)


def _sigmoid(x):
    return 0.5 + 0.5 * jnp.tanh(0.5 * x)


def _row_ids(i, tm):
    return i * tm + lax.broadcasted_iota(jnp.int32, (tm, 1), 0)


def _mod_row(mod_ref, k, is_ctx):
    return jnp.where(is_ctx, mod_ref[0, k:k + 1, :], mod_ref[1, k:k + 1, :])


def _neighbours(z, before, after, i, tm, ctx, total):
    start = i * tm
    keep_before = jnp.where((start == 0) | (start == ctx), 0.0, 1.0)
    keep_after = jnp.where((start + tm == ctx) | (start + tm == total), 0.0, 1.0)
    row = lax.broadcasted_iota(jnp.int32, (HALO, 1), 0)
    prev = pltpu.roll(z, 1, 0)
    nxt = pltpu.roll(z, tm - 1, 0)
    top = jnp.where(row == 0, before * keep_before, prev[:HALO])
    bottom = jnp.where(row == HALO - 1, after * keep_after, nxt[tm - HALO:])
    return (jnp.concatenate([top, prev[HALO:]], axis=0),
            jnp.concatenate([nxt[:tm - HALO], bottom], axis=0))


def _layer_weight(shape, layer):
    return pl.BlockSpec((None,) + shape, lambda *_: (layer,) + tuple(0 for _ in shape),
                        pipeline_mode=pl.Buffered(1))


def _halo_specs(width, tm, total, col=0, rows=HALO):
    per = tm // rows
    nblk = total // rows
    before = pl.BlockSpec((None, rows, width), lambda b, i: (b, jnp.maximum(i * per - 1, 0), col))
    after = pl.BlockSpec((None, rows, width), lambda b, i: (b, jnp.minimum((i + 1) * per, nblk - 1), col))
    return before, after


def _edge_rows(before_ref, after_ref):
    before = before_ref[...].astype(F32)
    return before[before.shape[0] - 1:], after_ref[...].astype(F32)[0:1]


def _ada_kernel(c_ref, w_ref, b_ref, o_ref):
    c = c_ref[...]
    o_ref[...] = _dot(c * _sigmoid(c), w_ref[...]) + b_ref[...]


def _ada(cvec, ada_w, ada_b):
    depth, d, n = ada_w.shape
    tn = 1536
    return pl.pallas_call(
        _ada_kernel,
        grid=(depth, n // tn),
        in_specs=[pl.BlockSpec((HALO, d), lambda l, j: (0, 0)),
                  pl.BlockSpec((None, d, tn), lambda l, j: (l, 0, j)),
                  pl.BlockSpec((None, 1, tn), lambda l, j: (l, 0, j))],
        out_specs=pl.BlockSpec((None, HALO, tn), lambda l, j: (l, 0, j)),
        out_shape=jax.ShapeDtypeStruct((depth, HALO, n), F32),
        compiler_params=_cparams(("arbitrary", "arbitrary")),
    )(cvec, ada_w, ada_b.reshape(depth, 1, n))


PROJ_SPLITS = ((0, 2176), (2176, 4352), (4352, N_PROJ_A))


def _rope(z, tab):
    cos, sin = tab[:, :LANES], tab[:, LANES:]
    lane = lax.broadcasted_iota(jnp.int32, (1, LANES), 1)
    first = jnp.bitwise_and(lane, HEAD_DIM // 2 - 1) < (HEAD_DIM // 4)
    out = []
    for p in range(z.shape[1] // LANES):
        zp = z[:, LANES * p:LANES * (p + 1)]
        partner = jnp.where(first, pltpu.roll(zp, LANES - HEAD_DIM // 4, 1), pltpu.roll(zp, HEAD_DIM // 4, 1))
        out.append(zp * cos + partner * sin)
    return jnp.concatenate(out, axis=1)


def _proj_kernel(x_ref, mod_ref, g_ref, w_ref, tab_ref, oa_ref, ob_ref, *, tm, ctx):
    i = pl.program_id(0)
    bsz = x_ref.shape[0]
    is_ctx = _row_ids(i, tm) < ctx
    hs = []
    for b in range(bsz):
        x = x_ref[b]
        y = x * lax.rsqrt(jnp.mean(x * x, axis=-1, keepdims=True) + NORM_EPS) * g_ref[...]
        mod_b = mod_ref.at[b]
        hs.append((y * (1.0 + _mod_row(mod_b, 1, is_ctx)) + _mod_row(mod_b, 0, is_ctx)).astype(BF16))
    h = jnp.concatenate(hs, axis=0)
    qkv = _dot(h, w_ref[:, N_PROJ_A + B_Q:N_PROJ_A + B_GATE])
    tab = tab_ref[...]
    for b in range(bsz):
        rows = slice(b * tm, (b + 1) * tm)
        rot = _rope(qkv[rows, :B_V - B_Q], tab)
        ob_ref[b, :, B_Q:B_K] = (rot[:, :B_K - B_Q] * ATT_SCALE).astype(ob_ref.dtype)
        ob_ref[b, :, B_K:B_V] = rot[:, B_K - B_Q:].astype(ob_ref.dtype)
        ob_ref[b, :, B_V:B_GATE] = qkv[rows, B_V - B_Q:].astype(ob_ref.dtype)
    gates = _sigmoid(_dot(h, w_ref[:, N_PROJ_A + B_GATE:]))
    ob_ref[:, :, B_GATE:] = gates.reshape(bsz, tm, N_PROJ_B - B_GATE).astype(ob_ref.dtype)
    for lo, hi in PROJ_SPLITS:
        out = _dot(h, w_ref[:, lo:hi])
        oa_ref[:, :, lo:hi] = out.reshape(bsz, tm, hi - lo).astype(oa_ref.dtype)


def _proj(x, mod, g, w, layer, rope_tab, *, ctx, tm):
    bsz, total, d = x.shape
    assert w.shape[2] == N_PROJ_A + N_PROJ_B
    kern = functools.partial(_proj_kernel, tm=tm, ctx=ctx)
    once = lambda shp: pl.BlockSpec(shp, lambda i: tuple(0 for _ in shp), pipeline_mode=pl.Buffered(1))
    out = lambda n: pl.BlockSpec((bsz, tm, n), lambda i: (0, i, 0))
    return pl.pallas_call(
        kern,
        grid=(total // tm,),
        in_specs=[pl.BlockSpec((bsz, tm, d), lambda i: (0, i, 0)),
                  pl.BlockSpec((bsz, 2, 6, d), lambda i: (0, 0, 0, 0)),
                  once((1, d)), _layer_weight((d, N_PROJ_A + N_PROJ_B), layer),
                  pl.BlockSpec((tm, 2 * LANES), lambda i: (i, 0))],
        out_specs=[out(N_PROJ_A), out(N_PROJ_B)],
        out_shape=[jax.ShapeDtypeStruct((bsz, total, N_PROJ_A), BF16),
                   jax.ShapeDtypeStruct((bsz, total, N_PROJ_B), BF16)],
        compiler_params=_cparams(("arbitrary",)),
    )(x, mod, g, w, rope_tab)


def _head_sums(z, head_ones):
    return jnp.dot(z.astype(BF16), head_ones, preferred_element_type=F32)


def _head_sums_exact(z, lane_lo):
    s_lo = jnp.sum(jnp.where(lane_lo, z, 0.0), axis=-1, keepdims=True)
    s_hi = jnp.sum(jnp.where(lane_lo, 0.0, z), axis=-1, keepdims=True)
    return jnp.where(lane_lo, s_lo, s_hi)


def _chunk_cumsum(tri, z):
    hi = z.astype(BF16)
    low = (z - hi.astype(F32)).astype(BF16)
    dot = functools.partial(jnp.dot, preferred_element_type=F32)
    return dot(tri, low) + dot(tri, hi)


def _prep_kernel(rkv_ref, before_ref, after_ref, lora_ref, tri_ref, ones_ref, mu_ref, w0_ref, w2_ref, a0_ref,
                 a2_ref, g2_ref, kk_ref, ka_ref, rk_ref,
                 at_out, rt_out, bt_out, kt_out, gend_out, v_out, bonus_out, g_out, *, tm, ctx, total):
    i = pl.program_id(1)
    x = rkv_ref[...].astype(F32)
    prev, nxt = _neighbours(x, *_edge_rows(before_ref, after_ref), i, tm, ctx, total)
    mu_prev = mu_ref[0:1, :]
    mu_next = mu_ref[1:2, :]
    s = prev * mu_prev + x * (1.0 - mu_prev - mu_next) + nxt * mu_next
    d = D_MODEL
    r, k, v = s[:, :d], s[:, d:2 * d], s[:, 2 * d:]

    lo = lora_ref[...].astype(F32)
    w_pre = w0_ref[...] + _dot(jnp.tanh(lo[:, :LANES]), w2_ref[...])
    lw = -DECAY_SCALE * _sigmoid(w_pre)
    a = _sigmoid(a0_ref[...] + _dot(lo[:, LANES:2 * LANES], a2_ref[...]))
    g = _dot(_sigmoid(lo[:, 2 * LANES:]), g2_ref[...])
    kk = k * kk_ref[...]
    ka = ka_ref[...]
    rk = rk_ref[...]
    cl = [_chunk_cumsum(tri_ref[dr], lw[:, d * dr:d * (dr + 1)]) for dr in range(2)]

    head_ones = ones_ref[...]
    for p in range(N_PAIR):
        sl = slice(LANES * p, LANES * (p + 1))
        kkp = kk[:, sl]
        kn = kkp * lax.rsqrt(_head_sums(kkp * kkp, head_ones) + KK_EPS)
        rp, kp, vp = r[:, sl], k[:, sl], v[:, sl]
        bonus = jnp.zeros_like(rp)
        for dr in range(2):
            dsl = slice(d * dr + LANES * p, d * dr + LANES * (p + 1))
            ad = a[:, dsl]
            kd = kp * (1.0 + (ad - 1.0) * ka[:, sl])
            bonus = bonus + _head_sums(rp * kd * rk[:, sl], head_ones) * vp
            cl_incl = cl[dr][:, sl]
            inv_g = jnp.exp(-cl_incl)
            at_out[dr, p] = (-kn * jnp.exp(cl_incl - lw[:, dsl])).astype(BF16)
            rt_out[dr, p] = (rp * jnp.exp(cl_incl)).astype(BF16)
            bt_out[dr, p] = (kn * ad * inv_g).astype(BF16)
            kt_out[dr, p] = (kd * inv_g).astype(BF16)
            for q in range(tm // CHUNK):
                end = CHUNK * q + (CHUNK - 1 if dr == 0 else 0)
                gend_out[dr, q, p:p + 1, :] = jnp.exp(cl_incl[end:end + 1, :])
        v_out[p] = vp.astype(BF16)
        bonus_out[p] = bonus.astype(BF16)
        g_out[p] = g[:, sl].astype(BF16)


def _prep(proj, tri, head_ones, mu, w0, w2bd, a0, a2bd, g2, k_k, k_a, r_k, *, ctx, tm):
    bsz, total, _ = proj.shape
    d = D_MODEL
    kern = functools.partial(_prep_kernel, tm=tm, ctx=ctx, total=total)
    before, after = _halo_specs(3 * d, tm, total, A_RKV // (3 * d), HALO_BF16)
    full = lambda shp: pl.BlockSpec(shp, lambda b, i: tuple(0 for _ in shp))
    pm = pl.BlockSpec((None, N_PAIR, tm, LANES), lambda b, i: (b, 0, i, 0))
    pm2 = pl.BlockSpec((2, None, N_PAIR, tm, LANES), lambda b, i: (0, b, 0, i, 0))
    ge = pl.BlockSpec((2, None, tm // CHUNK, N_PAIR, LANES), lambda b, i: (0, b, i, 0, 0))
    sds = jax.ShapeDtypeStruct((bsz, N_PAIR, total, LANES), BF16)
    sds2 = jax.ShapeDtypeStruct((2, bsz, N_PAIR, total, LANES), BF16)
    sds_ge = jax.ShapeDtypeStruct((2, bsz, total // CHUNK, N_PAIR, LANES), F32)
    return pl.pallas_call(
        kern,
        grid=(bsz, total // tm),
        in_specs=[pl.BlockSpec((None, tm, 3 * d), lambda b, i: (b, i, A_RKV // (3 * d))), before, after,
                  pl.BlockSpec((None, tm, LORA_W), lambda b, i: (b, i, A_LORA // LORA_W)),
                  full((2, tm, tm)), full((LANES, LANES)), full((2, 3 * d)), full((1, 2 * d)), full((LANES, 2 * d)),
                  full((1, 2 * d)),
                  full((LANES, 2 * d)), full((R_GATE, d)), full((1, d)), full((1, d)), full((1, d))],
        out_specs=[pm2, pm2, pm2, pm2, ge, pm, pm, pm],
        out_shape=[sds2, sds2, sds2, sds2, sds_ge, sds, sds, sds],
        compiler_params=_cparams(("arbitrary", "arbitrary")),
    )(proj, proj, proj, proj, tri, head_ones, mu, w0, w2bd, a0, a2bd, g2, k_k, k_a, r_k)


M_STRICT, M_INCL, M_EYE, M_LEVEL0 = 0, 1, 2, 3
N_MASKS = M_LEVEL0 + 1 + N_LEVELS


def _scan_masks(tm):
    c = CHUNK
    t = np.arange(c)[:, None]
    s = np.arange(c)[None, :]
    cum = np.zeros((2, tm, tm), np.float32)
    msk = np.zeros((2, N_MASKS, c, 2 * c), np.float32)
    for dr in range(2):
        earlier = (s < t) if dr == 0 else (s > t)
        cum[dr] = np.kron(np.eye(tm // c), earlier | (s == t))
        planes = [earlier, earlier | (s == t), s == t]
        m = 1
        while m < c:
            same = (t // (2 * m)) == (s // (2 * m))
            t_late = (t % (2 * m)) >= m
            s_late = (s % (2 * m)) >= m
            planes.append(same & (t_late & ~s_late if dr == 0 else ~t_late & s_late))
            m *= 2
        for q, plane in enumerate(planes):
            msk[dr, q] = np.concatenate([plane, plane], axis=1)
    bdm = np.kron(np.eye(2, dtype=np.float32), np.ones((c, c), np.float32))
    return jnp.asarray(cum, BF16), jnp.asarray(msk), jnp.asarray(bdm)


SCAN_GROUP = 4


def _scan_kernel(msk_ref, bdm_ref, at_ref, rt_ref, bt_ref, kt_ref, gend_ref, v_ref, y_ref, s_ref, *, reverse):
    c = CHUNK
    bsz = v_ref.shape[0]
    n = bsz * N_PAIR
    order = tuple(reversed(range(SCAN_GROUP))) if reverse else tuple(range(SCAN_GROUP))

    @pl.when(pl.program_id(0) == 0)
    def _():
        s_ref[...] = jnp.zeros_like(s_ref)

    bdm = bdm_ref[...]
    bdm_bf = bdm.astype(BF16)
    m_strict = msk_ref[M_STRICT]
    m_incl = msk_ref[M_INCL]

    def load(ref):
        z = ref[...].reshape(n, SCAN_GROUP * c, LANES)
        return jnp.concatenate([z[:, c * k:c * (k + 1)] for k in range(SCAN_GROUP)], axis=0)

    def stack(z):
        z = z.astype(BF16)
        return jnp.concatenate([z, z], axis=1) * bdm_bf

    bt, kt, v = load(bt_ref), load(kt_ref), load(v_ref)
    ar = jnp.concatenate([load(at_ref), load(rt_ref)], axis=1)
    gram = _bmm_nt(ar, jnp.concatenate([stack(bt), stack(kt)], axis=1))
    g_ab = gram[:, :c, :2 * c]
    from_v = jnp.concatenate([gram[:, :c, 2 * c:] * m_strict, gram[:, c:, 2 * c:] * m_incl], axis=1)
    from_v = _bmm(from_v, stack(v))
    y_from_u = gram[:, c:, :2 * c] * m_incl

    tri = msk_ref[M_EYE] + g_ab * msk_ref[M_LEVEL0]
    for lvl in range(1, N_LEVELS + 1):
        w = _bmm(g_ab * msk_ref[M_LEVEL0 + lvl], stack(tri))
        tri = tri + _bmm(tri, stack(w))

    state = s_ref[...]
    ys = [None] * SCAN_GROUP
    for k in order:
        sel = slice(n * k, n * (k + 1))
        g_end = jnp.stack([gend_ref[q // N_PAIR, k, q % N_PAIR:q % N_PAIR + 1, :] for q in range(n)], axis=0)
        partial = _bmm_nt(ar[sel], state) + from_v[sel]
        u = _bmm(tri[sel], stack(partial[:, :c]))
        ys[k] = partial[:, c:] + _bmm(y_from_u[sel], stack(u))
        upd = _bmm_tn(jnp.concatenate([u.astype(BF16), v[sel]], axis=1), jnp.concatenate([bt[sel], kt[sel]], axis=1))
        state = (state + upd * bdm) * g_end
    s_ref[...] = state
    y_ref[...] = jnp.concatenate(ys, axis=1).reshape(bsz, N_PAIR, SCAN_GROUP * c, LANES).astype(y_ref.dtype)


def _scan(at, rt, bt, kt, gend, v, masks, *, ctx, direction):
    bsz, _, total, _ = v.shape
    rows = SCAN_GROUP * CHUNK
    steps, ctx_steps = total // rows, ctx // rows
    assert steps * rows == total and ctx_steps * rows == ctx
    _, msk, bdm = masks

    def blk(j):
        return j if direction == 0 else jnp.where(j < ctx_steps, ctx_steps - 1 - j, steps + ctx_steps - 1 - j)

    shared = pl.BlockSpec((bsz, N_PAIR, rows, LANES), lambda j: (0, 0, blk(j), 0))
    per_dir = pl.BlockSpec((None, bsz, N_PAIR, rows, LANES), lambda j: (direction, 0, 0, blk(j), 0))
    ends = pl.BlockSpec((None, bsz, SCAN_GROUP, N_PAIR, LANES), lambda j: (direction, 0, blk(j), 0, 0))
    return pl.pallas_call(
        functools.partial(_scan_kernel, reverse=direction == 1),
        grid=(steps,),
        in_specs=[pl.BlockSpec((None, N_MASKS, CHUNK, 2 * CHUNK), lambda j: (direction, 0, 0, 0)),
                  pl.BlockSpec((2 * CHUNK, 2 * CHUNK), lambda j: (0, 0)),
                  per_dir, per_dir, per_dir, per_dir, ends, shared],
        out_specs=shared,
        out_shape=jax.ShapeDtypeStruct((bsz, N_PAIR, total, LANES), BF16),
        scratch_shapes=[pltpu.VMEM((bsz * N_PAIR, 2 * HEAD_DIM, 2 * HEAD_DIM), F32)],
        compiler_params=_cparams(("arbitrary",)),
    )(msk, bdm, at, rt, bt, kt, gend, v)


def _attn_kernel(sink_ref, q_ref, kvp_ref, kvc_ref, kvn_ref, kvx_ref, o_ref, *, n_ctx_blk, n_blk):
    i = pl.program_id(1)
    blk = ATT_BLOCK
    kv_refs = (kvp_ref, kvc_ref, kvn_ref, kvx_ref)
    q = q_ref[...].astype(F32)
    k_all = jnp.concatenate([ref[:, :KV_W] for ref in kv_refs], axis=0)
    v_all = jnp.concatenate([ref[:, KV_W:] for ref in kv_refs], axis=0)

    qi = lax.broadcasted_iota(jnp.int32, (blk, 3 * blk), 0)
    si = lax.broadcasted_iota(jnp.int32, (blk, 3 * blk), 1)
    rel = si - blk - qi
    key_blk = i - 1 + jnp.right_shift(si, LOG2_ATT_BLOCK)
    ok = (jnp.where(jnp.abs(rel) <= WINDOW, 1, 0) * jnp.where(key_blk >= n_ctx_blk, 1, 0)
          * jnp.where(key_blk < n_blk, 1, 0) * jnp.where(i >= n_ctx_blk, 1, 0))
    bias = jnp.where(ok > 0, 0.0, NEG_INF)

    lane_lo = lax.broadcasted_iota(jnp.int32, (1, LANES), 1) < HEAD_DIM
    stack = ATT_STACK
    outs = [None] * N_HEADS
    bias_g = jnp.concatenate([bias] * stack, axis=0)

    for h0 in range(0, N_HEADS, stack):
        kvh = h0 // (N_HEADS // N_KV)
        kv_slab = slice(LANES * (kvh // 2), LANES * (kvh // 2 + 1))
        kv_lo = kvh % 2 == 0
        rows = []
        for h in range(h0, h0 + stack):
            qp = q[:, LANES * (h // 2):LANES * (h // 2 + 1)]
            if (h % 2 == 0) != kv_lo:
                qp = pltpu.roll(qp, HEAD_DIM, 1)
            rows.append(jnp.where(lane_lo, qp, 0.0) if kv_lo else jnp.where(lane_lo, 0.0, qp))
        s = _dot_nt(jnp.concatenate(rows, axis=0), k_all[:, kv_slab])
        s = jnp.concatenate([s[:, :3 * blk] + bias_g, s[:, 3 * blk:]], axis=1)
        sink = jnp.concatenate([jnp.full((blk, 1), sink_ref[h], F32) for h in range(h0, h0 + stack)], axis=0)
        m = jnp.maximum(jnp.max(s, axis=-1, keepdims=True), sink)
        e = jnp.exp(s - m)
        denom = jnp.sum(e, axis=-1, keepdims=True) + jnp.exp(sink - m)
        o = _dot(e, v_all[:, kv_slab]) / denom
        for j, h in enumerate(range(h0, h0 + stack)):
            oh = o[j * blk:(j + 1) * blk]
            if (h % 2 == 0) != kv_lo:
                oh = pltpu.roll(oh, HEAD_DIM, 1)
            outs[h] = oh
    o_ref[...] = jnp.concatenate(
        [jnp.where(lane_lo, outs[2 * p], outs[2 * p + 1]) for p in range(N_PAIR)], axis=1).astype(o_ref.dtype)


def _attention(qkv, sink, *, ctx):
    bsz, total, _ = qkv.shape
    blk = ATT_BLOCK
    n_blk, n_ctx_blk = total // blk, ctx // blk
    assert B_V == B_K + KV_W and B_K % (2 * KV_W) == 0
    kv_col = B_K // (2 * KV_W)
    kern = functools.partial(_attn_kernel, n_ctx_blk=n_ctx_blk, n_blk=n_blk)
    prev_i = lambda i: jnp.maximum(i - 1, 0)
    next_i = lambda i: jnp.minimum(i + 1, n_blk - 1)
    same = lambda i: i

    def kv(row):
        return pl.BlockSpec((None, blk, 2 * KV_W), lambda b, i: (b, row(i), kv_col))

    return pl.pallas_call(
        kern,
        grid=(bsz, n_blk),
        in_specs=[pl.BlockSpec(memory_space=pltpu.SMEM),
                  pl.BlockSpec((None, blk, D_MODEL), lambda b, i: (b, i, B_Q // D_MODEL)),
                  kv(prev_i), kv(same), kv(next_i),
                  pl.BlockSpec((None, ctx, 2 * KV_W), lambda b, i: (b, 0, kv_col))],
        out_specs=pl.BlockSpec((None, blk, D_MODEL), lambda b, i: (b, i, 0)),
        out_shape=jax.ShapeDtypeStruct((bsz, total, D_MODEL), BF16),
        compiler_params=_cparams(("arbitrary", "arbitrary")),
    )(sink, qkv, qkv, qkv, qkv, qkv)


def _merge_kernel(x_ref, mod_ref, cv_ref, before_ref, after_ref, gate_lo_ref, gate_hi_ref, att_ref, yf_ref, yb_ref,
                  bonus_ref, g_ref,
                  cw_ref, lng_ref, lnb_ref, wa_ref, wb_ref, wc_ref, wo_ref, o_ref, *, tm, ctx, total):
    i = pl.program_id(0)
    bsz = x_ref.shape[0]
    d = D_MODEL
    cw = cw_ref[...]
    lng = lng_ref[...]
    lnb = lnb_ref[...]
    lane_lo = lax.broadcasted_iota(jnp.int32, (1, LANES), 1) < HEAD_DIM
    convs, rwkvs = [], []
    for b in range(bsz):
        cv = cv_ref[b].astype(F32)
        z = cv[:, d:2 * d] * cv[:, 2 * d:]
        before, after = _edge_rows(before_ref.at[b], after_ref.at[b])
        prev, nxt = _neighbours(z, before[:, d:2 * d] * before[:, 2 * d:], after[:, d:2 * d] * after[:, 2 * d:],
                                i, tm, ctx, total)
        convs.append((cv[:, :d] * (prev * cw[0:1] + z * cw[1:2] + nxt * cw[2:3])).astype(BF16))
        slabs = []
        for p in range(N_PAIR):
            sl = slice(LANES * p, LANES * (p + 1))
            y = yf_ref[b, p].astype(F32) + yb_ref[b, p].astype(F32)
            mean = _head_sums_exact(y, lane_lo) * (1.0 / HEAD_DIM)
            yc = y - mean
            var = _head_sums_exact(yc * yc, lane_lo) * (1.0 / HEAD_DIM)
            gn = yc * lax.rsqrt(var + GN_EPS) * lng[:, sl] + lnb[:, sl]
            slabs.append(((gn + bonus_ref[b, p]) * g_ref[b, p]).astype(BF16))
        rwkvs.append(jnp.concatenate(slabs, axis=1))

    rows = bsz * tm
    gates = jnp.concatenate([gate_lo_ref[...].reshape(rows, GATE_BLK), gate_hi_ref[...].reshape(rows, GATE_BLK)],
                            axis=1).astype(F32)
    m = (gates[:, :d] * _dot(jnp.concatenate(convs, axis=0), wa_ref[...])
         + gates[:, d:2 * d] * _dot(jnp.concatenate(rwkvs, axis=0), wb_ref[...])
         + gates[:, 2 * d:] * _dot(att_ref[...].reshape(rows, d), wc_ref[...]))
    out = _dot(m, wo_ref[...])
    is_ctx = _row_ids(i, tm) < ctx
    for b in range(bsz):
        o_ref[b] = x_ref[b] + _mod_row(mod_ref.at[b], 2, is_ctx) * out[b * tm:(b + 1) * tm]


def _merge(x, mod, proj_a, proj_b, att, y, bonus, g, conv_w, ln_g, ln_b, wa, wb, wc, wo, layer, *, ctx, tm):
    bsz, total, d = x.shape
    kern = functools.partial(_merge_kernel, tm=tm, ctx=ctx, total=total)
    per, nblk = tm // HALO_BF16, total // HALO_BF16
    col_cv, col_gate = A_CONV // (3 * d), B_GATE // GATE_BLK
    before = pl.BlockSpec((bsz, HALO_BF16, 3 * d), lambda i: (0, jnp.maximum(i * per - 1, 0), col_cv))
    after = pl.BlockSpec((bsz, HALO_BF16, 3 * d), lambda i: (0, jnp.minimum((i + 1) * per, nblk - 1), col_cv))
    row = lambda w, col=0: pl.BlockSpec((bsz, tm, w), lambda i: (0, i, col))
    once = lambda shp: pl.BlockSpec(shp, lambda i: tuple(0 for _ in shp), pipeline_mode=pl.Buffered(1))
    pm = pl.BlockSpec((bsz, N_PAIR, tm, LANES), lambda i: (0, 0, i, 0))
    return pl.pallas_call(
        kern,
        grid=(total // tm,),
        in_specs=[row(d), pl.BlockSpec((bsz, 2, 6, d), lambda i: (0, 0, 0, 0)),
                  row(3 * d, col_cv), before, after, row(GATE_BLK, col_gate), row(GATE_BLK, col_gate + 1),
                  row(d), pm, pm, pm, pm,
                  once((3, d)), once((1, d)), once((1, d)),
                  _layer_weight((d, d), layer), _layer_weight((d, d), layer), _layer_weight((d, d), layer),
                  _layer_weight((d, d), layer)],
        out_specs=row(d),
        out_shape=jax.ShapeDtypeStruct((bsz, total, d), F32),
        compiler_params=_cparams(("arbitrary",)),
    )(x, mod, proj_a, proj_a, proj_a, proj_b, proj_b, att, y[0], y[1], bonus, g, conv_w, ln_g, ln_b,
      wa, wb, wc, wo)


def _ffn_kernel(x_ref, before_ref, after_ref, mod_ref, g_ref, wu_ref, cw_ref, wd_ref, fin_ref, o_ref, *,
                tm, ctx, total):
    i = pl.program_id(0)
    bsz = x_ref.shape[0]
    start = i * tm
    ext = tm + 2 * HALO
    is_ctx = (start - HALO + lax.broadcasted_iota(jnp.int32, (ext, 1), 0)) < ctx
    tile_is_ctx = _row_ids(i, tm) < ctx

    keep_before = jnp.where((start == 0) | (start == ctx), 0.0, 1.0)
    keep_after = jnp.where((start + tm == ctx) | (start + tm == total), 0.0, 1.0)
    row = lax.broadcasted_iota(jnp.int32, (HALO, 1), 0)
    first_row = row == 0
    last_row = row == HALO - 1

    def up(b):
        xe = jnp.concatenate([before_ref[b], x_ref[b], after_ref[b]], axis=0)
        y = xe * lax.rsqrt(jnp.mean(xe * xe, axis=-1, keepdims=True) + NORM_EPS) * g_ref[...]
        mod_b = mod_ref.at[b]
        h = (y * (1.0 + _mod_row(mod_b, 4, is_ctx)) + _mod_row(mod_b, 3, is_ctx)).astype(BF16)
        return _dot(h, wu_ref[:, :D_FF]), _dot(h, wu_ref[:, D_FF:])

    def conv(u, cw):
        prev = pltpu.roll(u, 1, 0)[HALO:HALO + tm]
        nxt = pltpu.roll(u, ext - 1, 0)[HALO:HALO + tm]
        top = jnp.where(first_row, prev[:HALO] * keep_before, prev[:HALO])
        bottom = jnp.where(last_row, nxt[tm - HALO:] * keep_after, nxt[tm - HALO:])
        prev = jnp.concatenate([top, prev[HALO:]], axis=0)
        nxt = jnp.concatenate([nxt[:tm - HALO], bottom], axis=0)
        return prev * cw[0:1] + u[HALO:HALO + tm] * cw[1:2] + nxt * cw[2:3]

    u_next = up(0)
    for b in range(bsz):
        u_gate, u_val = u_next
        if b + 1 < bsz:
            u_next = up(b + 1)
        ug = conv(u_gate, cw_ref[:, :D_FF])
        uv = conv(u_val, cw_ref[:, D_FF:])
        out = x_ref[b] + _mod_row(mod_ref.at[b], 5, tile_is_ctx) * _dot(ug * _sigmoid(ug) * uv, wd_ref[...])
        if fin_ref is not None:
            out = out * lax.rsqrt(jnp.mean(out * out, axis=-1, keepdims=True) + NORM_EPS) * fin_ref[...]
        o_ref[b] = out


def _ffn_mid_kernel(x_ref, before_ref, after_ref, mod_ref, g_ref, wu_ref, cw_ref, wd_ref, o_ref, **kw):
    _ffn_kernel(x_ref, before_ref, after_ref, mod_ref, g_ref, wu_ref, cw_ref, wd_ref, None, o_ref, **kw)


def _ffn(x, mod, g, wu, conv_w, wd, layer, final_g, *, ctx, tm):
    bsz, total, d = x.shape
    per, nblk = tm // HALO, total // HALO
    before = pl.BlockSpec((bsz, HALO, d), lambda i: (0, jnp.maximum(i * per - 1, 0), 0))
    after = pl.BlockSpec((bsz, HALO, d), lambda i: (0, jnp.minimum((i + 1) * per, nblk - 1), 0))
    row = pl.BlockSpec((bsz, tm, d), lambda i: (0, i, 0))
    once = lambda shp: pl.BlockSpec(shp, lambda i: tuple(0 for _ in shp), pipeline_mode=pl.Buffered(1))
    in_specs = [row, before, after, pl.BlockSpec((bsz, 2, 6, d), lambda i: (0, 0, 0, 0)),
                once((1, d)), _layer_weight((d, 2 * D_FF), layer), once((3, 2 * D_FF)),
                _layer_weight((D_FF, d), layer)]
    args = [x, x, x, mod, g, wu, conv_w, wd]
    kw = dict(tm=tm, ctx=ctx, total=total)
    if final_g is None:
        kern, out_spec, out_rows = functools.partial(_ffn_mid_kernel, **kw), row, total
    else:
        off = ctx // tm
        kern = functools.partial(_ffn_kernel, **kw)
        out_spec = pl.BlockSpec((bsz, tm, d), lambda i: (0, jnp.maximum(i - off, 0), 0))
        out_rows = total - ctx
        in_specs.append(once((1, d)))
        args.append(final_g)
    return pl.pallas_call(
        kern,
        grid=(total // tm,),
        in_specs=in_specs,
        out_specs=out_spec,
        out_shape=jax.ShapeDtypeStruct((bsz, out_rows, d), F32),
        compiler_params=_cparams(("arbitrary",)),
    )(*args)


def _rope_tables(ctx, seq):
    rows = seq // GRID_W
    t_row = jnp.broadcast_to(jnp.arange(rows)[:, None], (rows, GRID_W)).reshape(-1).astype(F32)
    t_col = jnp.broadcast_to(jnp.arange(GRID_W)[None, :], (rows, GRID_W)).reshape(-1).astype(F32)
    n_freq = HEAD_DIM // 4
    inv = ROPE_THETA ** (-jnp.arange(n_freq, dtype=F32) / n_freq)
    ar, ac = t_row[:, None] * inv, t_col[:, None] * inv
    cos = jnp.concatenate([jnp.cos(ar), jnp.cos(ar), jnp.cos(ac), jnp.cos(ac)], axis=1)
    sin = jnp.concatenate([-jnp.sin(ar), jnp.sin(ar), -jnp.sin(ac), jnp.sin(ac)], axis=1)
    cos = jnp.concatenate([jnp.ones((ctx, HEAD_DIM), F32), cos], axis=0)
    sin = jnp.concatenate([jnp.zeros((ctx, HEAD_DIM), F32), sin], axis=0)
    return jnp.concatenate([jnp.tile(cos, (1, LANES // HEAD_DIM)), jnp.tile(sin, (1, LANES // HEAD_DIM))], axis=1)


def _block_diag2(w):
    z = jnp.zeros_like(w[0])
    return jnp.concatenate([jnp.concatenate([w[0], z], axis=1), jnp.concatenate([z, w[1]], axis=1)], axis=0)


def kernel(x, c, ctx, c_ctx, ada_w, ada_b, norm1_g, w_in, conv_a_w, a_out_w, rwkv_mu, rwkv_w0, rwkv_w2,
           rwkv_a0, rwkv_a2, rwkv_g2, rwkv_k_k, rwkv_k_a, rwkv_r_k, rwkv_ln_g, rwkv_ln_b, rwkv_out_w,
           attn_sink, attn_out_w, w_o, norm2_g, ffn_up, ffn_conv, ffn_down, final_norm_g):
    bsz, seq, d = x.shape
    n_ctx = ctx.shape[1]
    depth = ada_w.shape[0]
    total = n_ctx + seq
    assert d == D_MODEL and seq % ATT_BLOCK == 0 and n_ctx % ATT_BLOCK == 0 and bsz + 1 <= HALO
    tm = 256
    assert total % tm == 0 and n_ctx % tm == 0

    cvec = jnp.zeros((HALO, d), F32).at[:bsz].set(c).at[bsz].set(c_ctx)
    mods = _ada(cvec, ada_w.astype(BF16), ada_b)
    mods = mods.reshape(depth, HALO, 6, d)
    mods = jnp.stack([jnp.broadcast_to(mods[:, bsz:bsz + 1], (depth, bsz, 6, d)), mods[:, :bsz]], axis=2)

    rope_tab = _rope_tables(n_ctx, seq)
    masks = _scan_masks(tm)
    head_ones = masks[2].astype(BF16)
    xa = jnp.concatenate([ctx, x], axis=1)
    w_in, a_out_w, rwkv_out_w, attn_out_w, w_o, ffn_up, ffn_down = (
        w.astype(BF16) for w in (w_in, a_out_w, rwkv_out_w, attn_out_w, w_o, ffn_up, ffn_down))

    for l in range(depth):
        mod = mods[l]
        proj_a, proj_b = _proj(xa, mod, norm1_g[l][None], w_in, l, rope_tab, ctx=n_ctx, tm=tm // 2)

        at, rt, bt, kt, gend, v, bonus, g = _prep(
            proj_a, masks[0], head_ones, rwkv_mu[l], rwkv_w0[l].reshape(1, 2 * d), _block_diag2(rwkv_w2[l]).astype(BF16),
            rwkv_a0[l].reshape(1, 2 * d), _block_diag2(rwkv_a2[l]).astype(BF16), rwkv_g2[l].astype(BF16),
            rwkv_k_k[l][None], rwkv_k_a[l][None], rwkv_r_k[l].reshape(1, d), ctx=n_ctx, tm=tm)
        y = [_scan(at, rt, bt, kt, gend, v, masks, ctx=n_ctx, direction=dr) for dr in range(2)]
        att = _attention(proj_b, attn_sink[l], ctx=n_ctx)
        xa = _merge(xa, mod, proj_a, proj_b, att, y, bonus, g, conv_a_w[l], rwkv_ln_g[l][None], rwkv_ln_b[l][None],
                    a_out_w, rwkv_out_w, attn_out_w, w_o, l, ctx=n_ctx, tm=tm)
        xa = _ffn(xa, mod, norm2_g[l][None], ffn_up, ffn_conv[l], ffn_down, l,
                  final_norm_g[None] if l == depth - 1 else None, ctx=n_ctx, tm=tm)
    return xa
```

```python
import functools

import jax
import jax.numpy as jnp
import numpy as np
from jax import lax
from jax.experimental import pallas as pl
from jax.experimental.pallas import tpu as pltpu

D_MODEL = 1024
GRID_W = 64
N_HEADS = 16
HEAD_DIM = 64
N_KV = 4
R_LORA = 64
R_GATE = 128
GN_EPS = 64e-5
WINDOW = 128
ATT_BLOCK = 128
LOG2_ATT_BLOCK = 7
ATT_STACK = N_HEADS // N_KV
ATT_SCALE = HEAD_DIM ** -0.5
ROPE_THETA = 10000.0
NEG_INF = -1e30
D_FF = 2816
NORM_EPS = 1e-6
KK_EPS = 1e-12
DECAY_SCALE = float(np.exp(-0.5))

LANES = 128
HALO = 8
HALO_BF16 = 16
N_PAIR = D_MODEL // LANES
CHUNK = 64
N_LEVELS = 5
VMEM_LIMIT = 56 * 1024 * 1024

BF16 = jnp.bfloat16
F32 = jnp.float32

KV_W = N_KV * HEAD_DIM
LORA_W = 2 * R_LORA + 2 * R_LORA + R_GATE
W_CONV, W_RKV, W_LORA = 0, 3 * D_MODEL, 6 * D_MODEL
W_Q = W_LORA + LORA_W
W_GATE = W_Q + D_MODEL + 2 * KV_W
N_PROJ_B = D_MODEL + 2 * KV_W + 3 * D_MODEL
B_Q, B_K, B_V, B_GATE = 0, D_MODEL, D_MODEL + KV_W, D_MODEL + 2 * KV_W
GATE_BLK = B_GATE
assert B_K % KV_W == 0 and 2 * GATE_BLK == 3 * D_MODEL and GATE_BLK % LANES == 0


def _cparams(sem):
    return pltpu.CompilerParams(dimension_semantics=sem, vmem_limit_bytes=VMEM_LIMIT)


def _dot(a, b):
    return jnp.dot(a.astype(BF16), b.astype(BF16), preferred_element_type=F32)


def _dot_nt(a, b):
    return lax.dot_general(a.astype(BF16), b.astype(BF16), (((1,), (1,)), ((), ())),
                           preferred_element_type=F32)


def _bmm(a, b):
    return jnp.einsum('pmk,pkn->pmn', a.astype(BF16), b.astype(BF16), preferred_element_type=F32)


def _bmm_nt(a, b):
    return jnp.einsum('pmk,pnk->pmn', a.astype(BF16), b.astype(BF16), preferred_element_type=F32)


def _bmm_tn(a, b):
    return jnp.einsum('pkm,pkn->pmn', a.astype(BF16), b.astype(BF16), preferred_element_type=F32)


def _sigmoid(x):
    return 0.5 + 0.5 * jnp.tanh(0.5 * x)


def _row_ids(i, tm):
    return i * tm + lax.broadcasted_iota(jnp.int32, (tm, 1), 0)


def _mod_row(mod_ref, k, is_ctx):
    return jnp.where(is_ctx, mod_ref[0, k:k + 1, :], mod_ref[1, k:k + 1, :])


def _neighbours(z, before, after, i, tm, ctx, total):
    start = i * tm
    keep_before = jnp.where((start == 0) | (start == ctx), 0.0, 1.0)
    keep_after = jnp.where((start + tm == ctx) | (start + tm == total), 0.0, 1.0)
    row = lax.broadcasted_iota(jnp.int32, (HALO, 1), 0)
    prev = pltpu.roll(z, 1, 0)
    nxt = pltpu.roll(z, tm - 1, 0)
    top = jnp.where(row == 0, before * keep_before, prev[:HALO])
    bottom = jnp.where(row == HALO - 1, after * keep_after, nxt[tm - HALO:])
    return (jnp.concatenate([top, prev[HALO:]], axis=0),
            jnp.concatenate([nxt[:tm - HALO], bottom], axis=0))


def _layer_weight(shape, layer):
    return pl.BlockSpec((None,) + shape, lambda *_: (layer,) + tuple(0 for _ in shape),
                        pipeline_mode=pl.Buffered(1))


def _halo_specs(width, tm, total, col=0, rows=HALO):
    per = tm // rows
    nblk = total // rows
    before = pl.BlockSpec((None, rows, width), lambda b, i: (b, jnp.maximum(i * per - 1, 0), col))
    after = pl.BlockSpec((None, rows, width), lambda b, i: (b, jnp.minimum((i + 1) * per, nblk - 1), col))
    return before, after


def _edge_rows(before_ref, after_ref):
    before = before_ref[...].astype(F32)
    return before[before.shape[0] - 1:], after_ref[...].astype(F32)[0:1]


def _ada_kernel(c_ref, w_ref, b_ref, o_ref):
    c = c_ref[...]
    o_ref[...] = _dot(c * _sigmoid(c), w_ref[...]) + b_ref[...]


def _ada(cvec, ada_w, ada_b):
    depth, d, n = ada_w.shape
    tn = 1536
    return pl.pallas_call(
        _ada_kernel,
        grid=(depth, n // tn),
        in_specs=[pl.BlockSpec((HALO, d), lambda l, j: (0, 0)),
                  pl.BlockSpec((None, d, tn), lambda l, j: (l, 0, j)),
                  pl.BlockSpec((None, 1, tn), lambda l, j: (l, 0, j))],
        out_specs=pl.BlockSpec((None, HALO, tn), lambda l, j: (l, 0, j)),
        out_shape=jax.ShapeDtypeStruct((depth, HALO, n), F32),
        compiler_params=_cparams(("arbitrary", "arbitrary")),
    )(cvec, ada_w, ada_b.reshape(depth, 1, n))


def _rope(z, tab):
    cos, sin = tab[:, :LANES], tab[:, LANES:]
    lane = lax.broadcasted_iota(jnp.int32, (1, LANES), 1)
    first = jnp.bitwise_and(lane, HEAD_DIM // 2 - 1) < (HEAD_DIM // 4)
    out = []
    for p in range(z.shape[1] // LANES):
        zp = z[:, LANES * p:LANES * (p + 1)]
        partner = jnp.where(first, pltpu.roll(zp, LANES - HEAD_DIM // 4, 1), pltpu.roll(zp, HEAD_DIM // 4, 1))
        out.append(zp * cos + partner * sin)
    return jnp.concatenate(out, axis=1)


def _tile_neighbours(u, tm, keep_before, keep_after):
    ext = tm + 2 * HALO
    row = lax.broadcasted_iota(jnp.int32, (HALO, 1), 0)
    prev = pltpu.roll(u, 1, 0)[HALO:HALO + tm]
    nxt = pltpu.roll(u, ext - 1, 0)[HALO:HALO + tm]
    top = jnp.where(row == 0, prev[:HALO] * keep_before, prev[:HALO])
    bottom = jnp.where(row == HALO - 1, nxt[tm - HALO:] * keep_after, nxt[tm - HALO:])
    return (jnp.concatenate([top, prev[HALO:]], axis=0), jnp.concatenate([nxt[:tm - HALO], bottom], axis=0))


def _proj_kernel(x_ref, before_ref, after_ref, mod_ref, g_ref, w_ref, tab_ref,
                 tri_ref, ones_ref, mu_ref, w0_ref, w2_ref, a0_ref, a2_ref, g2_ref, kk_ref, ka_ref, rk_ref,
                 cv_out, ob_ref, at_out, rt_out, bt_out, kt_out, gend_out, v_out, bonus_out, gg_out, *,
                 tm, ctx, total):
    i = pl.program_id(0)
    bsz = x_ref.shape[0]
    d = D_MODEL
    start = i * tm
    ext = tm + 2 * HALO
    is_ctx = (start - HALO + lax.broadcasted_iota(jnp.int32, (ext, 1), 0)) < ctx
    h_ext = []
    for b in range(bsz):
        xe = jnp.concatenate([before_ref[b], x_ref[b], after_ref[b]], axis=0)
        y = xe * lax.rsqrt(jnp.mean(xe * xe, axis=-1, keepdims=True) + NORM_EPS) * g_ref[...]
        mod_b = mod_ref.at[b]
        h_ext.append((y * (1.0 + _mod_row(mod_b, 1, is_ctx)) + _mod_row(mod_b, 0, is_ctx)).astype(BF16))
    h = jnp.concatenate([he[HALO:HALO + tm] for he in h_ext], axis=0)

    rkv = _dot(jnp.concatenate(h_ext, axis=0), w_ref[:, W_RKV:W_RKV + 3 * d])
    lora = _dot(h, w_ref[:, W_LORA:W_LORA + LORA_W])
    qkv = _dot(h, w_ref[:, W_Q:W_GATE])
    tab = tab_ref[...]
    for b in range(bsz):
        rows = slice(b * tm, (b + 1) * tm)
        rot = _rope(qkv[rows, :B_V - B_Q], tab)
        ob_ref[b, :, B_Q:B_K] = (rot[:, :B_K - B_Q] * ATT_SCALE).astype(ob_ref.dtype)
        ob_ref[b, :, B_K:B_V] = rot[:, B_K - B_Q:].astype(ob_ref.dtype)
        ob_ref[b, :, B_V:B_GATE] = qkv[rows, B_V - B_Q:].astype(ob_ref.dtype)
    keep_before = jnp.where((start == 0) | (start == ctx), 0.0, 1.0)
    keep_after = jnp.where((start + tm == ctx) | (start + tm == total), 0.0, 1.0)

    def tokens(b):
        rkv_b = rkv[b * ext:(b + 1) * ext]
        prev, nxt = _tile_neighbours(rkv_b, tm, keep_before, keep_after)
        _rwkv_tokens(prev, rkv_b[HALO:HALO + tm], nxt, lora[b * tm:(b + 1) * tm],
                     tri_ref, ones_ref, mu_ref, w0_ref, w2_ref, a0_ref, a2_ref, g2_ref, kk_ref, ka_ref, rk_ref,
                     at_out.at[:, b], rt_out.at[:, b], bt_out.at[:, b], kt_out.at[:, b], gend_out.at[:, b],
                     v_out.at[b], bonus_out.at[b], gg_out.at[b], tm)

    gates = _dot(h, w_ref[:, W_GATE:])
    tokens(0)
    ob_ref[:, :, B_GATE:] = _sigmoid(gates).reshape(bsz, tm, 3 * d).astype(ob_ref.dtype)
    conv_cols = _dot(h, w_ref[:, W_CONV:W_CONV + 3 * d])
    for b in range(1, bsz):
        tokens(b)
    cv_out[...] = conv_cols.reshape(bsz, tm, 3 * d).astype(cv_out.dtype)


def _proj(x, mod, g, w, layer, rope_tab, tri, head_ones, mu, w0, w2bd, a0, a2bd, g2, k_k, k_a, r_k, *, ctx, tm):
    bsz, total, d = x.shape
    assert w.shape[2] == W_GATE + 3 * d
    kern = functools.partial(_proj_kernel, tm=tm, ctx=ctx, total=total)
    per, nblk = tm // HALO, total // HALO
    before = pl.BlockSpec((bsz, HALO, d), lambda i: (0, jnp.maximum(i * per - 1, 0), 0))
    after = pl.BlockSpec((bsz, HALO, d), lambda i: (0, jnp.minimum((i + 1) * per, nblk - 1), 0))
    once = lambda shp: pl.BlockSpec(shp, lambda i: tuple(0 for _ in shp), pipeline_mode=pl.Buffered(1))
    row = lambda n: pl.BlockSpec((bsz, tm, n), lambda i: (0, i, 0))
    pm = pl.BlockSpec((bsz, N_PAIR, tm, LANES), lambda i: (0, 0, i, 0))
    pm2 = pl.BlockSpec((2, bsz, N_PAIR, tm, LANES), lambda i: (0, 0, 0, i, 0))
    ge = pl.BlockSpec((2, bsz, tm // CHUNK, N_PAIR, LANES), lambda i: (0, 0, i, 0, 0))
    sds = jax.ShapeDtypeStruct((bsz, N_PAIR, total, LANES), BF16)
    sds2 = jax.ShapeDtypeStruct((2, bsz, N_PAIR, total, LANES), BF16)
    sds_ge = jax.ShapeDtypeStruct((2, bsz, total // CHUNK, N_PAIR, LANES), F32)
    return pl.pallas_call(
        kern,
        grid=(total // tm,),
        in_specs=[row(d), before, after, pl.BlockSpec((bsz, 2, 6, d), lambda i: (0, 0, 0, 0)),
                  once((1, d)), _layer_weight((d, W_GATE + 3 * d), layer),
                  pl.BlockSpec((tm, 2 * LANES), lambda i: (i, 0)),
                  once((2, tm, tm)), once((LANES, LANES)), once((2, 3 * d)), once((1, 2 * d)), once((LANES, 2 * d)),
                  once((1, 2 * d)), once((LANES, 2 * d)), once((R_GATE, d)), once((1, d)), once((1, d)),
                  once((1, d))],
        out_specs=[row(3 * d), row(N_PROJ_B), pm2, pm2, pm2, pm2, ge, pm, pm, pm],
        out_shape=[jax.ShapeDtypeStruct((bsz, total, 3 * d), BF16),
                   jax.ShapeDtypeStruct((bsz, total, N_PROJ_B), BF16),
                   sds2, sds2, sds2, sds2, sds_ge, sds, sds, sds],
        compiler_params=_cparams(("arbitrary",)),
    )(x, x, x, mod, g, w, rope_tab, tri, head_ones, mu, w0, w2bd, a0, a2bd, g2, k_k, k_a, r_k)


def _head_sums(z, head_ones):
    return jnp.dot(z.astype(BF16), head_ones, preferred_element_type=F32)


def _head_sums_exact(z, lane_lo):
    s_lo = jnp.sum(jnp.where(lane_lo, z, 0.0), axis=-1, keepdims=True)
    s_hi = jnp.sum(jnp.where(lane_lo, 0.0, z), axis=-1, keepdims=True)
    return jnp.where(lane_lo, s_lo, s_hi)


def _chunk_cumsum(tri, z):
    hi = z.astype(BF16)
    low = (z - hi.astype(F32)).astype(BF16)
    dot = functools.partial(jnp.dot, preferred_element_type=F32)
    return dot(tri, low) + dot(tri, hi)


def _rwkv_tokens(prev, x, nxt, lo, tri_ref, ones_ref, mu_ref, w0_ref, w2_ref, a0_ref, a2_ref, g2_ref, kk_ref,
                 ka_ref, rk_ref, at_out, rt_out, bt_out, kt_out, gend_out, v_out, bonus_out, g_out, tm):
    mu_prev = mu_ref[0:1, :]
    mu_next = mu_ref[1:2, :]
    s = prev * mu_prev + x * (1.0 - mu_prev - mu_next) + nxt * mu_next
    d = D_MODEL
    r, k, v = s[:, :d], s[:, d:2 * d], s[:, 2 * d:]

    w_pre = w0_ref[...] + _dot(jnp.tanh(lo[:, :LANES]), w2_ref[...])
    lw = -DECAY_SCALE * _sigmoid(w_pre)
    a = _sigmoid(a0_ref[...] + _dot(lo[:, LANES:2 * LANES], a2_ref[...]))
    g = _dot(_sigmoid(lo[:, 2 * LANES:]), g2_ref[...])
    kk = k * kk_ref[...]
    ka = ka_ref[...]
    rk = rk_ref[...]
    cl = [_chunk_cumsum(tri_ref[dr], lw[:, d * dr:d * (dr + 1)]) for dr in range(2)]

    head_ones = ones_ref[...]
    for p in range(N_PAIR):
        sl = slice(LANES * p, LANES * (p + 1))
        kkp = kk[:, sl]
        kn = kkp * lax.rsqrt(_head_sums(kkp * kkp, head_ones) + KK_EPS)
        rp, kp, vp = r[:, sl], k[:, sl], v[:, sl]
        bonus = jnp.zeros_like(rp)
        for dr in range(2):
            dsl = slice(d * dr + LANES * p, d * dr + LANES * (p + 1))
            ad = a[:, dsl]
            kd = kp * (1.0 + (ad - 1.0) * ka[:, sl])
            bonus = bonus + _head_sums(rp * kd * rk[:, sl], head_ones) * vp
            cl_incl = cl[dr][:, sl]
            inv_g = jnp.exp(-cl_incl)
            at_out[dr, p] = (-kn * jnp.exp(cl_incl - lw[:, dsl])).astype(BF16)
            rt_out[dr, p] = (rp * jnp.exp(cl_incl)).astype(BF16)
            bt_out[dr, p] = (kn * ad * inv_g).astype(BF16)
            kt_out[dr, p] = (kd * inv_g).astype(BF16)
            for q in range(tm // CHUNK):
                end = CHUNK * q + (CHUNK - 1 if dr == 0 else 0)
                gend_out[dr, q, p:p + 1, :] = jnp.exp(cl_incl[end:end + 1, :])
        v_out[p] = vp.astype(BF16)
        bonus_out[p] = bonus.astype(BF16)
        g_out[p] = g[:, sl].astype(BF16)


M_STRICT, M_INCL, M_EYE, M_LEVEL0 = 0, 1, 2, 3
N_MASKS = M_LEVEL0 + 1 + N_LEVELS


def _scan_masks(tm):
    c = CHUNK
    t = np.arange(c)[:, None]
    s = np.arange(c)[None, :]
    cum = np.zeros((2, tm, tm), np.float32)
    msk = np.zeros((2, N_MASKS, c, 2 * c), np.float32)
    for dr in range(2):
        earlier = (s < t) if dr == 0 else (s > t)
        cum[dr] = np.kron(np.eye(tm // c), earlier | (s == t))
        planes = [earlier, earlier | (s == t), s == t]
        m = 1
        while m < c:
            same = (t // (2 * m)) == (s // (2 * m))
            t_late = (t % (2 * m)) >= m
            s_late = (s % (2 * m)) >= m
            planes.append(same & (t_late & ~s_late if dr == 0 else ~t_late & s_late))
            m *= 2
        for q, plane in enumerate(planes):
            msk[dr, q] = np.concatenate([plane, plane], axis=1)
    bdm = np.kron(np.eye(2, dtype=np.float32), np.ones((c, c), np.float32))
    return jnp.asarray(cum, BF16), jnp.asarray(msk), jnp.asarray(bdm)


SCAN_GROUP = 4


def _scan_kernel(msk_ref, bdm_ref, at_ref, rt_ref, bt_ref, kt_ref, gend_ref, v_ref, y_ref, s_ref, *, reverse):
    c = CHUNK
    bsz = v_ref.shape[0]
    n = bsz * N_PAIR
    order = tuple(reversed(range(SCAN_GROUP))) if reverse else tuple(range(SCAN_GROUP))

    @pl.when(pl.program_id(0) == 0)
    def _():
        s_ref[...] = jnp.zeros_like(s_ref)

    bdm = bdm_ref[...]
    bdm_bf = bdm.astype(BF16)
    m_strict = msk_ref[M_STRICT]
    m_incl = msk_ref[M_INCL]

    def load(ref):
        z = ref[...].reshape(n, SCAN_GROUP * c, LANES)
        return jnp.concatenate([z[:, c * k:c * (k + 1)] for k in range(SCAN_GROUP)], axis=0)

    def stack(z):
        z = z.astype(BF16)
        return jnp.concatenate([z, z], axis=1) * bdm_bf

    bt, kt, v = load(bt_ref), load(kt_ref), load(v_ref)
    ar = jnp.concatenate([load(at_ref), load(rt_ref)], axis=1)
    gram = _bmm_nt(ar, jnp.concatenate([stack(bt), stack(kt)], axis=1))
    g_ab = gram[:, :c, :2 * c]
    from_v = jnp.concatenate([gram[:, :c, 2 * c:] * m_strict, gram[:, c:, 2 * c:] * m_incl], axis=1)
    from_v = _bmm(from_v, stack(v))
    y_from_u = gram[:, c:, :2 * c] * m_incl

    tri = msk_ref[M_EYE] + g_ab * msk_ref[M_LEVEL0]
    for lvl in range(1, N_LEVELS + 1):
        w = _bmm(g_ab * msk_ref[M_LEVEL0 + lvl], stack(tri))
        tri = tri + _bmm(tri, stack(w))

    state = s_ref[...]
    ys = [None] * SCAN_GROUP
    for k in order:
        sel = slice(n * k, n * (k + 1))
        g_end = jnp.stack([gend_ref[q // N_PAIR, k, q % N_PAIR:q % N_PAIR + 1, :] for q in range(n)], axis=0)
        partial = _bmm_nt(ar[sel], state) + from_v[sel]
        u = _bmm(tri[sel], stack(partial[:, :c]))
        ys[k] = partial[:, c:] + _bmm(y_from_u[sel], stack(u))
        upd = _bmm_tn(jnp.concatenate([u.astype(BF16), v[sel]], axis=1), jnp.concatenate([bt[sel], kt[sel]], axis=1))
        state = (state + upd * bdm) * g_end
    s_ref[...] = state
    y_ref[...] = jnp.concatenate(ys, axis=1).reshape(bsz, N_PAIR, SCAN_GROUP * c, LANES).astype(y_ref.dtype)


def _scan(at, rt, bt, kt, gend, v, masks, *, ctx, direction):
    bsz, _, total, _ = v.shape
    rows = SCAN_GROUP * CHUNK
    steps, ctx_steps = total // rows, ctx // rows
    assert steps * rows == total and ctx_steps * rows == ctx
    _, msk, bdm = masks

    def blk(j):
        return j if direction == 0 else jnp.where(j < ctx_steps, ctx_steps - 1 - j, steps + ctx_steps - 1 - j)

    shared = pl.BlockSpec((bsz, N_PAIR, rows, LANES), lambda j: (0, 0, blk(j), 0))
    per_dir = pl.BlockSpec((None, bsz, N_PAIR, rows, LANES), lambda j: (direction, 0, 0, blk(j), 0))
    ends = pl.BlockSpec((None, bsz, SCAN_GROUP, N_PAIR, LANES), lambda j: (direction, 0, blk(j), 0, 0))
    return pl.pallas_call(
        functools.partial(_scan_kernel, reverse=direction == 1),
        grid=(steps,),
        in_specs=[pl.BlockSpec((None, N_MASKS, CHUNK, 2 * CHUNK), lambda j: (direction, 0, 0, 0)),
                  pl.BlockSpec((2 * CHUNK, 2 * CHUNK), lambda j: (0, 0)),
                  per_dir, per_dir, per_dir, per_dir, ends, shared],
        out_specs=shared,
        out_shape=jax.ShapeDtypeStruct((bsz, N_PAIR, total, LANES), BF16),
        scratch_shapes=[pltpu.VMEM((bsz * N_PAIR, 2 * HEAD_DIM, 2 * HEAD_DIM), F32)],
        compiler_params=_cparams(("arbitrary",)),
    )(msk, bdm, at, rt, bt, kt, gend, v)


def _attn_kernel(sink_ref, q_ref, kvp_ref, kvc_ref, kvn_ref, kvx_ref, o_ref, *, n_ctx_blk, n_blk):
    i = pl.program_id(1)
    blk = ATT_BLOCK
    kv_refs = (kvp_ref, kvc_ref, kvn_ref, kvx_ref)
    q = q_ref[...].astype(F32)
    k_all = jnp.concatenate([ref[:, :KV_W] for ref in kv_refs], axis=0)
    v_all = jnp.concatenate([ref[:, KV_W:] for ref in kv_refs], axis=0)

    qi = lax.broadcasted_iota(jnp.int32, (blk, 3 * blk), 0)
    si = lax.broadcasted_iota(jnp.int32, (blk, 3 * blk), 1)
    rel = si - blk - qi
    key_blk = i - 1 + jnp.right_shift(si, LOG2_ATT_BLOCK)
    ok = (jnp.where(jnp.abs(rel) <= WINDOW, 1, 0) * jnp.where(key_blk >= n_ctx_blk, 1, 0)
          * jnp.where(key_blk < n_blk, 1, 0) * jnp.where(i >= n_ctx_blk, 1, 0))
    bias = jnp.where(ok > 0, 0.0, NEG_INF)

    lane_lo = lax.broadcasted_iota(jnp.int32, (1, LANES), 1) < HEAD_DIM
    stack = ATT_STACK
    outs = [None] * N_HEADS
    bias_g = jnp.concatenate([bias] * stack, axis=0)

    for h0 in range(0, N_HEADS, stack):
        kvh = h0 // (N_HEADS // N_KV)
        kv_slab = slice(LANES * (kvh // 2), LANES * (kvh // 2 + 1))
        kv_lo = kvh % 2 == 0
        rows = []
        for h in range(h0, h0 + stack):
            qp = q[:, LANES * (h // 2):LANES * (h // 2 + 1)]
            if (h % 2 == 0) != kv_lo:
                qp = pltpu.roll(qp, HEAD_DIM, 1)
            rows.append(jnp.where(lane_lo, qp, 0.0) if kv_lo else jnp.where(lane_lo, 0.0, qp))
        s = _dot_nt(jnp.concatenate(rows, axis=0), k_all[:, kv_slab])
        s = jnp.concatenate([s[:, :3 * blk] + bias_g, s[:, 3 * blk:]], axis=1)
        sink = jnp.concatenate([jnp.full((blk, 1), sink_ref[h], F32) for h in range(h0, h0 + stack)], axis=0)
        m = jnp.maximum(jnp.max(s, axis=-1, keepdims=True), sink)
        e = jnp.exp(s - m)
        denom = jnp.sum(e, axis=-1, keepdims=True) + jnp.exp(sink - m)
        o = _dot(e, v_all[:, kv_slab]) / denom
        for j, h in enumerate(range(h0, h0 + stack)):
            oh = o[j * blk:(j + 1) * blk]
            if (h % 2 == 0) != kv_lo:
                oh = pltpu.roll(oh, HEAD_DIM, 1)
            outs[h] = oh
    o_ref[...] = jnp.concatenate(
        [jnp.where(lane_lo, outs[2 * p], outs[2 * p + 1]) for p in range(N_PAIR)], axis=1).astype(o_ref.dtype)


def _attention(qkv, sink, *, ctx):
    bsz, total, _ = qkv.shape
    blk = ATT_BLOCK
    n_blk, n_ctx_blk = total // blk, ctx // blk
    assert B_V == B_K + KV_W and B_K % (2 * KV_W) == 0
    kv_col = B_K // (2 * KV_W)
    kern = functools.partial(_attn_kernel, n_ctx_blk=n_ctx_blk, n_blk=n_blk)
    prev_i = lambda i: jnp.maximum(i - 1, 0)
    next_i = lambda i: jnp.minimum(i + 1, n_blk - 1)
    same = lambda i: i

    def kv(row):
        return pl.BlockSpec((None, blk, 2 * KV_W), lambda b, i: (b, row(i), kv_col))

    return pl.pallas_call(
        kern,
        grid=(bsz, n_blk),
        in_specs=[pl.BlockSpec(memory_space=pltpu.SMEM),
                  pl.BlockSpec((None, blk, D_MODEL), lambda b, i: (b, i, B_Q // D_MODEL)),
                  kv(prev_i), kv(same), kv(next_i),
                  pl.BlockSpec((None, ctx, 2 * KV_W), lambda b, i: (b, 0, kv_col))],
        out_specs=pl.BlockSpec((None, blk, D_MODEL), lambda b, i: (b, i, 0)),
        out_shape=jax.ShapeDtypeStruct((bsz, total, D_MODEL), BF16),
        compiler_params=_cparams(("arbitrary", "arbitrary")),
    )(sink, qkv, qkv, qkv, qkv, qkv)


def _merge_kernel(x_ref, mod_ref, cv_ref, before_ref, after_ref, gate_lo_ref, gate_hi_ref, att_ref, yf_ref, yb_ref,
                  bonus_ref, g_ref,
                  cw_ref, lng_ref, lnb_ref, wa_ref, wb_ref, wc_ref, wo_ref, o_ref, *, tm, ctx, total):
    i = pl.program_id(0)
    bsz = x_ref.shape[0]
    d = D_MODEL
    cw = cw_ref[...]
    lng = lng_ref[...]
    lnb = lnb_ref[...]
    lane_lo = lax.broadcasted_iota(jnp.int32, (1, LANES), 1) < HEAD_DIM
    convs, rwkvs = [], []
    for b in range(bsz):
        cv = cv_ref[b].astype(F32)
        z = cv[:, d:2 * d] * cv[:, 2 * d:]
        before, after = _edge_rows(before_ref.at[b], after_ref.at[b])
        prev, nxt = _neighbours(z, before[:, d:2 * d] * before[:, 2 * d:], after[:, d:2 * d] * after[:, 2 * d:],
                                i, tm, ctx, total)
        convs.append((cv[:, :d] * (prev * cw[0:1] + z * cw[1:2] + nxt * cw[2:3])).astype(BF16))
        slabs = []
        for p in range(N_PAIR):
            sl = slice(LANES * p, LANES * (p + 1))
            y = yf_ref[b, p].astype(F32) + yb_ref[b, p].astype(F32)
            mean = _head_sums_exact(y, lane_lo) * (1.0 / HEAD_DIM)
            yc = y - mean
            var = _head_sums_exact(yc * yc, lane_lo) * (1.0 / HEAD_DIM)
            gn = yc * lax.rsqrt(var + GN_EPS) * lng[:, sl] + lnb[:, sl]
            slabs.append(((gn + bonus_ref[b, p]) * g_ref[b, p]).astype(BF16))
        rwkvs.append(jnp.concatenate(slabs, axis=1))

    rows = bsz * tm
    gates = jnp.concatenate([gate_lo_ref[...].reshape(rows, GATE_BLK), gate_hi_ref[...].reshape(rows, GATE_BLK)],
                            axis=1).astype(F32)
    m = (gates[:, :d] * _dot(jnp.concatenate(convs, axis=0), wa_ref[...])
         + gates[:, d:2 * d] * _dot(jnp.concatenate(rwkvs, axis=0), wb_ref[...])
         + gates[:, 2 * d:] * _dot(att_ref[...].reshape(rows, d), wc_ref[...]))
    out = _dot(m, wo_ref[...])
    is_ctx = _row_ids(i, tm) < ctx
    for b in range(bsz):
        o_ref[b] = x_ref[b] + _mod_row(mod_ref.at[b], 2, is_ctx) * out[b * tm:(b + 1) * tm]


def _merge(x, mod, proj_a, proj_b, att, y, bonus, g, conv_w, ln_g, ln_b, wa, wb, wc, wo, layer, *, ctx, tm):
    bsz, total, d = x.shape
    kern = functools.partial(_merge_kernel, tm=tm, ctx=ctx, total=total)
    per, nblk = tm // HALO_BF16, total // HALO_BF16
    col_cv, col_gate = 0, B_GATE // GATE_BLK
    before = pl.BlockSpec((bsz, HALO_BF16, 3 * d), lambda i: (0, jnp.maximum(i * per - 1, 0), col_cv))
    after = pl.BlockSpec((bsz, HALO_BF16, 3 * d), lambda i: (0, jnp.minimum((i + 1) * per, nblk - 1), col_cv))
    row = lambda w, col=0: pl.BlockSpec((bsz, tm, w), lambda i: (0, i, col))
    once = lambda shp: pl.BlockSpec(shp, lambda i: tuple(0 for _ in shp), pipeline_mode=pl.Buffered(1))
    pm = pl.BlockSpec((bsz, N_PAIR, tm, LANES), lambda i: (0, 0, i, 0))
    return pl.pallas_call(
        kern,
        grid=(total // tm,),
        in_specs=[row(d), pl.BlockSpec((bsz, 2, 6, d), lambda i: (0, 0, 0, 0)),
                  row(3 * d, col_cv), before, after, row(GATE_BLK, col_gate), row(GATE_BLK, col_gate + 1),
                  row(d), pm, pm, pm, pm,
                  once((3, d)), once((1, d)), once((1, d)),
                  _layer_weight((d, d), layer), _layer_weight((d, d), layer), _layer_weight((d, d), layer),
                  _layer_weight((d, d), layer)],
        out_specs=row(d),
        out_shape=jax.ShapeDtypeStruct((bsz, total, d), F32),
        compiler_params=_cparams(("arbitrary",)),
    )(x, mod, proj_a, proj_a, proj_a, proj_b, proj_b, att, y[0], y[1], bonus, g, conv_w, ln_g, ln_b,
      wa, wb, wc, wo)


def _ffn_kernel(x_ref, before_ref, after_ref, mod_ref, g_ref, wu_ref, cw_ref, wd_ref, fin_ref, o_ref, *,
                tm, ctx, total):
    i = pl.program_id(0)
    bsz = x_ref.shape[0]
    start = i * tm
    ext = tm + 2 * HALO
    is_ctx = (start - HALO + lax.broadcasted_iota(jnp.int32, (ext, 1), 0)) < ctx
    tile_is_ctx = _row_ids(i, tm) < ctx

    keep_before = jnp.where((start == 0) | (start == ctx), 0.0, 1.0)
    keep_after = jnp.where((start + tm == ctx) | (start + tm == total), 0.0, 1.0)
    row = lax.broadcasted_iota(jnp.int32, (HALO, 1), 0)
    first_row = row == 0
    last_row = row == HALO - 1

    def up(b):
        xe = jnp.concatenate([before_ref[b], x_ref[b], after_ref[b]], axis=0)
        y = xe * lax.rsqrt(jnp.mean(xe * xe, axis=-1, keepdims=True) + NORM_EPS) * g_ref[...]
        mod_b = mod_ref.at[b]
        h = (y * (1.0 + _mod_row(mod_b, 4, is_ctx)) + _mod_row(mod_b, 3, is_ctx)).astype(BF16)
        return _dot(h, wu_ref[:, :D_FF]), _dot(h, wu_ref[:, D_FF:])

    def conv(u, cw):
        prev = pltpu.roll(u, 1, 0)[HALO:HALO + tm]
        nxt = pltpu.roll(u, ext - 1, 0)[HALO:HALO + tm]
        top = jnp.where(first_row, prev[:HALO] * keep_before, prev[:HALO])
        bottom = jnp.where(last_row, nxt[tm - HALO:] * keep_after, nxt[tm - HALO:])
        prev = jnp.concatenate([top, prev[HALO:]], axis=0)
        nxt = jnp.concatenate([nxt[:tm - HALO], bottom], axis=0)
        return prev * cw[0:1] + u[HALO:HALO + tm] * cw[1:2] + nxt * cw[2:3]

    u_next = up(0)
    for b in range(bsz):
        u_gate, u_val = u_next
        if b + 1 < bsz:
            u_next = up(b + 1)
        ug = conv(u_gate, cw_ref[:, :D_FF])
        uv = conv(u_val, cw_ref[:, D_FF:])
        out = x_ref[b] + _mod_row(mod_ref.at[b], 5, tile_is_ctx) * _dot(ug * _sigmoid(ug) * uv, wd_ref[...])
        if fin_ref is not None:
            out = out * lax.rsqrt(jnp.mean(out * out, axis=-1, keepdims=True) + NORM_EPS) * fin_ref[...]
        o_ref[b] = out


def _ffn_mid_kernel(x_ref, before_ref, after_ref, mod_ref, g_ref, wu_ref, cw_ref, wd_ref, o_ref, **kw):
    _ffn_kernel(x_ref, before_ref, after_ref, mod_ref, g_ref, wu_ref, cw_ref, wd_ref, None, o_ref, **kw)


def _ffn(x, mod, g, wu, conv_w, wd, layer, final_g, *, ctx, tm):
    bsz, total, d = x.shape
    per, nblk = tm // HALO, total // HALO
    before = pl.BlockSpec((bsz, HALO, d), lambda i: (0, jnp.maximum(i * per - 1, 0), 0))
    after = pl.BlockSpec((bsz, HALO, d), lambda i: (0, jnp.minimum((i + 1) * per, nblk - 1), 0))
    row = pl.BlockSpec((bsz, tm, d), lambda i: (0, i, 0))
    once = lambda shp: pl.BlockSpec(shp, lambda i: tuple(0 for _ in shp), pipeline_mode=pl.Buffered(1))
    in_specs = [row, before, after, pl.BlockSpec((bsz, 2, 6, d), lambda i: (0, 0, 0, 0)),
                once((1, d)), _layer_weight((d, 2 * D_FF), layer), once((3, 2 * D_FF)),
                _layer_weight((D_FF, d), layer)]
    args = [x, x, x, mod, g, wu, conv_w, wd]
    kw = dict(tm=tm, ctx=ctx, total=total)
    if final_g is None:
        kern, out_spec, out_rows = functools.partial(_ffn_mid_kernel, **kw), row, total
    else:
        off = ctx // tm
        kern = functools.partial(_ffn_kernel, **kw)
        out_spec = pl.BlockSpec((bsz, tm, d), lambda i: (0, jnp.maximum(i - off, 0), 0))
        out_rows = total - ctx
        in_specs.append(once((1, d)))
        args.append(final_g)
    return pl.pallas_call(
        kern,
        grid=(total // tm,),
        in_specs=in_specs,
        out_specs=out_spec,
        out_shape=jax.ShapeDtypeStruct((bsz, out_rows, d), F32),
        compiler_params=_cparams(("arbitrary",)),
    )(*args)


def _rope_tables(ctx, seq):
    rows = seq // GRID_W
    t_row = jnp.broadcast_to(jnp.arange(rows)[:, None], (rows, GRID_W)).reshape(-1).astype(F32)
    t_col = jnp.broadcast_to(jnp.arange(GRID_W)[None, :], (rows, GRID_W)).reshape(-1).astype(F32)
    n_freq = HEAD_DIM // 4
    inv = ROPE_THETA ** (-jnp.arange(n_freq, dtype=F32) / n_freq)
    ar, ac = t_row[:, None] * inv, t_col[:, None] * inv
    cos = jnp.concatenate([jnp.cos(ar), jnp.cos(ar), jnp.cos(ac), jnp.cos(ac)], axis=1)
    sin = jnp.concatenate([-jnp.sin(ar), jnp.sin(ar), -jnp.sin(ac), jnp.sin(ac)], axis=1)
    cos = jnp.concatenate([jnp.ones((ctx, HEAD_DIM), F32), cos], axis=0)
    sin = jnp.concatenate([jnp.zeros((ctx, HEAD_DIM), F32), sin], axis=0)
    return jnp.concatenate([jnp.tile(cos, (1, LANES // HEAD_DIM)), jnp.tile(sin, (1, LANES // HEAD_DIM))], axis=1)


def _block_diag2(w):
    z = jnp.zeros_like(w[0])
    return jnp.concatenate([jnp.concatenate([w[0], z], axis=1), jnp.concatenate([z, w[1]], axis=1)], axis=0)


def kernel(x, c, ctx, c_ctx, ada_w, ada_b, norm1_g, w_in, conv_a_w, a_out_w, rwkv_mu, rwkv_w0, rwkv_w2,
           rwkv_a0, rwkv_a2, rwkv_g2, rwkv_k_k, rwkv_k_a, rwkv_r_k, rwkv_ln_g, rwkv_ln_b, rwkv_out_w,
           attn_sink, attn_out_w, w_o, norm2_g, ffn_up, ffn_conv, ffn_down, final_norm_g):
    bsz, seq, d = x.shape
    n_ctx = ctx.shape[1]
    depth = ada_w.shape[0]
    total = n_ctx + seq
    assert d == D_MODEL and seq % ATT_BLOCK == 0 and n_ctx % ATT_BLOCK == 0 and bsz + 1 <= HALO
    tm = 256
    assert total % tm == 0 and n_ctx % tm == 0

    cvec = jnp.zeros((HALO, d), F32).at[:bsz].set(c).at[bsz].set(c_ctx)
    mods = _ada(cvec, ada_w.astype(BF16), ada_b)
    mods = mods.reshape(depth, HALO, 6, d)
    mods = jnp.stack([jnp.broadcast_to(mods[:, bsz:bsz + 1], (depth, bsz, 6, d)), mods[:, :bsz]], axis=2)

    rope_tab = _rope_tables(n_ctx, seq)
    tm_proj = tm // 2
    masks = _scan_masks(tm_proj)
    head_ones = masks[2].astype(BF16)
    xa = jnp.concatenate([ctx, x], axis=1)
    w_in, a_out_w, rwkv_out_w, attn_out_w, w_o, ffn_up, ffn_down = (
        w.astype(BF16) for w in (w_in, a_out_w, rwkv_out_w, attn_out_w, w_o, ffn_up, ffn_down))

    for l in range(depth):
        mod = mods[l]
        proj_a, proj_b, at, rt, bt, kt, gend, v, bonus, g = _proj(
            xa, mod, norm1_g[l][None], w_in, l, rope_tab, masks[0], head_ones, rwkv_mu[l],
            rwkv_w0[l].reshape(1, 2 * d), _block_diag2(rwkv_w2[l]).astype(BF16),
            rwkv_a0[l].reshape(1, 2 * d), _block_diag2(rwkv_a2[l]).astype(BF16), rwkv_g2[l].astype(BF16),
            rwkv_k_k[l][None], rwkv_k_a[l][None], rwkv_r_k[l].reshape(1, d), ctx=n_ctx, tm=tm_proj)
        y = [_scan(at, rt, bt, kt, gend, v, masks, ctx=n_ctx, direction=dr) for dr in range(2)]
        att = _attention(proj_b, attn_sink[l], ctx=n_ctx)
        xa = _merge(xa, mod, proj_a, proj_b, att, y, bonus, g, conv_a_w[l], rwkv_ln_g[l][None], rwkv_ln_b[l][None],
                    a_out_w, rwkv_out_w, attn_out_w, w_o, l, ctx=n_ctx, tm=tm)
        xa = _ffn(xa, mod, norm2_g[l][None], ffn_up, ffn_conv[l], ffn_down, l,
                  final_norm_g[None] if l == depth - 1 else None, ctx=n_ctx, tm=tm)
    return xa
```

```python
import functools
import itertools

import jax
import jax.numpy as jnp
import numpy as np
from jax import lax
from jax.experimental import pallas as pl
from jax.experimental.pallas import tpu as pltpu

D_MODEL = 1024
GRID_W = 64
N_HEADS = 16
HEAD_DIM = 64
N_KV = 4
R_LORA = 64
R_GATE = 128
GN_EPS = 64e-5
WINDOW = 128
ATT_BLOCK = 128
LOG2_ATT_BLOCK = 7
ATT_STACK = N_HEADS // N_KV
ATT_SCALE = HEAD_DIM ** -0.5
ROPE_THETA = 10000.0
NEG_INF = -1e30
D_FF = 2816
NORM_EPS = 1e-6
KK_EPS = 1e-12
DECAY_SCALE = float(np.exp(-0.5))

LANES = 128
HALO = 8
HALO_BF16 = 16
N_PAIR = D_MODEL // LANES
CHUNK = 64
N_LEVELS = 5
VMEM_LIMIT = 56 * 1024 * 1024

BF16 = jnp.bfloat16
F32 = jnp.float32

KV_W = N_KV * HEAD_DIM
LORA_W = 2 * R_LORA + 2 * R_LORA + R_GATE
W_CONV, W_RKV, W_LORA = 0, 3 * D_MODEL, 6 * D_MODEL
W_Q = W_LORA + LORA_W
W_GATE = W_Q + D_MODEL + 2 * KV_W
N_PROJ_B = D_MODEL + 2 * KV_W + 3 * D_MODEL
B_Q, B_K, B_V, B_GATE = 0, D_MODEL, D_MODEL + KV_W, D_MODEL + 2 * KV_W
GATE_BLK = B_GATE
PROJ_CHUNK = 2 * KV_W
assert B_K % KV_W == 0 and 2 * GATE_BLK == 3 * D_MODEL and GATE_BLK % LANES == 0


def _cparams(sem):
    return pltpu.CompilerParams(dimension_semantics=sem, vmem_limit_bytes=VMEM_LIMIT)


def _dot(a, b):
    return jnp.dot(a.astype(BF16), b.astype(BF16), preferred_element_type=F32)


def _dot_nt(a, b):
    return lax.dot_general(a.astype(BF16), b.astype(BF16), (((1,), (1,)), ((), ())),
                           preferred_element_type=F32)


def _bmm(a, b):
    return jnp.einsum('pmk,pkn->pmn', a.astype(BF16), b.astype(BF16), preferred_element_type=F32)


def _bmm_nt(a, b):
    return jnp.einsum('pmk,pnk->pmn', a.astype(BF16), b.astype(BF16), preferred_element_type=F32)


def _bmm_tn(a, b):
    return jnp.einsum('pkm,pkn->pmn', a.astype(BF16), b.astype(BF16), preferred_element_type=F32)


def _sigmoid(x):
    return 0.5 + 0.5 * jnp.tanh(0.5 * x)


def _row_ids(i, tm):
    return i * tm + lax.broadcasted_iota(jnp.int32, (tm, 1), 0)


def _mod_row(mod_ref, k, is_ctx):
    return jnp.where(is_ctx, mod_ref[0, k:k + 1, :], mod_ref[1, k:k + 1, :])


def _neighbours(z, before, after, i, tm, ctx, total):
    start = i * tm
    keep_before = jnp.where((start == 0) | (start == ctx), 0.0, 1.0)
    keep_after = jnp.where((start + tm == ctx) | (start + tm == total), 0.0, 1.0)
    row = lax.broadcasted_iota(jnp.int32, (HALO, 1), 0)
    prev = pltpu.roll(z, 1, 0)
    nxt = pltpu.roll(z, tm - 1, 0)
    top = jnp.where(row == 0, before * keep_before, prev[:HALO])
    bottom = jnp.where(row == HALO - 1, after * keep_after, nxt[tm - HALO:])
    return (jnp.concatenate([top, prev[HALO:]], axis=0),
            jnp.concatenate([nxt[:tm - HALO], bottom], axis=0))


def _layer_weight(shape, layer):
    return pl.BlockSpec((None,) + shape, lambda *_: (layer,) + tuple(0 for _ in shape),
                        pipeline_mode=pl.Buffered(1))


def _halo_specs(width, tm, total, col=0, rows=HALO):
    per = tm // rows
    nblk = total // rows
    before = pl.BlockSpec((None, rows, width), lambda b, i: (b, jnp.maximum(i * per - 1, 0), col))
    after = pl.BlockSpec((None, rows, width), lambda b, i: (b, jnp.minimum((i + 1) * per, nblk - 1), col))
    return before, after


def _edge_rows(before_ref, after_ref):
    before = before_ref[...].astype(F32)
    return before[before.shape[0] - 1:], after_ref[...].astype(F32)[0:1]


def _ada_kernel(c_ref, w_ref, b_ref, o_ref):
    c = c_ref[...]
    o_ref[...] = _dot(c * _sigmoid(c), w_ref[...]) + b_ref[...]


def _ada(cvec, ada_w, ada_b):
    depth, d, n = ada_w.shape
    tn = 1536
    return pl.pallas_call(
        _ada_kernel,
        grid=(depth, n // tn),
        in_specs=[pl.BlockSpec((HALO, d), lambda l, j: (0, 0)),
                  pl.BlockSpec((None, d, tn), lambda l, j: (l, 0, j)),
                  pl.BlockSpec((None, 1, tn), lambda l, j: (l, 0, j))],
        out_specs=pl.BlockSpec((None, HALO, tn), lambda l, j: (l, 0, j)),
        out_shape=jax.ShapeDtypeStruct((depth, HALO, n), F32),
        compiler_params=_cparams(("arbitrary", "arbitrary")),
    )(cvec, ada_w, ada_b.reshape(depth, 1, n))


def _rope(z, tab):
    cos, sin = tab[:, :LANES], tab[:, LANES:]
    lane = lax.broadcasted_iota(jnp.int32, (1, LANES), 1)
    first = jnp.bitwise_and(lane, HEAD_DIM // 2 - 1) < (HEAD_DIM // 4)
    out = []
    for p in range(z.shape[1] // LANES):
        zp = z[:, LANES * p:LANES * (p + 1)]
        partner = jnp.where(first, pltpu.roll(zp, LANES - HEAD_DIM // 4, 1), pltpu.roll(zp, HEAD_DIM // 4, 1))
        out.append(zp * cos + partner * sin)
    return jnp.concatenate(out, axis=1)


def _tile_neighbours(u, tm, keep_before, keep_after):
    ext = tm + 2 * HALO
    row = lax.broadcasted_iota(jnp.int32, (HALO, 1), 0)
    prev = pltpu.roll(u, 1, 0)[HALO:HALO + tm]
    nxt = pltpu.roll(u, ext - 1, 0)[HALO:HALO + tm]
    top = jnp.where(row == 0, prev[:HALO] * keep_before, prev[:HALO])
    bottom = jnp.where(row == HALO - 1, nxt[tm - HALO:] * keep_after, nxt[tm - HALO:])
    return (jnp.concatenate([top, prev[HALO:]], axis=0), jnp.concatenate([nxt[:tm - HALO], bottom], axis=0))


def _proj_kernel(x_ref, before_ref, after_ref, mod_ref, g_ref, w_ref, tab_ref,
                 tri_ref, ones_ref, mu_ref, w0_ref, w2_ref, a0_ref, a2_ref, g2_ref, kk_ref, ka_ref, rk_ref,
                 cv_out, ob_ref, at_out, rt_out, bt_out, kt_out, gend_out, v_out, bonus_out, gg_out, *,
                 tm, ctx, total):
    i = pl.program_id(0)
    bsz = x_ref.shape[0]
    d = D_MODEL
    start = i * tm
    ext = tm + 2 * HALO
    is_ctx = (start - HALO + lax.broadcasted_iota(jnp.int32, (ext, 1), 0)) < ctx
    h_ext = []
    for b in range(bsz):
        xe = jnp.concatenate([before_ref[b], x_ref[b], after_ref[b]], axis=0)
        y = xe * lax.rsqrt(jnp.mean(xe * xe, axis=-1, keepdims=True) + NORM_EPS) * g_ref[...]
        mod_b = mod_ref.at[b]
        h_ext.append((y * (1.0 + _mod_row(mod_b, 1, is_ctx)) + _mod_row(mod_b, 0, is_ctx)).astype(BF16))
    h = jnp.concatenate([he[HALO:HALO + tm] for he in h_ext], axis=0)

    rkv = _dot(jnp.concatenate(h_ext, axis=0), w_ref[:, W_RKV:W_RKV + 3 * d])
    lora = _dot(h, w_ref[:, W_LORA:W_LORA + LORA_W])
    tab = tab_ref[...]
    keep_before = jnp.where((start == 0) | (start == ctx), 0.0, 1.0)
    keep_after = jnp.where((start + tm == ctx) | (start + tm == total), 0.0, 1.0)

    def tokens(b):
        rkv_b = rkv[b * ext:(b + 1) * ext]
        prev, nxt = _tile_neighbours(rkv_b, tm, keep_before, keep_after)
        yield from _rwkv_tokens(
            prev, rkv_b[HALO:HALO + tm], nxt, lora[b * tm:(b + 1) * tm],
            tri_ref, ones_ref, mu_ref, w0_ref, w2_ref, a0_ref, a2_ref, g2_ref, kk_ref, ka_ref, rk_ref,
            at_out.at[:, b], rt_out.at[:, b], bt_out.at[:, b], kt_out.at[:, b], gend_out.at[:, b],
            v_out.at[b], bonus_out.at[b], gg_out.at[b], tm)

    def columns(w_lo, out_ref, o_lo, kind):
        out = _dot(h, w_ref[:, w_lo:w_lo + PROJ_CHUNK])
        for b in range(bsz):
            z = out[b * tm:(b + 1) * tm]
            if kind == "q":
                z = _rope(z, tab) * ATT_SCALE
            elif kind == "kv":
                z = jnp.concatenate([_rope(z[:, :KV_W], tab), z[:, KV_W:]], axis=1)
            elif kind == "gate":
                z = _sigmoid(z)
            out_ref[b, :, o_lo:o_lo + PROJ_CHUNK] = z.astype(out_ref.dtype)

    chunks = ([(W_Q + c, ob_ref, B_Q + c, "q") for c in range(0, D_MODEL, PROJ_CHUNK)]
              + [(W_Q + D_MODEL, ob_ref, B_K, "kv")]
              + [(W_GATE + c, ob_ref, B_GATE + c, "gate") for c in range(0, 3 * d, PROJ_CHUNK)]
              + [(W_CONV + c, cv_out, c, "conv") for c in range(0, 3 * d, PROJ_CHUNK)])
    pieces = itertools.chain.from_iterable(tokens(b) for b in range(bsz))
    for chunk in chunks:
        columns(*chunk)
        next(pieces, None)
    for _ in pieces:
        pass


def _proj(x, mod, g, w, layer, rope_tab, tri, head_ones, mu, w0, w2bd, a0, a2bd, g2, k_k, k_a, r_k, *, ctx, tm):
    bsz, total, d = x.shape
    assert w.shape[2] == W_GATE + 3 * d
    kern = functools.partial(_proj_kernel, tm=tm, ctx=ctx, total=total)
    per, nblk = tm // HALO, total // HALO
    before = pl.BlockSpec((bsz, HALO, d), lambda i: (0, jnp.maximum(i * per - 1, 0), 0))
    after = pl.BlockSpec((bsz, HALO, d), lambda i: (0, jnp.minimum((i + 1) * per, nblk - 1), 0))
    once = lambda shp: pl.BlockSpec(shp, lambda i: tuple(0 for _ in shp), pipeline_mode=pl.Buffered(1))
    row = lambda n: pl.BlockSpec((bsz, tm, n), lambda i: (0, i, 0))
    pm = pl.BlockSpec((bsz, N_PAIR, tm, LANES), lambda i: (0, 0, i, 0))
    pm2 = pl.BlockSpec((2, bsz, N_PAIR, tm, LANES), lambda i: (0, 0, 0, i, 0))
    ge = pl.BlockSpec((2, bsz, tm // CHUNK, N_PAIR, LANES), lambda i: (0, 0, i, 0, 0))
    sds = jax.ShapeDtypeStruct((bsz, N_PAIR, total, LANES), BF16)
    sds2 = jax.ShapeDtypeStruct((2, bsz, N_PAIR, total, LANES), BF16)
    sds_ge = jax.ShapeDtypeStruct((2, bsz, total // CHUNK, N_PAIR, LANES), F32)
    return pl.pallas_call(
        kern,
        grid=(total // tm,),
        in_specs=[row(d), before, after, pl.BlockSpec((bsz, 2, 6, d), lambda i: (0, 0, 0, 0)),
                  once((1, d)), _layer_weight((d, W_GATE + 3 * d), layer),
                  pl.BlockSpec((tm, 2 * LANES), lambda i: (i, 0)),
                  once((2, tm, tm)), once((LANES, LANES)), once((2, 3 * d)), once((1, 2 * d)), once((LANES, 2 * d)),
                  once((1, 2 * d)), once((LANES, 2 * d)), once((R_GATE, d)), once((1, d)), once((1, d)),
                  once((1, d))],
        out_specs=[row(3 * d), row(N_PROJ_B), pm2, pm2, pm2, pm2, ge, pm, pm, pm],
        out_shape=[jax.ShapeDtypeStruct((bsz, total, 3 * d), BF16),
                   jax.ShapeDtypeStruct((bsz, total, N_PROJ_B), BF16),
                   sds2, sds2, sds2, sds2, sds_ge, sds, sds, sds],
        compiler_params=_cparams(("arbitrary",)),
    )(x, x, x, mod, g, w, rope_tab, tri, head_ones, mu, w0, w2bd, a0, a2bd, g2, k_k, k_a, r_k)


def _head_sums(z, head_ones):
    return jnp.dot(z.astype(BF16), head_ones, preferred_element_type=F32)


def _head_sums_exact(z, lane_lo):
    s_lo = jnp.sum(jnp.where(lane_lo, z, 0.0), axis=-1, keepdims=True)
    s_hi = jnp.sum(jnp.where(lane_lo, 0.0, z), axis=-1, keepdims=True)
    return jnp.where(lane_lo, s_lo, s_hi)


def _chunk_cumsum(tri, z):
    hi = z.astype(BF16)
    low = (z - hi.astype(F32)).astype(BF16)
    dot = functools.partial(jnp.dot, preferred_element_type=F32)
    return dot(tri, low) + dot(tri, hi)


def _rwkv_tokens(prev, x, nxt, lo, tri_ref, ones_ref, mu_ref, w0_ref, w2_ref, a0_ref, a2_ref, g2_ref, kk_ref,
                 ka_ref, rk_ref, at_out, rt_out, bt_out, kt_out, gend_out, v_out, bonus_out, g_out, tm):
    mu_prev = mu_ref[0:1, :]
    mu_next = mu_ref[1:2, :]
    s = prev * mu_prev + x * (1.0 - mu_prev - mu_next) + nxt * mu_next
    d = D_MODEL
    r, k, v = s[:, :d], s[:, d:2 * d], s[:, 2 * d:]

    w_pre = w0_ref[...] + _dot(jnp.tanh(lo[:, :LANES]), w2_ref[...])
    lw = -DECAY_SCALE * _sigmoid(w_pre)
    a = _sigmoid(a0_ref[...] + _dot(lo[:, LANES:2 * LANES], a2_ref[...]))
    g = _dot(_sigmoid(lo[:, 2 * LANES:]), g2_ref[...])
    kk = k * kk_ref[...]
    ka = ka_ref[...]
    rk = rk_ref[...]
    cl = [_chunk_cumsum(tri_ref[dr], lw[:, d * dr:d * (dr + 1)]) for dr in range(2)]

    head_ones = ones_ref[...]
    yield
    for p in range(N_PAIR):
        sl = slice(LANES * p, LANES * (p + 1))
        kkp = kk[:, sl]
        kn = kkp * lax.rsqrt(_head_sums(kkp * kkp, head_ones) + KK_EPS)
        rp, kp, vp = r[:, sl], k[:, sl], v[:, sl]
        bonus = jnp.zeros_like(rp)
        for dr in range(2):
            dsl = slice(d * dr + LANES * p, d * dr + LANES * (p + 1))
            ad = a[:, dsl]
            kd = kp * (1.0 + (ad - 1.0) * ka[:, sl])
            bonus = bonus + _head_sums(rp * kd * rk[:, sl], head_ones) * vp
            cl_incl = cl[dr][:, sl]
            inv_g = jnp.exp(-cl_incl)
            at_out[dr, p] = (-kn * jnp.exp(cl_incl - lw[:, dsl])).astype(BF16)
            rt_out[dr, p] = (rp * jnp.exp(cl_incl)).astype(BF16)
            bt_out[dr, p] = (kn * ad * inv_g).astype(BF16)
            kt_out[dr, p] = (kd * inv_g).astype(BF16)
            for q in range(tm // CHUNK):
                end = CHUNK * q + (CHUNK - 1 if dr == 0 else 0)
                gend_out[dr, q, p:p + 1, :] = jnp.exp(cl_incl[end:end + 1, :])
        v_out[p] = vp.astype(BF16)
        bonus_out[p] = bonus.astype(BF16)
        g_out[p] = g[:, sl].astype(BF16)
        yield


M_STRICT, M_INCL, M_EYE, M_LEVEL0 = 0, 1, 2, 3
N_MASKS = M_LEVEL0 + 1 + N_LEVELS


def _scan_masks(tm):
    c = CHUNK
    t = np.arange(c)[:, None]
    s = np.arange(c)[None, :]
    cum = np.zeros((2, tm, tm), np.float32)
    msk = np.zeros((2, N_MASKS, c, 2 * c), np.float32)
    for dr in range(2):
        earlier = (s < t) if dr == 0 else (s > t)
        cum[dr] = np.kron(np.eye(tm // c), earlier | (s == t))
        planes = [earlier, earlier | (s == t), s == t]
        m = 1
        while m < c:
            same = (t // (2 * m)) == (s // (2 * m))
            t_late = (t % (2 * m)) >= m
            s_late = (s % (2 * m)) >= m
            planes.append(same & (t_late & ~s_late if dr == 0 else ~t_late & s_late))
            m *= 2
        for q, plane in enumerate(planes):
            msk[dr, q] = np.concatenate([plane, plane], axis=1)
    bdm = np.kron(np.eye(2, dtype=np.float32), np.ones((c, c), np.float32))
    return jnp.asarray(cum, BF16), jnp.asarray(msk), jnp.asarray(bdm)


SCAN_GROUP = 4


def _scan_kernel(msk_ref, bdm_ref, at_ref, rt_ref, bt_ref, kt_ref, gend_ref, v_ref, y_ref, s_ref, *, reverse):
    c = CHUNK
    bsz = v_ref.shape[0]
    n = bsz * N_PAIR
    order = tuple(reversed(range(SCAN_GROUP))) if reverse else tuple(range(SCAN_GROUP))

    @pl.when(pl.program_id(0) == 0)
    def _():
        s_ref[...] = jnp.zeros_like(s_ref)

    bdm = bdm_ref[...]
    bdm_bf = bdm.astype(BF16)
    m_strict = msk_ref[M_STRICT]
    m_incl = msk_ref[M_INCL]

    def load(ref):
        z = ref[...].reshape(n, SCAN_GROUP * c, LANES)
        return jnp.concatenate([z[:, c * k:c * (k + 1)] for k in range(SCAN_GROUP)], axis=0)

    def stack(z):
        z = z.astype(BF16)
        return jnp.concatenate([z, z], axis=1) * bdm_bf

    bt, kt, v = load(bt_ref), load(kt_ref), load(v_ref)
    ar = jnp.concatenate([load(at_ref), load(rt_ref)], axis=1)
    gram = _bmm_nt(ar, jnp.concatenate([stack(bt), stack(kt)], axis=1))
    g_ab = gram[:, :c, :2 * c]
    from_v = jnp.concatenate([gram[:, :c, 2 * c:] * m_strict, gram[:, c:, 2 * c:] * m_incl], axis=1)
    from_v = _bmm(from_v, stack(v))
    y_from_u = gram[:, c:, :2 * c] * m_incl

    tri = msk_ref[M_EYE] + g_ab * msk_ref[M_LEVEL0]
    for lvl in range(1, N_LEVELS + 1):
        w = _bmm(g_ab * msk_ref[M_LEVEL0 + lvl], stack(tri))
        tri = tri + _bmm(tri, stack(w))

    state = s_ref[...]
    ys = [None] * SCAN_GROUP
    for k in order:
        sel = slice(n * k, n * (k + 1))
        g_end = jnp.stack([gend_ref[q // N_PAIR, k, q % N_PAIR:q % N_PAIR + 1, :] for q in range(n)], axis=0)
        partial = _bmm_nt(ar[sel], state) + from_v[sel]
        u = _bmm(tri[sel], stack(partial[:, :c]))
        ys[k] = partial[:, c:] + _bmm(y_from_u[sel], stack(u))
        upd = _bmm_tn(jnp.concatenate([u.astype(BF16), v[sel]], axis=1), jnp.concatenate([bt[sel], kt[sel]], axis=1))
        state = (state + upd * bdm) * g_end
    s_ref[...] = state
    y_ref[...] = jnp.concatenate(ys, axis=1).reshape(bsz, N_PAIR, SCAN_GROUP * c, LANES).astype(y_ref.dtype)


def _scan(at, rt, bt, kt, gend, v, masks, *, ctx, direction):
    bsz, _, total, _ = v.shape
    rows = SCAN_GROUP * CHUNK
    steps, ctx_steps = total // rows, ctx // rows
    assert steps * rows == total and ctx_steps * rows == ctx
    _, msk, bdm = masks

    def blk(j):
        return j if direction == 0 else jnp.where(j < ctx_steps, ctx_steps - 1 - j, steps + ctx_steps - 1 - j)

    shared = pl.BlockSpec((bsz, N_PAIR, rows, LANES), lambda j: (0, 0, blk(j), 0))
    per_dir = pl.BlockSpec((None, bsz, N_PAIR, rows, LANES), lambda j: (direction, 0, 0, blk(j), 0))
    ends = pl.BlockSpec((None, bsz, SCAN_GROUP, N_PAIR, LANES), lambda j: (direction, 0, blk(j), 0, 0))
    return pl.pallas_call(
        functools.partial(_scan_kernel, reverse=direction == 1),
        grid=(steps,),
        in_specs=[pl.BlockSpec((None, N_MASKS, CHUNK, 2 * CHUNK), lambda j: (direction, 0, 0, 0)),
                  pl.BlockSpec((2 * CHUNK, 2 * CHUNK), lambda j: (0, 0)),
                  per_dir, per_dir, per_dir, per_dir, ends, shared],
        out_specs=shared,
        out_shape=jax.ShapeDtypeStruct((bsz, N_PAIR, total, LANES), BF16),
        scratch_shapes=[pltpu.VMEM((bsz * N_PAIR, 2 * HEAD_DIM, 2 * HEAD_DIM), F32)],
        compiler_params=_cparams(("arbitrary",)),
    )(msk, bdm, at, rt, bt, kt, gend, v)


def _attn_kernel(sink_ref, q_ref, kvp_ref, kvc_ref, kvn_ref, kvx_ref, o_ref, *, n_ctx_blk, n_blk):
    i = pl.program_id(1)
    blk = ATT_BLOCK
    kv_refs = (kvp_ref, kvc_ref, kvn_ref, kvx_ref)
    q = q_ref[...].astype(F32)
    k_all = jnp.concatenate([ref[:, :KV_W] for ref in kv_refs], axis=0)
    v_all = jnp.concatenate([ref[:, KV_W:] for ref in kv_refs], axis=0)

    qi = lax.broadcasted_iota(jnp.int32, (blk, 3 * blk), 0)
    si = lax.broadcasted_iota(jnp.int32, (blk, 3 * blk), 1)
    rel = si - blk - qi
    key_blk = i - 1 + jnp.right_shift(si, LOG2_ATT_BLOCK)
    ok = (jnp.where(jnp.abs(rel) <= WINDOW, 1, 0) * jnp.where(key_blk >= n_ctx_blk, 1, 0)
          * jnp.where(key_blk < n_blk, 1, 0) * jnp.where(i >= n_ctx_blk, 1, 0))
    bias = jnp.where(ok > 0, 0.0, NEG_INF)

    lane_lo = lax.broadcasted_iota(jnp.int32, (1, LANES), 1) < HEAD_DIM
    stack = ATT_STACK
    outs = [None] * N_HEADS
    bias_g = jnp.concatenate([bias] * stack, axis=0)

    for h0 in range(0, N_HEADS, stack):
        kvh = h0 // (N_HEADS // N_KV)
        kv_slab = slice(LANES * (kvh // 2), LANES * (kvh // 2 + 1))
        kv_lo = kvh % 2 == 0
        rows = []
        for h in range(h0, h0 + stack):
            qp = q[:, LANES * (h // 2):LANES * (h // 2 + 1)]
            if (h % 2 == 0) != kv_lo:
                qp = pltpu.roll(qp, HEAD_DIM, 1)
            rows.append(jnp.where(lane_lo, qp, 0.0) if kv_lo else jnp.where(lane_lo, 0.0, qp))
        s = _dot_nt(jnp.concatenate(rows, axis=0), k_all[:, kv_slab])
        s = jnp.concatenate([s[:, :3 * blk] + bias_g, s[:, 3 * blk:]], axis=1)
        sink = jnp.concatenate([jnp.full((blk, 1), sink_ref[h], F32) for h in range(h0, h0 + stack)], axis=0)
        m = jnp.maximum(jnp.max(s, axis=-1, keepdims=True), sink)
        e = jnp.exp(s - m)
        denom = jnp.sum(e, axis=-1, keepdims=True) + jnp.exp(sink - m)
        o = _dot(e, v_all[:, kv_slab]) / denom
        for j, h in enumerate(range(h0, h0 + stack)):
            oh = o[j * blk:(j + 1) * blk]
            if (h % 2 == 0) != kv_lo:
                oh = pltpu.roll(oh, HEAD_DIM, 1)
            outs[h] = oh
    o_ref[...] = jnp.concatenate(
        [jnp.where(lane_lo, outs[2 * p], outs[2 * p + 1]) for p in range(N_PAIR)], axis=1).astype(o_ref.dtype)


def _attention(qkv, sink, *, ctx):
    bsz, total, _ = qkv.shape
    blk = ATT_BLOCK
    n_blk, n_ctx_blk = total // blk, ctx // blk
    assert B_V == B_K + KV_W and B_K % (2 * KV_W) == 0
    kv_col = B_K // (2 * KV_W)
    kern = functools.partial(_attn_kernel, n_ctx_blk=n_ctx_blk, n_blk=n_blk)
    prev_i = lambda i: jnp.maximum(i - 1, 0)
    next_i = lambda i: jnp.minimum(i + 1, n_blk - 1)
    same = lambda i: i

    def kv(row):
        return pl.BlockSpec((None, blk, 2 * KV_W), lambda b, i: (b, row(i), kv_col))

    return pl.pallas_call(
        kern,
        grid=(bsz, n_blk),
        in_specs=[pl.BlockSpec(memory_space=pltpu.SMEM),
                  pl.BlockSpec((None, blk, D_MODEL), lambda b, i: (b, i, B_Q // D_MODEL)),
                  kv(prev_i), kv(same), kv(next_i),
                  pl.BlockSpec((None, ctx, 2 * KV_W), lambda b, i: (b, 0, kv_col))],
        out_specs=pl.BlockSpec((None, blk, D_MODEL), lambda b, i: (b, i, 0)),
        out_shape=jax.ShapeDtypeStruct((bsz, total, D_MODEL), BF16),
        compiler_params=_cparams(("arbitrary", "arbitrary")),
    )(sink, qkv, qkv, qkv, qkv, qkv)


def _merge_kernel(x_ref, mod_ref, cv_ref, before_ref, after_ref, gate_lo_ref, gate_hi_ref, att_ref, yf_ref, yb_ref,
                  bonus_ref, g_ref,
                  cw_ref, lng_ref, lnb_ref, wa_ref, wb_ref, wc_ref, wo_ref, o_ref, *, tm, ctx, total):
    i = pl.program_id(0)
    bsz = x_ref.shape[0]
    d = D_MODEL
    cw = cw_ref[...]
    lng = lng_ref[...]
    lnb = lnb_ref[...]
    lane_lo = lax.broadcasted_iota(jnp.int32, (1, LANES), 1) < HEAD_DIM
    convs, rwkvs = [], []
    for b in range(bsz):
        cv = cv_ref[b].astype(F32)
        z = cv[:, d:2 * d] * cv[:, 2 * d:]
        before, after = _edge_rows(before_ref.at[b], after_ref.at[b])
        prev, nxt = _neighbours(z, before[:, d:2 * d] * before[:, 2 * d:], after[:, d:2 * d] * after[:, 2 * d:],
                                i, tm, ctx, total)
        convs.append((cv[:, :d] * (prev * cw[0:1] + z * cw[1:2] + nxt * cw[2:3])).astype(BF16))
        slabs = []
        for p in range(N_PAIR):
            sl = slice(LANES * p, LANES * (p + 1))
            y = yf_ref[b, p].astype(F32) + yb_ref[b, p].astype(F32)
            mean = _head_sums_exact(y, lane_lo) * (1.0 / HEAD_DIM)
            yc = y - mean
            var = _head_sums_exact(yc * yc, lane_lo) * (1.0 / HEAD_DIM)
            gn = yc * lax.rsqrt(var + GN_EPS) * lng[:, sl] + lnb[:, sl]
            slabs.append(((gn + bonus_ref[b, p]) * g_ref[b, p]).astype(BF16))
        rwkvs.append(jnp.concatenate(slabs, axis=1))

    rows = bsz * tm
    gates = jnp.concatenate([gate_lo_ref[...].reshape(rows, GATE_BLK), gate_hi_ref[...].reshape(rows, GATE_BLK)],
                            axis=1).astype(F32)
    m = (gates[:, :d] * _dot(jnp.concatenate(convs, axis=0), wa_ref[...])
         + gates[:, d:2 * d] * _dot(jnp.concatenate(rwkvs, axis=0), wb_ref[...])
         + gates[:, 2 * d:] * _dot(att_ref[...].reshape(rows, d), wc_ref[...]))
    out = _dot(m, wo_ref[...])
    is_ctx = _row_ids(i, tm) < ctx
    for b in range(bsz):
        o_ref[b] = x_ref[b] + _mod_row(mod_ref.at[b], 2, is_ctx) * out[b * tm:(b + 1) * tm]


def _merge(x, mod, proj_a, proj_b, att, y, bonus, g, conv_w, ln_g, ln_b, wa, wb, wc, wo, layer, *, ctx, tm):
    bsz, total, d = x.shape
    kern = functools.partial(_merge_kernel, tm=tm, ctx=ctx, total=total)
    per, nblk = tm // HALO_BF16, total // HALO_BF16
    col_cv, col_gate = 0, B_GATE // GATE_BLK
    before = pl.BlockSpec((bsz, HALO_BF16, 3 * d), lambda i: (0, jnp.maximum(i * per - 1, 0), col_cv))
    after = pl.BlockSpec((bsz, HALO_BF16, 3 * d), lambda i: (0, jnp.minimum((i + 1) * per, nblk - 1), col_cv))
    row = lambda w, col=0: pl.BlockSpec((bsz, tm, w), lambda i: (0, i, col))
    once = lambda shp: pl.BlockSpec(shp, lambda i: tuple(0 for _ in shp), pipeline_mode=pl.Buffered(1))
    pm = pl.BlockSpec((bsz, N_PAIR, tm, LANES), lambda i: (0, 0, i, 0))
    return pl.pallas_call(
        kern,
        grid=(total // tm,),
        in_specs=[row(d), pl.BlockSpec((bsz, 2, 6, d), lambda i: (0, 0, 0, 0)),
                  row(3 * d, col_cv), before, after, row(GATE_BLK, col_gate), row(GATE_BLK, col_gate + 1),
                  row(d), pm, pm, pm, pm,
                  once((3, d)), once((1, d)), once((1, d)),
                  _layer_weight((d, d), layer), _layer_weight((d, d), layer), _layer_weight((d, d), layer),
                  _layer_weight((d, d), layer)],
        out_specs=row(d),
        out_shape=jax.ShapeDtypeStruct((bsz, total, d), F32),
        compiler_params=_cparams(("arbitrary",)),
    )(x, mod, proj_a, proj_a, proj_a, proj_b, proj_b, att, y[0], y[1], bonus, g, conv_w, ln_g, ln_b,
      wa, wb, wc, wo)


def _ffn_kernel(x_ref, before_ref, after_ref, mod_ref, g_ref, wu_ref, cw_ref, wd_ref, fin_ref, o_ref, *,
                tm, ctx, total, tile0):
    i = pl.program_id(0) + tile0
    bsz = x_ref.shape[0]
    start = i * tm
    ext = tm + 2 * HALO
    is_ctx = (start - HALO + lax.broadcasted_iota(jnp.int32, (ext, 1), 0)) < ctx
    tile_is_ctx = _row_ids(i, tm) < ctx

    keep_before = jnp.where((start == 0) | (start == ctx), 0.0, 1.0)
    keep_after = jnp.where((start + tm == ctx) | (start + tm == total), 0.0, 1.0)

    def up(b):
        xe = jnp.concatenate([before_ref[b], x_ref[b], after_ref[b]], axis=0)
        y = xe * lax.rsqrt(jnp.mean(xe * xe, axis=-1, keepdims=True) + NORM_EPS) * g_ref[...]
        mod_b = mod_ref.at[b]
        h = (y * (1.0 + _mod_row(mod_b, 4, is_ctx)) + _mod_row(mod_b, 3, is_ctx)).astype(BF16)
        return _dot(h, wu_ref[:, :D_FF]), _dot(h, wu_ref[:, D_FF:])

    def conv(u, cw):
        prev, nxt = _tile_neighbours(u, tm, keep_before, keep_after)
        return prev * cw[0:1] + u[HALO:HALO + tm] * cw[1:2] + nxt * cw[2:3]

    u_next = up(0)
    for b in range(bsz):
        u_gate, u_val = u_next
        if b + 1 < bsz:
            u_next = up(b + 1)
        ug = conv(u_gate, cw_ref[:, :D_FF])
        uv = conv(u_val, cw_ref[:, D_FF:])
        out = x_ref[b] + _mod_row(mod_ref.at[b], 5, tile_is_ctx) * _dot(ug * _sigmoid(ug) * uv, wd_ref[...])
        if fin_ref is not None:
            out = out * lax.rsqrt(jnp.mean(out * out, axis=-1, keepdims=True) + NORM_EPS) * fin_ref[...]
        o_ref[b] = out


def _ffn_mid_kernel(x_ref, before_ref, after_ref, mod_ref, g_ref, wu_ref, cw_ref, wd_ref, o_ref, **kw):
    _ffn_kernel(x_ref, before_ref, after_ref, mod_ref, g_ref, wu_ref, cw_ref, wd_ref, None, o_ref, **kw)


def _ffn(x, mod, g, wu, conv_w, wd, layer, final_g, *, ctx, tm):
    bsz, total, d = x.shape
    tile0 = 0 if final_g is None else ctx // tm
    per, nblk = tm // HALO, total // HALO
    before = pl.BlockSpec((bsz, HALO, d), lambda i: (0, jnp.maximum((i + tile0) * per - 1, 0), 0))
    after = pl.BlockSpec((bsz, HALO, d), lambda i: (0, jnp.minimum((i + tile0 + 1) * per, nblk - 1), 0))
    row = pl.BlockSpec((bsz, tm, d), lambda i: (0, i + tile0, 0))
    once = lambda shp: pl.BlockSpec(shp, lambda i: tuple(0 for _ in shp), pipeline_mode=pl.Buffered(1))
    in_specs = [row, before, after, pl.BlockSpec((bsz, 2, 6, d), lambda i: (0, 0, 0, 0)),
                once((1, d)), _layer_weight((d, 2 * D_FF), layer), once((3, 2 * D_FF)),
                _layer_weight((D_FF, d), layer)]
    args = [x, x, x, mod, g, wu, conv_w, wd]
    kw = dict(tm=tm, ctx=ctx, total=total, tile0=tile0)
    if final_g is None:
        kern = functools.partial(_ffn_mid_kernel, **kw)
    else:
        kern = functools.partial(_ffn_kernel, **kw)
        in_specs.append(once((1, d)))
        args.append(final_g)
    out_spec = pl.BlockSpec((bsz, tm, d), lambda i: (0, i, 0))
    out_rows = total - tile0 * tm
    return pl.pallas_call(
        kern,
        grid=(total // tm - tile0,),
        in_specs=in_specs,
        out_specs=out_spec,
        out_shape=jax.ShapeDtypeStruct((bsz, out_rows, d), F32),
        compiler_params=_cparams(("arbitrary",)),
    )(*args)


def _rope_tables(ctx, seq):
    rows = seq // GRID_W
    t_row = jnp.broadcast_to(jnp.arange(rows)[:, None], (rows, GRID_W)).reshape(-1).astype(F32)
    t_col = jnp.broadcast_to(jnp.arange(GRID_W)[None, :], (rows, GRID_W)).reshape(-1).astype(F32)
    n_freq = HEAD_DIM // 4
    inv = ROPE_THETA ** (-jnp.arange(n_freq, dtype=F32) / n_freq)
    ar, ac = t_row[:, None] * inv, t_col[:, None] * inv
    cos = jnp.concatenate([jnp.cos(ar), jnp.cos(ar), jnp.cos(ac), jnp.cos(ac)], axis=1)
    sin = jnp.concatenate([-jnp.sin(ar), jnp.sin(ar), -jnp.sin(ac), jnp.sin(ac)], axis=1)
    cos = jnp.concatenate([jnp.ones((ctx, HEAD_DIM), F32), cos], axis=0)
    sin = jnp.concatenate([jnp.zeros((ctx, HEAD_DIM), F32), sin], axis=0)
    return jnp.concatenate([jnp.tile(cos, (1, LANES // HEAD_DIM)), jnp.tile(sin, (1, LANES // HEAD_DIM))], axis=1)


def _block_diag2(w):
    z = jnp.zeros_like(w[0])
    return jnp.concatenate([jnp.concatenate([w[0], z], axis=1), jnp.concatenate([z, w[1]], axis=1)], axis=0)


def kernel(x, c, ctx, c_ctx, ada_w, ada_b, norm1_g, w_in, conv_a_w, a_out_w, rwkv_mu, rwkv_w0, rwkv_w2,
           rwkv_a0, rwkv_a2, rwkv_g2, rwkv_k_k, rwkv_k_a, rwkv_r_k, rwkv_ln_g, rwkv_ln_b, rwkv_out_w,
           attn_sink, attn_out_w, w_o, norm2_g, ffn_up, ffn_conv, ffn_down, final_norm_g):
    bsz, seq, d = x.shape
    n_ctx = ctx.shape[1]
    depth = ada_w.shape[0]
    total = n_ctx + seq
    assert d == D_MODEL and seq % ATT_BLOCK == 0 and n_ctx % ATT_BLOCK == 0 and bsz + 1 <= HALO
    tm = 256
    assert total % tm == 0 and n_ctx % tm == 0

    cvec = jnp.zeros((HALO, d), F32).at[:bsz].set(c).at[bsz].set(c_ctx)
    mods = _ada(cvec, ada_w.astype(BF16), ada_b)
    mods = mods.reshape(depth, HALO, 6, d)
    mods = jnp.stack([jnp.broadcast_to(mods[:, bsz:bsz + 1], (depth, bsz, 6, d)), mods[:, :bsz]], axis=2)

    rope_tab = _rope_tables(n_ctx, seq)
    tm_proj = tm // 2
    masks = _scan_masks(tm_proj)
    head_ones = masks[2].astype(BF16)
    xa = jnp.concatenate([ctx, x], axis=1)
    w_in, a_out_w, rwkv_out_w, attn_out_w, w_o, ffn_up, ffn_down = (
        w.astype(BF16) for w in (w_in, a_out_w, rwkv_out_w, attn_out_w, w_o, ffn_up, ffn_down))

    for l in range(depth):
        mod = mods[l]
        proj_a, proj_b, at, rt, bt, kt, gend, v, bonus, g = _proj(
            xa, mod, norm1_g[l][None], w_in, l, rope_tab, masks[0], head_ones, rwkv_mu[l],
            rwkv_w0[l].reshape(1, 2 * d), _block_diag2(rwkv_w2[l]).astype(BF16),
            rwkv_a0[l].reshape(1, 2 * d), _block_diag2(rwkv_a2[l]).astype(BF16), rwkv_g2[l].astype(BF16),
            rwkv_k_k[l][None], rwkv_k_a[l][None], rwkv_r_k[l].reshape(1, d), ctx=n_ctx, tm=tm_proj)
        y = [_scan(at, rt, bt, kt, gend, v, masks, ctx=n_ctx, direction=dr) for dr in range(2)]
        att = _attention(proj_b, attn_sink[l], ctx=n_ctx)
        xa = _merge(xa, mod, proj_a, proj_b, att, y, bonus, g, conv_a_w[l], rwkv_ln_g[l][None], rwkv_ln_b[l][None],
                    a_out_w, rwkv_out_w, attn_out_w, w_o, l, ctx=n_ctx, tm=tm)
        xa = _ffn(xa, mod, norm2_g[l][None], ffn_up, ffn_conv[l], ffn_down, l,
                  final_norm_g[None] if l == depth - 1 else None, ctx=n_ctx, tm=tm)
    return xa
```

```python
import functools

import jax
import jax.numpy as jnp
import numpy as np
from jax import lax
from jax.experimental import pallas as pl
from jax.experimental.pallas import tpu as pltpu

D_MODEL = 1024
GRID_W = 64
N_HEADS = 16
HEAD_DIM = 64
N_KV = 4
R_LORA = 64
R_GATE = 128
GN_EPS = 64e-5
WINDOW = 128
ATT_BLOCK = 128
LOG2_ATT_BLOCK = 7
ATT_STACK = N_HEADS // N_KV
ATT_SCALE = HEAD_DIM ** -0.5
ROPE_THETA = 10000.0
NEG_INF = -1e30
D_FF = 2816
NORM_EPS = 1e-6
KK_EPS = 1e-12
DECAY_SCALE = float(np.exp(-0.5))

LANES = 128
HALO = 8
HALO_BF16 = 16
N_PAIR = D_MODEL // LANES
CHUNK = 64
N_LEVELS = 5
VMEM_LIMIT = 56 * 1024 * 1024

BF16 = jnp.bfloat16
F32 = jnp.float32

KV_W = N_KV * HEAD_DIM
LORA_W = 2 * R_LORA + 2 * R_LORA + R_GATE
W_CONV, W_RKV, W_LORA = 0, 3 * D_MODEL, 6 * D_MODEL
W_Q = W_LORA + LORA_W
W_GATE = W_Q + D_MODEL + 2 * KV_W
N_PROJ_B = D_MODEL + 2 * KV_W + 3 * D_MODEL
B_Q, B_K, B_V, B_GATE = 0, D_MODEL, D_MODEL + KV_W, D_MODEL + 2 * KV_W
GATE_BLK = B_GATE
assert B_K % KV_W == 0 and 2 * GATE_BLK == 3 * D_MODEL and GATE_BLK % LANES == 0


def _cparams(sem):
    return pltpu.CompilerParams(dimension_semantics=sem, vmem_limit_bytes=VMEM_LIMIT)


def _dot(a, b):
    return jnp.dot(a.astype(BF16), b.astype(BF16), preferred_element_type=F32)


def _dot_nt(a, b):
    return lax.dot_general(a.astype(BF16), b.astype(BF16), (((1,), (1,)), ((), ())),
                           preferred_element_type=F32)


def _bmm(a, b):
    return jnp.einsum('pmk,pkn->pmn', a.astype(BF16), b.astype(BF16), preferred_element_type=F32)


def _bmm_nt(a, b):
    return jnp.einsum('pmk,pnk->pmn', a.astype(BF16), b.astype(BF16), preferred_element_type=F32)


def _bmm_tn(a, b):
    return jnp.einsum('pkm,pkn->pmn', a.astype(BF16), b.astype(BF16), preferred_element_type=F32)


def _sigmoid(x):
    return 0.5 + 0.5 * jnp.tanh(0.5 * x)


def _row_ids(i, tm):
    return i * tm + lax.broadcasted_iota(jnp.int32, (tm, 1), 0)


def _mod_row(mod_ref, k, is_ctx):
    return jnp.where(is_ctx, mod_ref[0, k:k + 1, :], mod_ref[1, k:k + 1, :])


def _neighbours(z, before, after, i, tm, ctx, total):
    start = i * tm
    keep_before = jnp.where((start == 0) | (start == ctx), 0.0, 1.0)
    keep_after = jnp.where((start + tm == ctx) | (start + tm == total), 0.0, 1.0)
    row = lax.broadcasted_iota(jnp.int32, (HALO, 1), 0)
    prev = pltpu.roll(z, 1, 0)
    nxt = pltpu.roll(z, tm - 1, 0)
    top = jnp.where(row == 0, before * keep_before, prev[:HALO])
    bottom = jnp.where(row == HALO - 1, after * keep_after, nxt[tm - HALO:])
    return (jnp.concatenate([top, prev[HALO:]], axis=0),
            jnp.concatenate([nxt[:tm - HALO], bottom], axis=0))


def _layer_weight(shape, layer):
    return pl.BlockSpec((None,) + shape, lambda *_: (layer,) + tuple(0 for _ in shape),
                        pipeline_mode=pl.Buffered(1))


def _halo_specs(bsz, rows, width, tm, total, tile0=0):
    per = tm // rows
    nblk = total // rows
    before = pl.BlockSpec((bsz, rows, width), lambda i: (0, jnp.maximum((i + tile0) * per - 1, 0), 0))
    after = pl.BlockSpec((bsz, rows, width), lambda i: (0, jnp.minimum((i + tile0 + 1) * per, nblk - 1), 0))
    return before, after


def _edge_rows(before_ref, after_ref):
    before = before_ref[...].astype(F32)
    return before[before.shape[0] - 1:], after_ref[...].astype(F32)[0:1]


def _ada_kernel(c_ref, w_ref, b_ref, o_ref):
    c = c_ref[...]
    o_ref[...] = _dot(c * _sigmoid(c), w_ref[...]) + b_ref[...]


def _ada(cvec, ada_w, ada_b):
    depth, d, n = ada_w.shape
    tn = 1536
    return pl.pallas_call(
        _ada_kernel,
        grid=(depth, n // tn),
        in_specs=[pl.BlockSpec((HALO, d), lambda l, j: (0, 0)),
                  pl.BlockSpec((None, d, tn), lambda l, j: (l, 0, j)),
                  pl.BlockSpec((None, 1, tn), lambda l, j: (l, 0, j))],
        out_specs=pl.BlockSpec((None, HALO, tn), lambda l, j: (l, 0, j)),
        out_shape=jax.ShapeDtypeStruct((depth, HALO, n), F32),
        compiler_params=_cparams(("arbitrary", "arbitrary")),
    )(cvec, ada_w, ada_b.reshape(depth, 1, n))


def _rope(z, tab):
    cos, sin = tab[:, :LANES], tab[:, LANES:]
    lane = lax.broadcasted_iota(jnp.int32, (1, LANES), 1)
    first = jnp.bitwise_and(lane, HEAD_DIM // 2 - 1) < (HEAD_DIM // 4)
    out = []
    for p in range(z.shape[1] // LANES):
        zp = z[:, LANES * p:LANES * (p + 1)]
        partner = jnp.where(first, pltpu.roll(zp, LANES - HEAD_DIM // 4, 1), pltpu.roll(zp, HEAD_DIM // 4, 1))
        out.append(zp * cos + partner * sin)
    return jnp.concatenate(out, axis=1)


def _tile_neighbours(u, tm, keep_before, keep_after):
    ext = tm + 2 * HALO
    row = lax.broadcasted_iota(jnp.int32, (HALO, 1), 0)
    prev = pltpu.roll(u, 1, 0)[HALO:HALO + tm]
    nxt = pltpu.roll(u, ext - 1, 0)[HALO:HALO + tm]
    top = jnp.where(row == 0, prev[:HALO] * keep_before, prev[:HALO])
    bottom = jnp.where(row == HALO - 1, nxt[tm - HALO:] * keep_after, nxt[tm - HALO:])
    return (jnp.concatenate([top, prev[HALO:]], axis=0), jnp.concatenate([nxt[:tm - HALO], bottom], axis=0))


def _proj_kernel(x_ref, before_ref, after_ref, mod_ref, g_ref, w_ref, tab_ref,
                 tri_ref, ones_ref, mu_ref, w0_ref, w2_ref, a0_ref, a2_ref, g2_ref, kk_ref, ka_ref, rk_ref,
                 cv_out, ob_ref, at_out, rt_out, bt_out, kt_out, gend_out, v_out, bonus_out, gg_out, *,
                 tm, ctx, total):
    i = pl.program_id(0)
    bsz = x_ref.shape[0]
    d = D_MODEL
    start = i * tm
    ext = tm + 2 * HALO
    is_ctx = (start - HALO + lax.broadcasted_iota(jnp.int32, (ext, 1), 0)) < ctx
    h_ext = []
    for b in range(bsz):
        xe = jnp.concatenate([before_ref[b], x_ref[b], after_ref[b]], axis=0)
        y = xe * lax.rsqrt(jnp.mean(xe * xe, axis=-1, keepdims=True) + NORM_EPS) * g_ref[...]
        mod_b = mod_ref.at[b]
        h_ext.append((y * (1.0 + _mod_row(mod_b, 1, is_ctx)) + _mod_row(mod_b, 0, is_ctx)).astype(BF16))
    h = jnp.concatenate([he[HALO:HALO + tm] for he in h_ext], axis=0)

    rkv = _dot(jnp.concatenate(h_ext, axis=0), w_ref[:, W_RKV:W_RKV + 3 * d])
    lora = _dot(h, w_ref[:, W_LORA:W_LORA + LORA_W])
    tab = tab_ref[...]
    keep_before = jnp.where((start == 0) | (start == ctx), 0.0, 1.0)
    keep_after = jnp.where((start + tm == ctx) | (start + tm == total), 0.0, 1.0)

    def tokens(b):
        rkv_b = rkv[b * ext:(b + 1) * ext]
        prev, nxt = _tile_neighbours(rkv_b, tm, keep_before, keep_after)
        _rwkv_tokens(
            prev, rkv_b[HALO:HALO + tm], nxt, lora[b * tm:(b + 1) * tm],
            tri_ref, ones_ref, mu_ref, w0_ref, w2_ref, a0_ref, a2_ref, g2_ref, kk_ref, ka_ref, rk_ref,
            at_out.at[:, b], rt_out.at[:, b], bt_out.at[:, b], kt_out.at[:, b], gend_out.at[:, b],
            v_out.at[b], bonus_out.at[b], gg_out.at[b], tm)

    qkv = _dot(h, w_ref[:, W_Q:W_GATE])
    for b in range(bsz):
        rows = slice(b * tm, (b + 1) * tm)
        rot = _rope(qkv[rows, :B_V - B_Q], tab)
        ob_ref[b, :, B_Q:B_K] = (rot[:, :B_K - B_Q] * ATT_SCALE).astype(ob_ref.dtype)
        ob_ref[b, :, B_K:B_V] = rot[:, B_K - B_Q:].astype(ob_ref.dtype)
        ob_ref[b, :, B_V:B_GATE] = qkv[rows, B_V - B_Q:].astype(ob_ref.dtype)
    gates = _sigmoid(_dot(h, w_ref[:, W_GATE:]))
    ob_ref[:, :, B_GATE:] = gates.reshape(bsz, tm, 3 * d).astype(ob_ref.dtype)
    cv_out[...] = _dot(h, w_ref[:, W_CONV:W_CONV + 3 * d]).reshape(bsz, tm, 3 * d).astype(cv_out.dtype)
    for b in range(bsz):
        tokens(b)


def _proj(x, mod, g, w, layer, rope_tab, tri, head_ones, mu, w0, w2bd, a0, a2bd, g2, k_k, k_a, r_k, *, ctx, tm):
    bsz, total, d = x.shape
    assert w.shape[2] == W_GATE + 3 * d
    kern = functools.partial(_proj_kernel, tm=tm, ctx=ctx, total=total)
    before, after = _halo_specs(bsz, HALO, d, tm, total)
    once = lambda shp: pl.BlockSpec(shp, lambda i: tuple(0 for _ in shp), pipeline_mode=pl.Buffered(1))
    row = lambda n: pl.BlockSpec((bsz, tm, n), lambda i: (0, i, 0))
    pm = pl.BlockSpec((bsz, N_PAIR, tm, LANES), lambda i: (0, 0, i, 0))
    pm2 = pl.BlockSpec((2, bsz, N_PAIR, tm, LANES), lambda i: (0, 0, 0, i, 0))
    ge = pl.BlockSpec((2, bsz, tm // CHUNK, N_PAIR, LANES), lambda i: (0, 0, i, 0, 0))
    sds = jax.ShapeDtypeStruct((bsz, N_PAIR, total, LANES), BF16)
    sds2 = jax.ShapeDtypeStruct((2, bsz, N_PAIR, total, LANES), BF16)
    sds_ge = jax.ShapeDtypeStruct((2, bsz, total // CHUNK, N_PAIR, LANES), F32)
    return pl.pallas_call(
        kern,
        grid=(total // tm,),
        in_specs=[row(d), before, after, pl.BlockSpec((bsz, 2, 6, d), lambda i: (0, 0, 0, 0)),
                  once((1, d)), _layer_weight((d, W_GATE + 3 * d), layer),
                  pl.BlockSpec((tm, 2 * LANES), lambda i: (i, 0)),
                  once((2, tm, tm)), once((LANES, LANES)), once((2, 3 * d)), once((1, 2 * d)), once((LANES, 2 * d)),
                  once((1, 2 * d)), once((LANES, 2 * d)), once((R_GATE, d)), once((1, d)), once((1, d)),
                  once((1, d))],
        out_specs=[row(3 * d), row(N_PROJ_B), pm2, pm2, pm2, pm2, ge, pm, pm, pm],
        out_shape=[jax.ShapeDtypeStruct((bsz, total, 3 * d), BF16),
                   jax.ShapeDtypeStruct((bsz, total, N_PROJ_B), BF16),
                   sds2, sds2, sds2, sds2, sds_ge, sds, sds, sds],
        compiler_params=_cparams(("arbitrary",)),
    )(x, x, x, mod, g, w, rope_tab, tri, head_ones, mu, w0, w2bd, a0, a2bd, g2, k_k, k_a, r_k)


def _head_sums(z, head_ones):
    return jnp.dot(z.astype(BF16), head_ones, preferred_element_type=F32)


def _head_sums_exact(z, lane_lo):
    s_lo = jnp.sum(jnp.where(lane_lo, z, 0.0), axis=-1, keepdims=True)
    s_hi = jnp.sum(jnp.where(lane_lo, 0.0, z), axis=-1, keepdims=True)
    return jnp.where(lane_lo, s_lo, s_hi)


def _chunk_cumsum(tri, z):
    hi = z.astype(BF16)
    low = (z - hi.astype(F32)).astype(BF16)
    dot = functools.partial(jnp.dot, preferred_element_type=F32)
    return dot(tri, low) + dot(tri, hi)


def _rwkv_tokens(prev, x, nxt, lo, tri_ref, ones_ref, mu_ref, w0_ref, w2_ref, a0_ref, a2_ref, g2_ref, kk_ref,
                 ka_ref, rk_ref, at_out, rt_out, bt_out, kt_out, gend_out, v_out, bonus_out, g_out, tm):
    mu_prev = mu_ref[0:1, :]
    mu_next = mu_ref[1:2, :]
    s = prev * mu_prev + x * (1.0 - mu_prev - mu_next) + nxt * mu_next
    d = D_MODEL
    r, k, v = s[:, :d], s[:, d:2 * d], s[:, 2 * d:]

    w_pre = w0_ref[...] + _dot(jnp.tanh(lo[:, :LANES]), w2_ref[...])
    lw = -DECAY_SCALE * _sigmoid(w_pre)
    a = _sigmoid(a0_ref[...] + _dot(lo[:, LANES:2 * LANES], a2_ref[...]))
    g = _dot(_sigmoid(lo[:, 2 * LANES:]), g2_ref[...])
    kk = k * kk_ref[...]
    ka = ka_ref[...]
    rk = rk_ref[...]
    cl = [_chunk_cumsum(tri_ref[dr], lw[:, d * dr:d * (dr + 1)]) for dr in range(2)]

    head_ones = ones_ref[...]
    for p in range(N_PAIR):
        sl = slice(LANES * p, LANES * (p + 1))
        kkp = kk[:, sl]
        kn = kkp * lax.rsqrt(_head_sums(kkp * kkp, head_ones) + KK_EPS)
        rp, kp, vp = r[:, sl], k[:, sl], v[:, sl]
        bonus = jnp.zeros_like(rp)
        for dr in range(2):
            dsl = slice(d * dr + LANES * p, d * dr + LANES * (p + 1))
            ad = a[:, dsl]
            kd = kp * (1.0 + (ad - 1.0) * ka[:, sl])
            bonus = bonus + _head_sums(rp * kd * rk[:, sl], head_ones) * vp
            cl_incl = cl[dr][:, sl]
            inv_g = jnp.exp(-cl_incl)
            at_out[dr, p] = (-kn * jnp.exp(cl_incl - lw[:, dsl])).astype(BF16)
            rt_out[dr, p] = (rp * jnp.exp(cl_incl)).astype(BF16)
            bt_out[dr, p] = (kn * ad * inv_g).astype(BF16)
            kt_out[dr, p] = (kd * inv_g).astype(BF16)
            for q in range(tm // CHUNK):
                end = CHUNK * q + (CHUNK - 1 if dr == 0 else 0)
                gend_out[dr, q, p:p + 1, :] = jnp.exp(cl_incl[end:end + 1, :])
        v_out[p] = vp.astype(BF16)
        bonus_out[p] = bonus.astype(BF16)
        g_out[p] = g[:, sl].astype(BF16)


M_STRICT, M_INCL, M_EYE, M_LEVEL0 = 0, 1, 2, 3
N_MASKS = M_LEVEL0 + 1 + N_LEVELS


def _scan_masks(tm):
    c = CHUNK
    t = np.arange(c)[:, None]
    s = np.arange(c)[None, :]
    cum = np.zeros((2, tm, tm), np.float32)
    msk = np.zeros((2, N_MASKS, c, 2 * c), np.float32)
    for dr in range(2):
        earlier = (s < t) if dr == 0 else (s > t)
        cum[dr] = np.kron(np.eye(tm // c), earlier | (s == t))
        planes = [earlier, earlier | (s == t), s == t]
        m = 1
        while m < c:
            same = (t // (2 * m)) == (s // (2 * m))
            t_late = (t % (2 * m)) >= m
            s_late = (s % (2 * m)) >= m
            planes.append(same & (t_late & ~s_late if dr == 0 else ~t_late & s_late))
            m *= 2
        for q, plane in enumerate(planes):
            msk[dr, q] = np.concatenate([plane, plane], axis=1)
    bdm = np.kron(np.eye(2, dtype=np.float32), np.ones((c, c), np.float32))
    return jnp.asarray(cum, BF16), jnp.asarray(msk), jnp.asarray(bdm)


SCAN_GROUP = 4


def _scan_kernel(msk_ref, bdm_ref, at_ref, rt_ref, bt_ref, kt_ref, gend_ref, v_ref, y_ref, s_ref, *, reverse):
    c = CHUNK
    bsz = v_ref.shape[0]
    n = bsz * N_PAIR
    order = tuple(reversed(range(SCAN_GROUP))) if reverse else tuple(range(SCAN_GROUP))

    @pl.when(pl.program_id(0) == 0)
    def _():
        s_ref[...] = jnp.zeros_like(s_ref)

    bdm = bdm_ref[...]
    bdm_bf = bdm.astype(BF16)
    m_strict = msk_ref[M_STRICT]
    m_incl = msk_ref[M_INCL]

    def load(ref):
        z = ref[...].reshape(n, SCAN_GROUP * c, LANES)
        return jnp.concatenate([z[:, c * k:c * (k + 1)] for k in range(SCAN_GROUP)], axis=0)

    def stack(z):
        z = z.astype(BF16)
        return jnp.concatenate([z, z], axis=1) * bdm_bf

    bt, kt, v = load(bt_ref), load(kt_ref), load(v_ref)
    ar = jnp.concatenate([load(at_ref), load(rt_ref)], axis=1)
    gram = _bmm_nt(ar, jnp.concatenate([stack(bt), stack(kt)], axis=1))
    g_ab = gram[:, :c, :2 * c]
    from_v = jnp.concatenate([gram[:, :c, 2 * c:] * m_strict, gram[:, c:, 2 * c:] * m_incl], axis=1)
    from_v = _bmm(from_v, stack(v))
    y_from_u = gram[:, c:, :2 * c] * m_incl

    tri = msk_ref[M_EYE] + g_ab * msk_ref[M_LEVEL0]
    for lvl in range(1, N_LEVELS + 1):
        w = _bmm(g_ab * msk_ref[M_LEVEL0 + lvl], stack(tri))
        tri = tri + _bmm(tri, stack(w))

    state = s_ref[...]
    ys = [None] * SCAN_GROUP
    for k in order:
        sel = slice(n * k, n * (k + 1))
        g_end = jnp.stack([gend_ref[q // N_PAIR, k, q % N_PAIR:q % N_PAIR + 1, :] for q in range(n)], axis=0)
        partial = _bmm_nt(ar[sel], state) + from_v[sel]
        u = _bmm(tri[sel], stack(partial[:, :c]))
        ys[k] = partial[:, c:] + _bmm(y_from_u[sel], stack(u))
        upd = _bmm_tn(jnp.concatenate([u.astype(BF16), v[sel]], axis=1), jnp.concatenate([bt[sel], kt[sel]], axis=1))
        state = (state + upd * bdm) * g_end
    s_ref[...] = state
    y_ref[...] = jnp.concatenate(ys, axis=1).reshape(bsz, N_PAIR, SCAN_GROUP * c, LANES).astype(y_ref.dtype)


def _scan(at, rt, bt, kt, gend, v, masks, *, ctx, direction):
    bsz, _, total, _ = v.shape
    rows = SCAN_GROUP * CHUNK
    steps, ctx_steps = total // rows, ctx // rows
    assert steps * rows == total and ctx_steps * rows == ctx
    _, msk, bdm = masks

    def blk(j):
        return j if direction == 0 else jnp.where(j < ctx_steps, ctx_steps - 1 - j, steps + ctx_steps - 1 - j)

    shared = pl.BlockSpec((bsz, N_PAIR, rows, LANES), lambda j: (0, 0, blk(j), 0))
    per_dir = pl.BlockSpec((None, bsz, N_PAIR, rows, LANES), lambda j: (direction, 0, 0, blk(j), 0))
    ends = pl.BlockSpec((None, bsz, SCAN_GROUP, N_PAIR, LANES), lambda j: (direction, 0, blk(j), 0, 0))
    return pl.pallas_call(
        functools.partial(_scan_kernel, reverse=direction == 1),
        grid=(steps,),
        in_specs=[pl.BlockSpec((None, N_MASKS, CHUNK, 2 * CHUNK), lambda j: (direction, 0, 0, 0)),
                  pl.BlockSpec((2 * CHUNK, 2 * CHUNK), lambda j: (0, 0)),
                  per_dir, per_dir, per_dir, per_dir, ends, shared],
        out_specs=shared,
        out_shape=jax.ShapeDtypeStruct((bsz, N_PAIR, total, LANES), BF16),
        scratch_shapes=[pltpu.VMEM((bsz * N_PAIR, 2 * HEAD_DIM, 2 * HEAD_DIM), F32)],
        compiler_params=_cparams(("arbitrary",)),
    )(msk, bdm, at, rt, bt, kt, gend, v)


def _attn_kernel(sink_ref, q_ref, kvp_ref, kvc_ref, kvn_ref, kvx_ref, o_ref, *, n_ctx_blk, n_blk):
    i = pl.program_id(1)
    blk = ATT_BLOCK
    kv_refs = (kvp_ref, kvc_ref, kvn_ref, kvx_ref)
    q = q_ref[...].astype(F32)
    k_all = jnp.concatenate([ref[:, :KV_W] for ref in kv_refs], axis=0)
    v_all = jnp.concatenate([ref[:, KV_W:] for ref in kv_refs], axis=0)

    qi = lax.broadcasted_iota(jnp.int32, (blk, 3 * blk), 0)
    si = lax.broadcasted_iota(jnp.int32, (blk, 3 * blk), 1)
    rel = si - blk - qi
    key_blk = i - 1 + jnp.right_shift(si, LOG2_ATT_BLOCK)
    ok = (jnp.where(jnp.abs(rel) <= WINDOW, 1, 0) * jnp.where(key_blk >= n_ctx_blk, 1, 0)
          * jnp.where(key_blk < n_blk, 1, 0) * jnp.where(i >= n_ctx_blk, 1, 0))
    bias = jnp.where(ok > 0, 0.0, NEG_INF)

    lane_lo = lax.broadcasted_iota(jnp.int32, (1, LANES), 1) < HEAD_DIM
    stack = ATT_STACK
    outs = [None] * N_HEADS
    bias_g = jnp.concatenate([bias] * stack, axis=0)

    for h0 in range(0, N_HEADS, stack):
        kvh = h0 // (N_HEADS // N_KV)
        kv_slab = slice(LANES * (kvh // 2), LANES * (kvh // 2 + 1))
        kv_lo = kvh % 2 == 0
        rows = []
        for h in range(h0, h0 + stack):
            qp = q[:, LANES * (h // 2):LANES * (h // 2 + 1)]
            if (h % 2 == 0) != kv_lo:
                qp = pltpu.roll(qp, HEAD_DIM, 1)
            rows.append(jnp.where(lane_lo, qp, 0.0) if kv_lo else jnp.where(lane_lo, 0.0, qp))
        s = _dot_nt(jnp.concatenate(rows, axis=0), k_all[:, kv_slab])
        s = jnp.concatenate([s[:, :3 * blk] + bias_g, s[:, 3 * blk:]], axis=1)
        sink = jnp.concatenate([jnp.full((blk, 1), sink_ref[h], F32) for h in range(h0, h0 + stack)], axis=0)
        m = jnp.maximum(jnp.max(s, axis=-1, keepdims=True), sink)
        e = jnp.exp(s - m)
        denom = jnp.sum(e, axis=-1, keepdims=True) + jnp.exp(sink - m)
        o = _dot(e, v_all[:, kv_slab]) / denom
        for j, h in enumerate(range(h0, h0 + stack)):
            oh = o[j * blk:(j + 1) * blk]
            if (h % 2 == 0) != kv_lo:
                oh = pltpu.roll(oh, HEAD_DIM, 1)
            outs[h] = oh
    o_ref[...] = jnp.concatenate(
        [jnp.where(lane_lo, outs[2 * p], outs[2 * p + 1]) for p in range(N_PAIR)], axis=1).astype(o_ref.dtype)


def _attention(qkv, sink, *, ctx):
    bsz, total, _ = qkv.shape
    blk = ATT_BLOCK
    n_blk, n_ctx_blk = total // blk, ctx // blk
    assert B_V == B_K + KV_W and B_K % (2 * KV_W) == 0
    kv_col = B_K // (2 * KV_W)
    kern = functools.partial(_attn_kernel, n_ctx_blk=n_ctx_blk, n_blk=n_blk)
    prev_i = lambda i: jnp.maximum(i - 1, 0)
    next_i = lambda i: jnp.minimum(i + 1, n_blk - 1)
    same = lambda i: i

    def kv(row):
        return pl.BlockSpec((None, blk, 2 * KV_W), lambda b, i: (b, row(i), kv_col))

    return pl.pallas_call(
        kern,
        grid=(bsz, n_blk),
        in_specs=[pl.BlockSpec(memory_space=pltpu.SMEM),
                  pl.BlockSpec((None, blk, D_MODEL), lambda b, i: (b, i, B_Q // D_MODEL)),
                  kv(prev_i), kv(same), kv(next_i),
                  pl.BlockSpec((None, ctx, 2 * KV_W), lambda b, i: (b, 0, kv_col))],
        out_specs=pl.BlockSpec((None, blk, D_MODEL), lambda b, i: (b, i, 0)),
        out_shape=jax.ShapeDtypeStruct((bsz, total, D_MODEL), BF16),
        compiler_params=_cparams(("arbitrary", "arbitrary")),
    )(sink, qkv, qkv, qkv, qkv, qkv)


def _merge_kernel(x_ref, mod_ref, cv_ref, before_ref, after_ref, gate_lo_ref, gate_hi_ref, att_ref, yf_ref, yb_ref,
                  bonus_ref, g_ref,
                  cw_ref, lng_ref, lnb_ref, wa_ref, wb_ref, wc_ref, wo_ref, o_ref, *, tm, ctx, total):
    i = pl.program_id(0)
    bsz = x_ref.shape[0]
    d = D_MODEL
    cw = cw_ref[...]
    lng = lng_ref[...]
    lnb = lnb_ref[...]
    lane_lo = lax.broadcasted_iota(jnp.int32, (1, LANES), 1) < HEAD_DIM
    convs, rwkvs = [], []
    for b in range(bsz):
        cv = cv_ref[b].astype(F32)
        z = cv[:, d:2 * d] * cv[:, 2 * d:]
        before, after = _edge_rows(before_ref.at[b], after_ref.at[b])
        prev, nxt = _neighbours(z, before[:, d:2 * d] * before[:, 2 * d:], after[:, d:2 * d] * after[:, 2 * d:],
                                i, tm, ctx, total)
        convs.append((cv[:, :d] * (prev * cw[0:1] + z * cw[1:2] + nxt * cw[2:3])).astype(BF16))
        slabs = []
        for p in range(N_PAIR):
            sl = slice(LANES * p, LANES * (p + 1))
            y = yf_ref[b, p].astype(F32) + yb_ref[b, p].astype(F32)
            mean = _head_sums_exact(y, lane_lo) * (1.0 / HEAD_DIM)
            yc = y - mean
            var = _head_sums_exact(yc * yc, lane_lo) * (1.0 / HEAD_DIM)
            gn = yc * lax.rsqrt(var + GN_EPS) * lng[:, sl] + lnb[:, sl]
            slabs.append(((gn + bonus_ref[b, p]) * g_ref[b, p]).astype(BF16))
        rwkvs.append(jnp.concatenate(slabs, axis=1))

    rows = bsz * tm
    gates = jnp.concatenate([gate_lo_ref[...].reshape(rows, GATE_BLK), gate_hi_ref[...].reshape(rows, GATE_BLK)],
                            axis=1).astype(F32)
    m = (gates[:, :d] * _dot(jnp.concatenate(convs, axis=0), wa_ref[...])
         + gates[:, d:2 * d] * _dot(jnp.concatenate(rwkvs, axis=0), wb_ref[...])
         + gates[:, 2 * d:] * _dot(att_ref[...].reshape(rows, d), wc_ref[...]))
    out = _dot(m, wo_ref[...])
    is_ctx = _row_ids(i, tm) < ctx
    for b in range(bsz):
        o_ref[b] = x_ref[b] + _mod_row(mod_ref.at[b], 2, is_ctx) * out[b * tm:(b + 1) * tm]


def _merge(x, mod, proj_a, proj_b, att, y, bonus, g, conv_w, ln_g, ln_b, wa, wb, wc, wo, layer, *, ctx, tm):
    bsz, total, d = x.shape
    kern = functools.partial(_merge_kernel, tm=tm, ctx=ctx, total=total)
    col_cv, col_gate = 0, B_GATE // GATE_BLK
    before, after = _halo_specs(bsz, HALO_BF16, 3 * d, tm, total)
    row = lambda w, col=0: pl.BlockSpec((bsz, tm, w), lambda i: (0, i, col))
    once = lambda shp: pl.BlockSpec(shp, lambda i: tuple(0 for _ in shp), pipeline_mode=pl.Buffered(1))
    pm = pl.BlockSpec((bsz, N_PAIR, tm, LANES), lambda i: (0, 0, i, 0))
    return pl.pallas_call(
        kern,
        grid=(total // tm,),
        in_specs=[row(d), pl.BlockSpec((bsz, 2, 6, d), lambda i: (0, 0, 0, 0)),
                  row(3 * d, col_cv), before, after, row(GATE_BLK, col_gate), row(GATE_BLK, col_gate + 1),
                  row(d), pm, pm, pm, pm,
                  once((3, d)), once((1, d)), once((1, d)),
                  _layer_weight((d, d), layer), _layer_weight((d, d), layer), _layer_weight((d, d), layer),
                  _layer_weight((d, d), layer)],
        out_specs=row(d),
        out_shape=jax.ShapeDtypeStruct((bsz, total, d), F32),
        compiler_params=_cparams(("arbitrary",)),
    )(x, mod, proj_a, proj_a, proj_a, proj_b, proj_b, att, y[0], y[1], bonus, g, conv_w, ln_g, ln_b,
      wa, wb, wc, wo)


def _ffn_kernel(x_ref, before_ref, after_ref, mod_ref, g_ref, wu_ref, cw_ref, wd_ref, fin_ref, o_ref, *,
                tm, ctx, total, tile0):
    i = pl.program_id(0) + tile0
    bsz = x_ref.shape[0]
    start = i * tm
    ext = tm + 2 * HALO
    is_ctx = (start - HALO + lax.broadcasted_iota(jnp.int32, (ext, 1), 0)) < ctx
    tile_is_ctx = _row_ids(i, tm) < ctx

    keep_before = jnp.where((start == 0) | (start == ctx), 0.0, 1.0)
    keep_after = jnp.where((start + tm == ctx) | (start + tm == total), 0.0, 1.0)

    def up(b):
        xe = jnp.concatenate([before_ref[b], x_ref[b], after_ref[b]], axis=0)
        y = xe * lax.rsqrt(jnp.mean(xe * xe, axis=-1, keepdims=True) + NORM_EPS) * g_ref[...]
        mod_b = mod_ref.at[b]
        h = (y * (1.0 + _mod_row(mod_b, 4, is_ctx)) + _mod_row(mod_b, 3, is_ctx)).astype(BF16)
        return _dot(h, wu_ref[:, :D_FF]), _dot(h, wu_ref[:, D_FF:])

    def conv(u, cw):
        prev, nxt = _tile_neighbours(u, tm, keep_before, keep_after)
        return prev * cw[0:1] + u[HALO:HALO + tm] * cw[1:2] + nxt * cw[2:3]

    u_next = up(0)
    for b in range(bsz):
        u_gate, u_val = u_next
        if b + 1 < bsz:
            u_next = up(b + 1)
        ug = conv(u_gate, cw_ref[:, :D_FF])
        uv = conv(u_val, cw_ref[:, D_FF:])
        out = x_ref[b] + _mod_row(mod_ref.at[b], 5, tile_is_ctx) * _dot(ug * _sigmoid(ug) * uv, wd_ref[...])
        if fin_ref is not None:
            out = out * lax.rsqrt(jnp.mean(out * out, axis=-1, keepdims=True) + NORM_EPS) * fin_ref[...]
        o_ref[b] = out


def _ffn_mid_kernel(x_ref, before_ref, after_ref, mod_ref, g_ref, wu_ref, cw_ref, wd_ref, o_ref, **kw):
    _ffn_kernel(x_ref, before_ref, after_ref, mod_ref, g_ref, wu_ref, cw_ref, wd_ref, None, o_ref, **kw)


def _ffn(x, mod, g, wu, conv_w, wd, layer, final_g, *, ctx, tm):
    bsz, total, d = x.shape
    tile0 = 0 if final_g is None else ctx // tm
    before, after = _halo_specs(bsz, HALO, d, tm, total, tile0)
    row = pl.BlockSpec((bsz, tm, d), lambda i: (0, i + tile0, 0))
    once = lambda shp: pl.BlockSpec(shp, lambda i: tuple(0 for _ in shp), pipeline_mode=pl.Buffered(1))
    in_specs = [row, before, after, pl.BlockSpec((bsz, 2, 6, d), lambda i: (0, 0, 0, 0)),
                once((1, d)), _layer_weight((d, 2 * D_FF), layer), once((3, 2 * D_FF)),
                _layer_weight((D_FF, d), layer)]
    args = [x, x, x, mod, g, wu, conv_w, wd]
    kw = dict(tm=tm, ctx=ctx, total=total, tile0=tile0)
    if final_g is None:
        kern = functools.partial(_ffn_mid_kernel, **kw)
    else:
        kern = functools.partial(_ffn_kernel, **kw)
        in_specs.append(once((1, d)))
        args.append(final_g)
    out_spec = pl.BlockSpec((bsz, tm, d), lambda i: (0, i, 0))
    out_rows = total - tile0 * tm
    return pl.pallas_call(
        kern,
        grid=(total // tm - tile0,),
        in_specs=in_specs,
        out_specs=out_spec,
        out_shape=jax.ShapeDtypeStruct((bsz, out_rows, d), F32),
        compiler_params=_cparams(("arbitrary",)),
    )(*args)


def _rope_tables(ctx, seq):
    rows = seq // GRID_W
    t_row = jnp.broadcast_to(jnp.arange(rows)[:, None], (rows, GRID_W)).reshape(-1).astype(F32)
    t_col = jnp.broadcast_to(jnp.arange(GRID_W)[None, :], (rows, GRID_W)).reshape(-1).astype(F32)
    n_freq = HEAD_DIM // 4
    inv = ROPE_THETA ** (-jnp.arange(n_freq, dtype=F32) / n_freq)
    ar, ac = t_row[:, None] * inv, t_col[:, None] * inv
    cos = jnp.concatenate([jnp.cos(ar), jnp.cos(ar), jnp.cos(ac), jnp.cos(ac)], axis=1)
    sin = jnp.concatenate([-jnp.sin(ar), jnp.sin(ar), -jnp.sin(ac), jnp.sin(ac)], axis=1)
    cos = jnp.concatenate([jnp.ones((ctx, HEAD_DIM), F32), cos], axis=0)
    sin = jnp.concatenate([jnp.zeros((ctx, HEAD_DIM), F32), sin], axis=0)
    return jnp.concatenate([jnp.tile(cos, (1, LANES // HEAD_DIM)), jnp.tile(sin, (1, LANES // HEAD_DIM))], axis=1)


def _block_diag2(w):
    z = jnp.zeros_like(w[0])
    return jnp.concatenate([jnp.concatenate([w[0], z], axis=1), jnp.concatenate([z, w[1]], axis=1)], axis=0)


def kernel(x, c, ctx, c_ctx, ada_w, ada_b, norm1_g, w_in, conv_a_w, a_out_w, rwkv_mu, rwkv_w0, rwkv_w2,
           rwkv_a0, rwkv_a2, rwkv_g2, rwkv_k_k, rwkv_k_a, rwkv_r_k, rwkv_ln_g, rwkv_ln_b, rwkv_out_w,
           attn_sink, attn_out_w, w_o, norm2_g, ffn_up, ffn_conv, ffn_down, final_norm_g):
    bsz, seq, d = x.shape
    n_ctx = ctx.shape[1]
    depth = ada_w.shape[0]
    total = n_ctx + seq
    assert d == D_MODEL and seq % ATT_BLOCK == 0 and n_ctx % ATT_BLOCK == 0 and bsz + 1 <= HALO
    tm = 256
    assert total % tm == 0 and n_ctx % tm == 0

    cvec = jnp.zeros((HALO, d), F32).at[:bsz].set(c).at[bsz].set(c_ctx)
    mods = _ada(cvec, ada_w, ada_b)
    mods = mods.reshape(depth, HALO, 6, d)
    mods = jnp.stack([jnp.broadcast_to(mods[:, bsz:bsz + 1], (depth, bsz, 6, d)), mods[:, :bsz]], axis=2)

    rope_tab = _rope_tables(n_ctx, seq)
    tm_proj = tm // 2
    masks = _scan_masks(tm_proj)
    head_ones = masks[2].astype(BF16)
    xa = jnp.concatenate([ctx, x], axis=1)
    w_in, a_out_w, rwkv_out_w, attn_out_w, w_o, ffn_up, ffn_down = (
        w.astype(BF16) for w in (w_in, a_out_w, rwkv_out_w, attn_out_w, w_o, ffn_up, ffn_down))

    for l in range(depth):
        mod = mods[l]
        proj_a, proj_b, at, rt, bt, kt, gend, v, bonus, g = _proj(
            xa, mod, norm1_g[l][None], w_in, l, rope_tab, masks[0], head_ones, rwkv_mu[l],
            rwkv_w0[l].reshape(1, 2 * d), _block_diag2(rwkv_w2[l]).astype(BF16),
            rwkv_a0[l].reshape(1, 2 * d), _block_diag2(rwkv_a2[l]).astype(BF16), rwkv_g2[l].astype(BF16),
            rwkv_k_k[l][None], rwkv_k_a[l][None], rwkv_r_k[l].reshape(1, d), ctx=n_ctx, tm=tm_proj)
        y = [_scan(at, rt, bt, kt, gend, v, masks, ctx=n_ctx, direction=dr) for dr in range(2)]
        att = _attention(proj_b, attn_sink[l], ctx=n_ctx)
        xa = _merge(xa, mod, proj_a, proj_b, att, y, bonus, g, conv_a_w[l], rwkv_ln_g[l][None], rwkv_ln_b[l][None],
                    a_out_w, rwkv_out_w, attn_out_w, w_o, l, ctx=n_ctx, tm=tm)
        xa = _ffn(xa, mod, norm2_g[l][None], ffn_up, ffn_conv[l], ffn_down, l,
                  final_norm_g[None] if l == depth - 1 else None, ctx=n_ctx, tm=tm)
    return xa
```

```python
import functools

import jax
import jax.numpy as jnp
import numpy as np
from jax import lax
from jax.experimental import pallas as pl
from jax.experimental.pallas import tpu as pltpu

D_MODEL = 1024
GRID_W = 64
N_HEADS = 16
HEAD_DIM = 64
N_KV = 4
R_LORA = 64
R_GATE = 128
GN_EPS = 64e-5
WINDOW = 128
ATT_BLOCK = 128
LOG2_ATT_BLOCK = 7
ATT_STACK = N_HEADS // N_KV
ATT_SCALE = HEAD_DIM ** -0.5
ROPE_THETA = 10000.0
NEG_INF = -1e30
D_FF = 2816
NORM_EPS = 1e-6
KK_EPS = 1e-12
DECAY_SCALE = float(np.exp(-0.5))

LANES = 128
HALO = 8
HALO_BF16 = 16
N_PAIR = D_MODEL // LANES
CHUNK = 64
N_LEVELS = 5
VMEM_LIMIT = 56 * 1024 * 1024

BF16 = jnp.bfloat16
F32 = jnp.float32

KV_W = N_KV * HEAD_DIM
LORA_W = 2 * R_LORA + 2 * R_LORA + R_GATE
W_CONV, W_RKV, W_LORA = 0, 3 * D_MODEL, 6 * D_MODEL
W_Q = W_LORA + LORA_W
W_GATE = W_Q + D_MODEL + 2 * KV_W
N_PROJ_B = D_MODEL + 2 * KV_W + 3 * D_MODEL
B_Q, B_K, B_V, B_GATE = 0, D_MODEL, D_MODEL + KV_W, D_MODEL + 2 * KV_W
GATE_BLK = B_GATE
assert B_K % KV_W == 0 and 2 * GATE_BLK == 3 * D_MODEL and GATE_BLK % LANES == 0


def _cparams(sem):
    return pltpu.CompilerParams(dimension_semantics=sem, vmem_limit_bytes=VMEM_LIMIT)


def _dot(a, b):
    return jnp.dot(a.astype(BF16), b.astype(BF16), preferred_element_type=F32)


def _dot_nt(a, b):
    return lax.dot_general(a.astype(BF16), b.astype(BF16), (((1,), (1,)), ((), ())),
                           preferred_element_type=F32)


def _bmm(a, b):
    return jnp.einsum('pmk,pkn->pmn', a.astype(BF16), b.astype(BF16), preferred_element_type=F32)


def _bmm_nt(a, b):
    return jnp.einsum('pmk,pnk->pmn', a.astype(BF16), b.astype(BF16), preferred_element_type=F32)


def _bmm_tn(a, b):
    return jnp.einsum('pkm,pkn->pmn', a.astype(BF16), b.astype(BF16), preferred_element_type=F32)


def _sigmoid(x):
    return 0.5 + 0.5 * jnp.tanh(0.5 * x)


def _row_ids(i, tm):
    return i * tm + lax.broadcasted_iota(jnp.int32, (tm, 1), 0)


def _mod_row(mod_ref, k, is_ctx):
    return jnp.where(is_ctx, mod_ref[0, k:k + 1, :], mod_ref[1, k:k + 1, :])


def _neighbours(z, before, after, i, tm, ctx, total):
    start = i * tm
    keep_before = jnp.where((start == 0) | (start == ctx), 0.0, 1.0)
    keep_after = jnp.where((start + tm == ctx) | (start + tm == total), 0.0, 1.0)
    row = lax.broadcasted_iota(jnp.int32, (HALO, 1), 0)
    prev = pltpu.roll(z, 1, 0)
    nxt = pltpu.roll(z, tm - 1, 0)
    top = jnp.where(row == 0, before * keep_before, prev[:HALO])
    bottom = jnp.where(row == HALO - 1, after * keep_after, nxt[tm - HALO:])
    return (jnp.concatenate([top, prev[HALO:]], axis=0),
            jnp.concatenate([nxt[:tm - HALO], bottom], axis=0))


def _layer_weight(shape, layer):
    return pl.BlockSpec((None,) + shape, lambda *_: (layer,) + tuple(0 for _ in shape),
                        pipeline_mode=pl.Buffered(1))


def _halo_specs(bsz, rows, width, tm, total, tile0=0):
    per = tm // rows
    nblk = total // rows
    before = pl.BlockSpec((bsz, rows, width), lambda i: (0, jnp.maximum((i + tile0) * per - 1, 0), 0))
    after = pl.BlockSpec((bsz, rows, width), lambda i: (0, jnp.minimum((i + tile0 + 1) * per, nblk - 1), 0))
    return before, after


def _edge_rows(before_ref, after_ref):
    before = before_ref[...].astype(F32)
    return before[before.shape[0] - 1:], after_ref[...].astype(F32)[0:1]


def _ada_kernel(c_ref, w_ref, b_ref, o_ref):
    c = c_ref[...]
    o_ref[...] = _dot(c * _sigmoid(c), w_ref[...]) + b_ref[...]


def _ada(cvec, ada_w, ada_b):
    depth, d, n = ada_w.shape
    tn = 1536
    return pl.pallas_call(
        _ada_kernel,
        grid=(depth, n // tn),
        in_specs=[pl.BlockSpec((HALO, d), lambda l, j: (0, 0)),
                  pl.BlockSpec((None, d, tn), lambda l, j: (l, 0, j)),
                  pl.BlockSpec((None, 1, tn), lambda l, j: (l, 0, j))],
        out_specs=pl.BlockSpec((None, HALO, tn), lambda l, j: (l, 0, j)),
        out_shape=jax.ShapeDtypeStruct((depth, HALO, n), F32),
        compiler_params=_cparams(("arbitrary", "arbitrary")),
    )(cvec, ada_w, ada_b.reshape(depth, 1, n))


def _rope(z, tab):
    cos, sin = tab[:, :LANES], tab[:, LANES:]
    lane = lax.broadcasted_iota(jnp.int32, (1, LANES), 1)
    first = jnp.bitwise_and(lane, HEAD_DIM // 2 - 1) < (HEAD_DIM // 4)
    out = []
    for p in range(z.shape[1] // LANES):
        zp = z[:, LANES * p:LANES * (p + 1)]
        partner = jnp.where(first, pltpu.roll(zp, LANES - HEAD_DIM // 4, 1), pltpu.roll(zp, HEAD_DIM // 4, 1))
        out.append(zp * cos + partner * sin)
    return jnp.concatenate(out, axis=1)


def _tile_neighbours(u, tm, keep_before, keep_after):
    ext = tm + 2 * HALO
    row = lax.broadcasted_iota(jnp.int32, (HALO, 1), 0)
    prev = pltpu.roll(u, 1, 0)[HALO:HALO + tm]
    nxt = pltpu.roll(u, ext - 1, 0)[HALO:HALO + tm]
    top = jnp.where(row == 0, prev[:HALO] * keep_before, prev[:HALO])
    bottom = jnp.where(row == HALO - 1, nxt[tm - HALO:] * keep_after, nxt[tm - HALO:])
    return (jnp.concatenate([top, prev[HALO:]], axis=0), jnp.concatenate([nxt[:tm - HALO], bottom], axis=0))


def _proj_kernel(x_ref, before_ref, after_ref, mod_ref, g_ref, w_ref, tab_ref,
                 tri_ref, ones_ref, mu_ref, w0_ref, w2_ref, a0_ref, a2_ref, g2_ref, kk_ref, ka_ref, rk_ref,
                 cv_out, ob_ref, at_out, rt_out, bt_out, kt_out, gend_out, v_out, bonus_out, gg_out, *,
                 tm, ctx, total):
    i = pl.program_id(0)
    bsz = x_ref.shape[0]
    d = D_MODEL
    start = i * tm
    ext = tm + 2 * HALO
    is_ctx = (start - HALO + lax.broadcasted_iota(jnp.int32, (ext, 1), 0)) < ctx
    h_ext = []
    for b in range(bsz):
        xe = jnp.concatenate([before_ref[b], x_ref[b], after_ref[b]], axis=0)
        y = xe * lax.rsqrt(jnp.mean(xe * xe, axis=-1, keepdims=True) + NORM_EPS) * g_ref[...]
        mod_b = mod_ref.at[b]
        h_ext.append((y * (1.0 + _mod_row(mod_b, 1, is_ctx)) + _mod_row(mod_b, 0, is_ctx)).astype(BF16))
    h = jnp.concatenate([he[HALO:HALO + tm] for he in h_ext], axis=0)

    rkv = _dot(jnp.concatenate(h_ext, axis=0), w_ref[:, W_RKV:W_RKV + 3 * d])
    lora = _dot(h, w_ref[:, W_LORA:W_LORA + LORA_W])
    tab = tab_ref[...]
    keep_before = jnp.where((start == 0) | (start == ctx), 0.0, 1.0)
    keep_after = jnp.where((start + tm == ctx) | (start + tm == total), 0.0, 1.0)

    def tokens(b):
        rkv_b = rkv[b * ext:(b + 1) * ext]
        prev, nxt = _tile_neighbours(rkv_b, tm, keep_before, keep_after)
        _rwkv_tokens(
            prev, rkv_b[HALO:HALO + tm], nxt, lora[b * tm:(b + 1) * tm],
            tri_ref, ones_ref, mu_ref, w0_ref, w2_ref, a0_ref, a2_ref, g2_ref, kk_ref, ka_ref, rk_ref,
            at_out.at[:, b], rt_out.at[:, b], bt_out.at[:, b], kt_out.at[:, b], gend_out.at[:, b],
            v_out.at[b], bonus_out.at[b], gg_out.at[b], tm)

    qkv = _dot(h, w_ref[:, W_Q:W_GATE])
    for b in range(bsz):
        rows = slice(b * tm, (b + 1) * tm)
        rot = _rope(qkv[rows, :B_V - B_Q], tab)
        ob_ref[b, :, B_Q:B_K] = (rot[:, :B_K - B_Q] * ATT_SCALE).astype(ob_ref.dtype)
        ob_ref[b, :, B_K:B_V] = rot[:, B_K - B_Q:].astype(ob_ref.dtype)
        ob_ref[b, :, B_V:B_GATE] = qkv[rows, B_V - B_Q:].astype(ob_ref.dtype)
    gates = _sigmoid(_dot(h, w_ref[:, W_GATE:]))
    ob_ref[:, :, B_GATE:] = gates.reshape(bsz, tm, 3 * d).astype(ob_ref.dtype)
    cv_out[...] = _dot(h, w_ref[:, W_CONV:W_CONV + 3 * d]).reshape(bsz, tm, 3 * d).astype(cv_out.dtype)
    for b in range(bsz):
        tokens(b)


def _proj(x, mod, g, w, layer, rope_tab, tri, head_ones, mu, w0, w2bd, a0, a2bd, g2, k_k, k_a, r_k, *, ctx, tm):
    bsz, total, d = x.shape
    assert w.shape[2] == W_GATE + 3 * d
    kern = functools.partial(_proj_kernel, tm=tm, ctx=ctx, total=total)
    before, after = _halo_specs(bsz, HALO, d, tm, total)
    once = lambda shp: pl.BlockSpec(shp, lambda i: tuple(0 for _ in shp), pipeline_mode=pl.Buffered(1))
    row = lambda n: pl.BlockSpec((bsz, tm, n), lambda i: (0, i, 0))
    pm = pl.BlockSpec((bsz, N_PAIR, tm, LANES), lambda i: (0, 0, i, 0))
    pm2 = pl.BlockSpec((2, bsz, N_PAIR, tm, LANES), lambda i: (0, 0, 0, i, 0))
    ge = pl.BlockSpec((2, bsz, tm // CHUNK, N_PAIR, LANES), lambda i: (0, 0, i, 0, 0))
    sds = jax.ShapeDtypeStruct((bsz, N_PAIR, total, LANES), BF16)
    sds2 = jax.ShapeDtypeStruct((2, bsz, N_PAIR, total, LANES), BF16)
    sds_ge = jax.ShapeDtypeStruct((2, bsz, total // CHUNK, N_PAIR, LANES), F32)
    return pl.pallas_call(
        kern,
        grid=(total // tm,),
        in_specs=[row(d), before, after, pl.BlockSpec((bsz, 2, 6, d), lambda i: (0, 0, 0, 0)),
                  once((1, d)), _layer_weight((d, W_GATE + 3 * d), layer),
                  pl.BlockSpec((tm, 2 * LANES), lambda i: (i, 0)),
                  once((2, tm, tm)), once((LANES, LANES)), once((2, 3 * d)), once((1, 2 * d)), once((LANES, 2 * d)),
                  once((1, 2 * d)), once((LANES, 2 * d)), once((R_GATE, d)), once((1, d)), once((1, d)),
                  once((1, d))],
        out_specs=[row(3 * d), row(N_PROJ_B), pm2, pm2, pm2, pm2, ge, pm, pm, pm],
        out_shape=[jax.ShapeDtypeStruct((bsz, total, 3 * d), BF16),
                   jax.ShapeDtypeStruct((bsz, total, N_PROJ_B), BF16),
                   sds2, sds2, sds2, sds2, sds_ge, sds, sds, sds],
        compiler_params=_cparams(("arbitrary",)),
    )(x, x, x, mod, g, w, rope_tab, tri, head_ones, mu, w0, w2bd, a0, a2bd, g2, k_k, k_a, r_k)


def _head_sums(z, head_ones):
    return jnp.dot(z.astype(BF16), head_ones, preferred_element_type=F32)


def _head_sums_exact(z, lane_lo):
    s_lo = jnp.sum(jnp.where(lane_lo, z, 0.0), axis=-1, keepdims=True)
    s_hi = jnp.sum(jnp.where(lane_lo, 0.0, z), axis=-1, keepdims=True)
    return jnp.where(lane_lo, s_lo, s_hi)


def _chunk_cumsum(tri, z):
    hi = z.astype(BF16)
    low = (z - hi.astype(F32)).astype(BF16)
    dot = functools.partial(jnp.dot, preferred_element_type=F32)
    return dot(tri, low) + dot(tri, hi)


def _rwkv_tokens(prev, x, nxt, lo, tri_ref, ones_ref, mu_ref, w0_ref, w2_ref, a0_ref, a2_ref, g2_ref, kk_ref,
                 ka_ref, rk_ref, at_out, rt_out, bt_out, kt_out, gend_out, v_out, bonus_out, g_out, tm):
    mu_prev = mu_ref[0:1, :]
    mu_next = mu_ref[1:2, :]
    s = prev * mu_prev + x * (1.0 - mu_prev - mu_next) + nxt * mu_next
    d = D_MODEL
    r, k, v = s[:, :d], s[:, d:2 * d], s[:, 2 * d:]

    w_pre = w0_ref[...] + _dot(jnp.tanh(lo[:, :LANES]), w2_ref[...])
    lw = -DECAY_SCALE * _sigmoid(w_pre)
    a = _sigmoid(a0_ref[...] + _dot(lo[:, LANES:2 * LANES], a2_ref[...]))
    g = _dot(_sigmoid(lo[:, 2 * LANES:]), g2_ref[...])
    kk = k * kk_ref[...]
    ka = ka_ref[...]
    rk = rk_ref[...]
    cl = [_chunk_cumsum(tri_ref[dr], lw[:, d * dr:d * (dr + 1)]) for dr in range(2)]

    head_ones = ones_ref[...]
    for p in range(N_PAIR):
        sl = slice(LANES * p, LANES * (p + 1))
        kkp = kk[:, sl]
        kn = kkp * lax.rsqrt(_head_sums(kkp * kkp, head_ones) + KK_EPS)
        rp, kp, vp = r[:, sl], k[:, sl], v[:, sl]
        bonus = jnp.zeros_like(rp)
        for dr in range(2):
            dsl = slice(d * dr + LANES * p, d * dr + LANES * (p + 1))
            ad = a[:, dsl]
            kd = kp * (1.0 + (ad - 1.0) * ka[:, sl])
            bonus = bonus + _head_sums(rp * kd * rk[:, sl], head_ones) * vp
            cl_incl = cl[dr][:, sl]
            inv_g = jnp.exp(-cl_incl)
            at_out[dr, p] = (-kn * jnp.exp(cl_incl - lw[:, dsl])).astype(BF16)
            rt_out[dr, p] = (rp * jnp.exp(cl_incl)).astype(BF16)
            bt_out[dr, p] = (kn * ad * inv_g).astype(BF16)
            kt_out[dr, p] = (kd * inv_g).astype(BF16)
            for q in range(tm // CHUNK):
                end = CHUNK * q + (CHUNK - 1 if dr == 0 else 0)
                gend_out[dr, q, p:p + 1, :] = jnp.exp(cl_incl[end:end + 1, :])
        v_out[p] = vp.astype(BF16)
        bonus_out[p] = bonus.astype(BF16)
        g_out[p] = g[:, sl].astype(BF16)


M_STRICT, M_INCL, M_EYE, M_LEVEL0 = 0, 1, 2, 3
N_MASKS = M_LEVEL0 + 1 + N_LEVELS


def _scan_masks(tm):
    c = CHUNK
    t = np.arange(c)[:, None]
    s = np.arange(c)[None, :]
    cum = np.zeros((2, tm, tm), np.float32)
    msk = np.zeros((2, N_MASKS, c, 2 * c), np.float32)
    for dr in range(2):
        earlier = (s < t) if dr == 0 else (s > t)
        cum[dr] = np.kron(np.eye(tm // c), earlier | (s == t))
        planes = [earlier, earlier | (s == t), s == t]
        m = 1
        while m < c:
            same = (t // (2 * m)) == (s // (2 * m))
            t_late = (t % (2 * m)) >= m
            s_late = (s % (2 * m)) >= m
            planes.append(same & (t_late & ~s_late if dr == 0 else ~t_late & s_late))
            m *= 2
        for q, plane in enumerate(planes):
            msk[dr, q] = np.concatenate([plane, plane], axis=1)
    bdm = np.kron(np.eye(2, dtype=np.float32), np.ones((c, c), np.float32))
    return jnp.asarray(cum, BF16), jnp.asarray(msk), jnp.asarray(bdm)


SCAN_GROUP = 4


def _scan_kernel(msk_ref, bdm_ref, at_ref, rt_ref, bt_ref, kt_ref, gend_ref, v_ref, y_ref, s_ref, *, reverse):
    c = CHUNK
    bsz = v_ref.shape[0]
    n = bsz * N_PAIR
    order = tuple(reversed(range(SCAN_GROUP))) if reverse else tuple(range(SCAN_GROUP))

    @pl.when(pl.program_id(0) == 0)
    def _():
        s_ref[...] = jnp.zeros_like(s_ref)

    bdm = bdm_ref[...]
    bdm_bf = bdm.astype(BF16)
    m_strict = msk_ref[M_STRICT]
    m_incl = msk_ref[M_INCL]

    def load(ref):
        z = ref[...].reshape(n, SCAN_GROUP * c, LANES)
        return jnp.concatenate([z[:, c * k:c * (k + 1)] for k in range(SCAN_GROUP)], axis=0)

    def stack(z):
        z = z.astype(BF16)
        return jnp.concatenate([z, z], axis=1) * bdm_bf

    bt, kt, v = load(bt_ref), load(kt_ref), load(v_ref)
    ar = jnp.concatenate([load(at_ref), load(rt_ref)], axis=1)
    gram = _bmm_nt(ar, jnp.concatenate([stack(bt), stack(kt)], axis=1))
    g_ab = gram[:, :c, :2 * c]
    from_v = jnp.concatenate([gram[:, :c, 2 * c:] * m_strict, gram[:, c:, 2 * c:] * m_incl], axis=1)
    from_v = _bmm(from_v, stack(v))
    y_from_u = gram[:, c:, :2 * c] * m_incl

    tri = msk_ref[M_EYE] + g_ab * msk_ref[M_LEVEL0]
    for lvl in range(1, N_LEVELS + 1):
        w = _bmm(g_ab * msk_ref[M_LEVEL0 + lvl], stack(tri))
        tri = tri + _bmm(tri, stack(w))

    state = s_ref[...]
    ys = [None] * SCAN_GROUP
    for k in order:
        sel = slice(n * k, n * (k + 1))
        g_end = jnp.stack([gend_ref[q // N_PAIR, k, q % N_PAIR:q % N_PAIR + 1, :] for q in range(n)], axis=0)
        partial = _bmm_nt(ar[sel], state) + from_v[sel]
        u = _bmm(tri[sel], stack(partial[:, :c]))
        ys[k] = partial[:, c:] + _bmm(y_from_u[sel], stack(u))
        upd = _bmm_tn(jnp.concatenate([u.astype(BF16), v[sel]], axis=1), jnp.concatenate([bt[sel], kt[sel]], axis=1))
        state = (state + upd * bdm) * g_end
    s_ref[...] = state
    y_ref[...] = jnp.concatenate(ys, axis=1).reshape(bsz, N_PAIR, SCAN_GROUP * c, LANES).astype(y_ref.dtype)


def _scan(at, rt, bt, kt, gend, v, masks, *, ctx, direction):
    bsz, _, total, _ = v.shape
    rows = SCAN_GROUP * CHUNK
    steps, ctx_steps = total // rows, ctx // rows
    assert steps * rows == total and ctx_steps * rows == ctx
    _, msk, bdm = masks

    def blk(j):
        return j if direction == 0 else jnp.where(j < ctx_steps, ctx_steps - 1 - j, steps + ctx_steps - 1 - j)

    shared = pl.BlockSpec((bsz, N_PAIR, rows, LANES), lambda j: (0, 0, blk(j), 0))
    per_dir = pl.BlockSpec((None, bsz, N_PAIR, rows, LANES), lambda j: (direction, 0, 0, blk(j), 0))
    ends = pl.BlockSpec((None, bsz, SCAN_GROUP, N_PAIR, LANES), lambda j: (direction, 0, blk(j), 0, 0))
    return pl.pallas_call(
        functools.partial(_scan_kernel, reverse=direction == 1),
        grid=(steps,),
        in_specs=[pl.BlockSpec((None, N_MASKS, CHUNK, 2 * CHUNK), lambda j: (direction, 0, 0, 0)),
                  pl.BlockSpec((2 * CHUNK, 2 * CHUNK), lambda j: (0, 0)),
                  per_dir, per_dir, per_dir, per_dir, ends, shared],
        out_specs=shared,
        out_shape=jax.ShapeDtypeStruct((bsz, N_PAIR, total, LANES), BF16),
        scratch_shapes=[pltpu.VMEM((bsz * N_PAIR, 2 * HEAD_DIM, 2 * HEAD_DIM), F32)],
        compiler_params=_cparams(("arbitrary",)),
    )(msk, bdm, at, rt, bt, kt, gend, v)


def _attn_kernel(sink_ref, q_ref, kvp_ref, kvc_ref, kvn_ref, kvx_ref, o_ref, *, n_ctx_blk, n_blk):
    i = pl.program_id(1)
    blk = ATT_BLOCK
    kv_refs = (kvp_ref, kvc_ref, kvn_ref, kvx_ref)
    q = q_ref[...].astype(F32)
    k_all = jnp.concatenate([ref[:, :KV_W] for ref in kv_refs], axis=0)
    v_all = jnp.concatenate([ref[:, KV_W:] for ref in kv_refs], axis=0)

    qi = lax.broadcasted_iota(jnp.int32, (blk, 3 * blk), 0)
    si = lax.broadcasted_iota(jnp.int32, (blk, 3 * blk), 1)
    rel = si - blk - qi
    key_blk = i - 1 + jnp.right_shift(si, LOG2_ATT_BLOCK)
    ok = (jnp.where(jnp.abs(rel) <= WINDOW, 1, 0) * jnp.where(key_blk >= n_ctx_blk, 1, 0)
          * jnp.where(key_blk < n_blk, 1, 0) * jnp.where(i >= n_ctx_blk, 1, 0))
    bias = jnp.where(ok > 0, 0.0, NEG_INF)

    lane_lo = lax.broadcasted_iota(jnp.int32, (1, LANES), 1) < HEAD_DIM
    stack = ATT_STACK
    outs = [None] * N_HEADS
    bias_g = jnp.concatenate([bias] * stack, axis=0)

    for h0 in range(0, N_HEADS, stack):
        kvh = h0 // (N_HEADS // N_KV)
        kv_slab = slice(LANES * (kvh // 2), LANES * (kvh // 2 + 1))
        kv_lo = kvh % 2 == 0
        rows = []
        for h in range(h0, h0 + stack):
            qp = q[:, LANES * (h // 2):LANES * (h // 2 + 1)]
            if (h % 2 == 0) != kv_lo:
                qp = pltpu.roll(qp, HEAD_DIM, 1)
            rows.append(jnp.where(lane_lo, qp, 0.0) if kv_lo else jnp.where(lane_lo, 0.0, qp))
        s = _dot_nt(jnp.concatenate(rows, axis=0), k_all[:, kv_slab])
        s = jnp.concatenate([s[:, :3 * blk] + bias_g, s[:, 3 * blk:]], axis=1)
        sink = jnp.concatenate([jnp.full((blk, 1), sink_ref[h], F32) for h in range(h0, h0 + stack)], axis=0)
        m = jnp.maximum(jnp.max(s, axis=-1, keepdims=True), sink)
        e = jnp.exp(s - m)
        denom = jnp.sum(e, axis=-1, keepdims=True) + jnp.exp(sink - m)
        o = _dot(e, v_all[:, kv_slab]) / denom
        for j, h in enumerate(range(h0, h0 + stack)):
            oh = o[j * blk:(j + 1) * blk]
            if (h % 2 == 0) != kv_lo:
                oh = pltpu.roll(oh, HEAD_DIM, 1)
            outs[h] = oh
    o_ref[...] = jnp.concatenate(
        [jnp.where(lane_lo, outs[2 * p], outs[2 * p + 1]) for p in range(N_PAIR)], axis=1).astype(o_ref.dtype)


def _attention(qkv, sink, *, ctx):
    bsz, total, _ = qkv.shape
    blk = ATT_BLOCK
    n_blk, n_ctx_blk = total // blk, ctx // blk
    assert B_V == B_K + KV_W and B_K % (2 * KV_W) == 0
    kv_col = B_K // (2 * KV_W)
    kern = functools.partial(_attn_kernel, n_ctx_blk=n_ctx_blk, n_blk=n_blk)
    prev_i = lambda i: jnp.maximum(i - 1, 0)
    next_i = lambda i: jnp.minimum(i + 1, n_blk - 1)
    same = lambda i: i

    def kv(row):
        return pl.BlockSpec((None, blk, 2 * KV_W), lambda b, i: (b, row(i), kv_col))

    return pl.pallas_call(
        kern,
        grid=(bsz, n_blk),
        in_specs=[pl.BlockSpec(memory_space=pltpu.SMEM),
                  pl.BlockSpec((None, blk, D_MODEL), lambda b, i: (b, i, B_Q // D_MODEL)),
                  kv(prev_i), kv(same), kv(next_i),
                  pl.BlockSpec((None, ctx, 2 * KV_W), lambda b, i: (b, 0, kv_col))],
        out_specs=pl.BlockSpec((None, blk, D_MODEL), lambda b, i: (b, i, 0)),
        out_shape=jax.ShapeDtypeStruct((bsz, total, D_MODEL), BF16),
        compiler_params=_cparams(("arbitrary", "arbitrary")),
    )(sink, qkv, qkv, qkv, qkv, qkv)


def _merge_kernel(x_ref, mod_ref, cv_ref, before_ref, after_ref, gate_lo_ref, gate_hi_ref, att_ref, yf_ref, yb_ref,
                  bonus_ref, g_ref,
                  cw_ref, lng_ref, lnb_ref, wa_ref, wb_ref, wc_ref, wo_ref, o_ref, *, tm, ctx, total):
    i = pl.program_id(0)
    bsz = x_ref.shape[0]
    d = D_MODEL
    cw = cw_ref[...]
    lng = lng_ref[...]
    lnb = lnb_ref[...]
    lane_lo = lax.broadcasted_iota(jnp.int32, (1, LANES), 1) < HEAD_DIM
    convs, rwkvs = [], []
    for b in range(bsz):
        cv = cv_ref[b].astype(F32)
        z = cv[:, d:2 * d] * cv[:, 2 * d:]
        before, after = _edge_rows(before_ref.at[b], after_ref.at[b])
        prev, nxt = _neighbours(z, before[:, d:2 * d] * before[:, 2 * d:], after[:, d:2 * d] * after[:, 2 * d:],
                                i, tm, ctx, total)
        convs.append((cv[:, :d] * (prev * cw[0:1] + z * cw[1:2] + nxt * cw[2:3])).astype(BF16))
        slabs = []
        for p in range(N_PAIR):
            sl = slice(LANES * p, LANES * (p + 1))
            y = yf_ref[b, p].astype(F32) + yb_ref[b, p].astype(F32)
            mean = _head_sums_exact(y, lane_lo) * (1.0 / HEAD_DIM)
            yc = y - mean
            var = _head_sums_exact(yc * yc, lane_lo) * (1.0 / HEAD_DIM)
            gn = yc * lax.rsqrt(var + GN_EPS) * lng[:, sl] + lnb[:, sl]
            slabs.append(((gn + bonus_ref[b, p]) * g_ref[b, p]).astype(BF16))
        rwkvs.append(jnp.concatenate(slabs, axis=1))

    rows = bsz * tm
    gates = jnp.concatenate([gate_lo_ref[...].reshape(rows, GATE_BLK), gate_hi_ref[...].reshape(rows, GATE_BLK)],
                            axis=1).astype(F32)
    m = (gates[:, :d] * _dot(jnp.concatenate(convs, axis=0), wa_ref[...])
         + gates[:, d:2 * d] * _dot(jnp.concatenate(rwkvs, axis=0), wb_ref[...])
         + gates[:, 2 * d:] * _dot(att_ref[...].reshape(rows, d), wc_ref[...]))
    out = _dot(m, wo_ref[...])
    is_ctx = _row_ids(i, tm) < ctx
    for b in range(bsz):
        o_ref[b] = x_ref[b] + _mod_row(mod_ref.at[b], 2, is_ctx) * out[b * tm:(b + 1) * tm]


def _merge(x, mod, cv, proj_b, att, y, bonus, g, conv_w, ln_g, ln_b, wa, wb, wc, wo, layer, *, ctx, tm):
    bsz, total, d = x.shape
    kern = functools.partial(_merge_kernel, tm=tm, ctx=ctx, total=total)
    col_cv, col_gate = 0, B_GATE // GATE_BLK
    before, after = _halo_specs(bsz, HALO_BF16, 3 * d, tm, total)
    row = lambda w, col=0: pl.BlockSpec((bsz, tm, w), lambda i: (0, i, col))
    once = lambda shp: pl.BlockSpec(shp, lambda i: tuple(0 for _ in shp), pipeline_mode=pl.Buffered(1))
    pm = pl.BlockSpec((bsz, N_PAIR, tm, LANES), lambda i: (0, 0, i, 0))
    return pl.pallas_call(
        kern,
        grid=(total // tm,),
        in_specs=[row(d), pl.BlockSpec((bsz, 2, 6, d), lambda i: (0, 0, 0, 0)),
                  row(3 * d, col_cv), before, after, row(GATE_BLK, col_gate), row(GATE_BLK, col_gate + 1),
                  row(d), pm, pm, pm, pm,
                  once((3, d)), once((1, d)), once((1, d)),
                  _layer_weight((d, d), layer), _layer_weight((d, d), layer), _layer_weight((d, d), layer),
                  _layer_weight((d, d), layer)],
        out_specs=row(d),
        out_shape=jax.ShapeDtypeStruct((bsz, total, d), F32),
        compiler_params=_cparams(("arbitrary",)),
    )(x, mod, cv, cv, cv, proj_b, proj_b, att, y[0], y[1], bonus, g, conv_w, ln_g, ln_b,
      wa, wb, wc, wo)


def _ffn_kernel(x_ref, before_ref, after_ref, mod_ref, g_ref, wu_ref, cw_ref, wd_ref, fin_ref, o_ref, *,
                tm, ctx, total, tile0):
    i = pl.program_id(0) + tile0
    bsz = x_ref.shape[0]
    start = i * tm
    ext = tm + 2 * HALO
    is_ctx = (start - HALO + lax.broadcasted_iota(jnp.int32, (ext, 1), 0)) < ctx
    tile_is_ctx = _row_ids(i, tm) < ctx

    keep_before = jnp.where((start == 0) | (start == ctx), 0.0, 1.0)
    keep_after = jnp.where((start + tm == ctx) | (start + tm == total), 0.0, 1.0)

    def up(b):
        xe = jnp.concatenate([before_ref[b], x_ref[b], after_ref[b]], axis=0)
        y = xe * lax.rsqrt(jnp.mean(xe * xe, axis=-1, keepdims=True) + NORM_EPS) * g_ref[...]
        mod_b = mod_ref.at[b]
        h = (y * (1.0 + _mod_row(mod_b, 4, is_ctx)) + _mod_row(mod_b, 3, is_ctx)).astype(BF16)
        return _dot(h, wu_ref[:, :D_FF]), _dot(h, wu_ref[:, D_FF:])

    def conv(u, cw):
        prev, nxt = _tile_neighbours(u, tm, keep_before, keep_after)
        return prev * cw[0:1] + u[HALO:HALO + tm] * cw[1:2] + nxt * cw[2:3]

    u_next = up(0)
    for b in range(bsz):
        u_gate, u_val = u_next
        if b + 1 < bsz:
            u_next = up(b + 1)
        ug = conv(u_gate, cw_ref[:, :D_FF])
        uv = conv(u_val, cw_ref[:, D_FF:])
        out = x_ref[b] + _mod_row(mod_ref.at[b], 5, tile_is_ctx) * _dot(ug * _sigmoid(ug) * uv, wd_ref[...])
        if fin_ref is not None:
            out = out * lax.rsqrt(jnp.mean(out * out, axis=-1, keepdims=True) + NORM_EPS) * fin_ref[...]
        o_ref[b] = out


def _ffn_mid_kernel(x_ref, before_ref, after_ref, mod_ref, g_ref, wu_ref, cw_ref, wd_ref, o_ref, **kw):
    _ffn_kernel(x_ref, before_ref, after_ref, mod_ref, g_ref, wu_ref, cw_ref, wd_ref, None, o_ref, **kw)


def _ffn(x, mod, g, wu, conv_w, wd, layer, final_g, *, ctx, tm):
    bsz, total, d = x.shape
    tile0 = 0 if final_g is None else ctx // tm
    before, after = _halo_specs(bsz, HALO, d, tm, total, tile0)
    row = pl.BlockSpec((bsz, tm, d), lambda i: (0, i + tile0, 0))
    once = lambda shp: pl.BlockSpec(shp, lambda i: tuple(0 for _ in shp), pipeline_mode=pl.Buffered(1))
    in_specs = [row, before, after, pl.BlockSpec((bsz, 2, 6, d), lambda i: (0, 0, 0, 0)),
                once((1, d)), _layer_weight((d, 2 * D_FF), layer), once((3, 2 * D_FF)),
                _layer_weight((D_FF, d), layer)]
    args = [x, x, x, mod, g, wu, conv_w, wd]
    kw = dict(tm=tm, ctx=ctx, total=total, tile0=tile0)
    if final_g is None:
        kern = functools.partial(_ffn_mid_kernel, **kw)
    else:
        kern = functools.partial(_ffn_kernel, **kw)
        in_specs.append(once((1, d)))
        args.append(final_g)
    out_spec = pl.BlockSpec((bsz, tm, d), lambda i: (0, i, 0))
    out_rows = total - tile0 * tm
    return pl.pallas_call(
        kern,
        grid=(total // tm - tile0,),
        in_specs=in_specs,
        out_specs=out_spec,
        out_shape=jax.ShapeDtypeStruct((bsz, out_rows, d), F32),
        compiler_params=_cparams(("arbitrary",)),
    )(*args)


def _rope_tables(ctx, seq):
    rows = seq // GRID_W
    t_row = jnp.broadcast_to(jnp.arange(rows)[:, None], (rows, GRID_W)).reshape(-1).astype(F32)
    t_col = jnp.broadcast_to(jnp.arange(GRID_W)[None, :], (rows, GRID_W)).reshape(-1).astype(F32)
    n_freq = HEAD_DIM // 4
    inv = ROPE_THETA ** (-jnp.arange(n_freq, dtype=F32) / n_freq)
    ar, ac = t_row[:, None] * inv, t_col[:, None] * inv
    cos = jnp.concatenate([jnp.cos(ar), jnp.cos(ar), jnp.cos(ac), jnp.cos(ac)], axis=1)
    sin = jnp.concatenate([-jnp.sin(ar), jnp.sin(ar), -jnp.sin(ac), jnp.sin(ac)], axis=1)
    cos = jnp.concatenate([jnp.ones((ctx, HEAD_DIM), F32), cos], axis=0)
    sin = jnp.concatenate([jnp.zeros((ctx, HEAD_DIM), F32), sin], axis=0)
    return jnp.concatenate([jnp.tile(cos, (1, LANES // HEAD_DIM)), jnp.tile(sin, (1, LANES // HEAD_DIM))], axis=1)


def _block_diag2(w):
    z = jnp.zeros_like(w[0])
    return jnp.concatenate([jnp.concatenate([w[0], z], axis=1), jnp.concatenate([z, w[1]], axis=1)], axis=0)


def kernel(x, c, ctx, c_ctx, ada_w, ada_b, norm1_g, w_in, conv_a_w, a_out_w, rwkv_mu, rwkv_w0, rwkv_w2,
           rwkv_a0, rwkv_a2, rwkv_g2, rwkv_k_k, rwkv_k_a, rwkv_r_k, rwkv_ln_g, rwkv_ln_b, rwkv_out_w,
           attn_sink, attn_out_w, w_o, norm2_g, ffn_up, ffn_conv, ffn_down, final_norm_g):
    bsz, seq, d = x.shape
    n_ctx = ctx.shape[1]
    depth = ada_w.shape[0]
    total = n_ctx + seq
    assert d == D_MODEL and seq % ATT_BLOCK == 0 and n_ctx % ATT_BLOCK == 0 and bsz + 1 <= HALO
    tm = 256
    assert total % tm == 0 and n_ctx % tm == 0

    cvec = jnp.zeros((HALO, d), F32).at[:bsz].set(c).at[bsz].set(c_ctx)
    mods = _ada(cvec, ada_w, ada_b)
    mods = mods.reshape(depth, HALO, 6, d)
    mods = jnp.stack([jnp.broadcast_to(mods[:, bsz:bsz + 1], (depth, bsz, 6, d)), mods[:, :bsz]], axis=2)

    rope_tab = _rope_tables(n_ctx, seq)
    tm_proj = tm // 2
    masks = _scan_masks(tm_proj)
    head_ones = masks[2].astype(BF16)
    xa = jnp.concatenate([ctx, x], axis=1)
    w_in, a_out_w, rwkv_out_w, attn_out_w, w_o, ffn_up, ffn_down = (
        w.astype(BF16) for w in (w_in, a_out_w, rwkv_out_w, attn_out_w, w_o, ffn_up, ffn_down))

    for l in range(depth):
        mod = mods[l]
        cv, proj_b, at, rt, bt, kt, gend, v, bonus, g = _proj(
            xa, mod, norm1_g[l][None], w_in, l, rope_tab, masks[0], head_ones, rwkv_mu[l],
            rwkv_w0[l].reshape(1, 2 * d), _block_diag2(rwkv_w2[l]).astype(BF16),
            rwkv_a0[l].reshape(1, 2 * d), _block_diag2(rwkv_a2[l]).astype(BF16), rwkv_g2[l].astype(BF16),
            rwkv_k_k[l][None], rwkv_k_a[l][None], rwkv_r_k[l].reshape(1, d), ctx=n_ctx, tm=tm_proj)
        y = [_scan(at, rt, bt, kt, gend, v, masks, ctx=n_ctx, direction=dr) for dr in range(2)]
        att = _attention(proj_b, attn_sink[l], ctx=n_ctx)
        xa = _merge(xa, mod, cv, proj_b, att, y, bonus, g, conv_a_w[l], rwkv_ln_g[l][None], rwkv_ln_b[l][None],
                    a_out_w, rwkv_out_w, attn_out_w, w_o, l, ctx=n_ctx, tm=tm)
        xa = _ffn(xa, mod, norm2_g[l][None], ffn_up, ffn_conv[l], ffn_down, l,
                  final_norm_g[None] if l == depth - 1 else None, ctx=n_ctx, tm=tm)
    return xa
```

```python
import functools

import jax
import jax.numpy as jnp
import numpy as np
from jax import lax
from jax.experimental import pallas as pl
from jax.experimental.pallas import tpu as pltpu

D_MODEL = 1024
GRID_W = 64
N_HEADS = 16
HEAD_DIM = 64
N_KV = 4
R_LORA = 64
R_GATE = 128
GN_EPS = 64e-5
WINDOW = 128
ATT_BLOCK = 128
LOG2_ATT_BLOCK = 7
ATT_STACK = N_HEADS // N_KV
ATT_SCALE = HEAD_DIM ** -0.5
ROPE_THETA = 10000.0
NEG_INF = -1e30
D_FF = 2816
NORM_EPS = 1e-6
KK_EPS = 1e-12
DECAY_SCALE = float(np.exp(-0.5))

LANES = 128
HALO = 8
HALO_BF16 = 16
N_PAIR = D_MODEL // LANES
CHUNK = 64
N_LEVELS = 5
VMEM_LIMIT = 56 * 1024 * 1024

BF16 = jnp.bfloat16
F32 = jnp.float32

KV_W = N_KV * HEAD_DIM
LORA_W = 2 * R_LORA + 2 * R_LORA + R_GATE
W_CONV, W_RKV, W_LORA = 0, 3 * D_MODEL, 6 * D_MODEL
W_Q = W_LORA + LORA_W
W_GATE = W_Q + D_MODEL + 2 * KV_W
N_PROJ_B = D_MODEL + 2 * KV_W + 3 * D_MODEL
B_Q, B_K, B_V, B_GATE = 0, D_MODEL, D_MODEL + KV_W, D_MODEL + 2 * KV_W
GATE_BLK = B_GATE
assert B_K % KV_W == 0 and 2 * GATE_BLK == 3 * D_MODEL and GATE_BLK % LANES == 0


def _cparams(sem):
    return pltpu.CompilerParams(dimension_semantics=sem, vmem_limit_bytes=VMEM_LIMIT)


def _dot(a, b):
    return jnp.dot(a.astype(BF16), b.astype(BF16), preferred_element_type=F32)


def _dot_nt(a, b):
    return lax.dot_general(a.astype(BF16), b.astype(BF16), (((1,), (1,)), ((), ())),
                           preferred_element_type=F32)


def _bmm(a, b):
    return jnp.einsum('pmk,pkn->pmn', a.astype(BF16), b.astype(BF16), preferred_element_type=F32)


def _bmm_nt(a, b):
    return jnp.einsum('pmk,pnk->pmn', a.astype(BF16), b.astype(BF16), preferred_element_type=F32)


def _bmm_tn(a, b):
    return jnp.einsum('pkm,pkn->pmn', a.astype(BF16), b.astype(BF16), preferred_element_type=F32)


def _sigmoid(x):
    return 0.5 + 0.5 * jnp.tanh(0.5 * x)


def _row_ids(i, tm):
    return i * tm + lax.broadcasted_iota(jnp.int32, (tm, 1), 0)


def _mod_row(mod_ref, k, is_ctx):
    return jnp.where(is_ctx, mod_ref[0, k:k + 1, :], mod_ref[1, k:k + 1, :])


def _neighbours(z, before, after, i, tm, ctx, total):
    start = i * tm
    keep_before = jnp.where((start == 0) | (start == ctx), 0.0, 1.0)
    keep_after = jnp.where((start + tm == ctx) | (start + tm == total), 0.0, 1.0)
    row = lax.broadcasted_iota(jnp.int32, (HALO, 1), 0)
    prev = pltpu.roll(z, 1, 0)
    nxt = pltpu.roll(z, tm - 1, 0)
    top = jnp.where(row == 0, before * keep_before, prev[:HALO])
    bottom = jnp.where(row == HALO - 1, after * keep_after, nxt[tm - HALO:])
    return (jnp.concatenate([top, prev[HALO:]], axis=0),
            jnp.concatenate([nxt[:tm - HALO], bottom], axis=0))


def _layer_weight(shape, layer):
    return pl.BlockSpec((None,) + shape, lambda *_: (layer,) + tuple(0 for _ in shape),
                        pipeline_mode=pl.Buffered(1))


def _halo_specs(bsz, rows, width, tm, total, tile0=0):
    per = tm // rows
    nblk = total // rows
    before = pl.BlockSpec((bsz, rows, width), lambda i: (0, jnp.maximum((i + tile0) * per - 1, 0), 0))
    after = pl.BlockSpec((bsz, rows, width), lambda i: (0, jnp.minimum((i + tile0 + 1) * per, nblk - 1), 0))
    return before, after


def _edge_rows(before_ref, after_ref):
    before = before_ref[...].astype(F32)
    return before[before.shape[0] - 1:], after_ref[...].astype(F32)[0:1]


def _ada_kernel(c_ref, w_ref, b_ref, o_ref):
    c = c_ref[...]
    o_ref[...] = _dot(c * _sigmoid(c), w_ref[...]) + b_ref[...]


def _ada(cvec, ada_w, ada_b):
    depth, d, n = ada_w.shape
    tn = 1536
    return pl.pallas_call(
        _ada_kernel,
        grid=(depth, n // tn),
        in_specs=[pl.BlockSpec((HALO, d), lambda l, j: (0, 0)),
                  pl.BlockSpec((None, d, tn), lambda l, j: (l, 0, j)),
                  pl.BlockSpec((None, 1, tn), lambda l, j: (l, 0, j))],
        out_specs=pl.BlockSpec((None, HALO, tn), lambda l, j: (l, 0, j)),
        out_shape=jax.ShapeDtypeStruct((depth, HALO, n), F32),
        compiler_params=_cparams(("arbitrary", "arbitrary")),
    )(cvec, ada_w, ada_b.reshape(depth, 1, n))


def _rope(z, tab):
    cos, sin = tab[:, :LANES], tab[:, LANES:]
    lane = lax.broadcasted_iota(jnp.int32, (1, LANES), 1)
    first = jnp.bitwise_and(lane, HEAD_DIM // 2 - 1) < (HEAD_DIM // 4)
    out = []
    for p in range(z.shape[1] // LANES):
        zp = z[:, LANES * p:LANES * (p + 1)]
        partner = jnp.where(first, pltpu.roll(zp, LANES - HEAD_DIM // 4, 1), pltpu.roll(zp, HEAD_DIM // 4, 1))
        out.append(zp * cos + partner * sin)
    return jnp.concatenate(out, axis=1)


def _tile_neighbours(u, tm, keep_before, keep_after):
    ext = tm + 2 * HALO
    row = lax.broadcasted_iota(jnp.int32, (HALO, 1), 0)
    prev = pltpu.roll(u, 1, 0)[HALO:HALO + tm]
    nxt = pltpu.roll(u, ext - 1, 0)[HALO:HALO + tm]
    top = jnp.where(row == 0, prev[:HALO] * keep_before, prev[:HALO])
    bottom = jnp.where(row == HALO - 1, nxt[tm - HALO:] * keep_after, nxt[tm - HALO:])
    return (jnp.concatenate([top, prev[HALO:]], axis=0), jnp.concatenate([nxt[:tm - HALO], bottom], axis=0))


def _proj_kernel(x_ref, before_ref, after_ref, mod_ref, g_ref, w_ref, tab_ref,
                 tri_ref, ones_ref, mu_ref, w0_ref, w2_ref, a0_ref, a2_ref, g2_ref, kk_ref, ka_ref, rk_ref,
                 cv_out, ob_ref, at_out, rt_out, bt_out, kt_out, gend_out, v_out, bonus_out, gg_out, *,
                 tm, ctx, total):
    i = pl.program_id(0)
    bsz = x_ref.shape[0]
    d = D_MODEL
    start = i * tm
    ext = tm + 2 * HALO
    is_ctx = (start - HALO + lax.broadcasted_iota(jnp.int32, (ext, 1), 0)) < ctx
    h_ext = []
    for b in range(bsz):
        xe = jnp.concatenate([before_ref[b], x_ref[b], after_ref[b]], axis=0)
        y = xe * lax.rsqrt(jnp.mean(xe * xe, axis=-1, keepdims=True) + NORM_EPS) * g_ref[...]
        mod_b = mod_ref.at[b]
        h_ext.append((y * (1.0 + _mod_row(mod_b, 1, is_ctx)) + _mod_row(mod_b, 0, is_ctx)).astype(BF16))
    h = jnp.concatenate([he[HALO:HALO + tm] for he in h_ext], axis=0)

    rkv = _dot(jnp.concatenate(h_ext, axis=0), w_ref[:, W_RKV:W_RKV + 3 * d])
    lora = _dot(h, w_ref[:, W_LORA:W_LORA + LORA_W])
    tab = tab_ref[...]
    keep_before = jnp.where((start == 0) | (start == ctx), 0.0, 1.0)
    keep_after = jnp.where((start + tm == ctx) | (start + tm == total), 0.0, 1.0)

    def tokens(b):
        rkv_b = rkv[b * ext:(b + 1) * ext]
        prev, nxt = _tile_neighbours(rkv_b, tm, keep_before, keep_after)
        _rwkv_tokens(
            prev, rkv_b[HALO:HALO + tm], nxt, lora[b * tm:(b + 1) * tm],
            tri_ref, ones_ref, mu_ref, w0_ref, w2_ref, a0_ref, a2_ref, g2_ref, kk_ref, ka_ref, rk_ref,
            at_out.at[:, b], rt_out.at[:, b], bt_out.at[:, b], kt_out.at[:, b], gend_out.at[:, b],
            v_out.at[b], bonus_out.at[b], gg_out.at[b], tm)

    qkv = _dot(h, w_ref[:, W_Q:W_GATE])
    for b in range(bsz):
        rows = slice(b * tm, (b + 1) * tm)
        rot = _rope(qkv[rows, :B_V - B_Q], tab)
        ob_ref[b, :, B_Q:B_K] = (rot[:, :B_K - B_Q] * ATT_SCALE).astype(ob_ref.dtype)
        ob_ref[b, :, B_K:B_V] = rot[:, B_K - B_Q:].astype(ob_ref.dtype)
        ob_ref[b, :, B_V:B_GATE] = qkv[rows, B_V - B_Q:].astype(ob_ref.dtype)
    gates = _sigmoid(_dot(h, w_ref[:, W_GATE:]))
    ob_ref[:, :, B_GATE:] = gates.reshape(bsz, tm, 3 * d).astype(ob_ref.dtype)
    cv_out[...] = _dot(h, w_ref[:, W_CONV:W_CONV + 3 * d]).reshape(bsz, tm, 3 * d).astype(cv_out.dtype)
    for b in range(bsz):
        tokens(b)


def _proj(x, mod, g, w, layer, rope_tab, tri, head_ones, mu, w0, w2bd, a0, a2bd, g2, k_k, k_a, r_k, *, ctx, tm):
    bsz, total, d = x.shape
    assert w.shape[2] == W_GATE + 3 * d
    kern = functools.partial(_proj_kernel, tm=tm, ctx=ctx, total=total)
    before, after = _halo_specs(bsz, HALO, d, tm, total)
    once = lambda shp: pl.BlockSpec(shp, lambda i: tuple(0 for _ in shp), pipeline_mode=pl.Buffered(1))
    row = lambda n: pl.BlockSpec((bsz, tm, n), lambda i: (0, i, 0))
    pm = pl.BlockSpec((bsz, N_PAIR, tm, LANES), lambda i: (0, 0, i, 0))
    pm2 = pl.BlockSpec((2, bsz, N_PAIR, tm, LANES), lambda i: (0, 0, 0, i, 0))
    ge = pl.BlockSpec((2, bsz, tm // CHUNK, N_PAIR, LANES), lambda i: (0, 0, i, 0, 0))
    sds = jax.ShapeDtypeStruct((bsz, N_PAIR, total, LANES), BF16)
    sds2 = jax.ShapeDtypeStruct((2, bsz, N_PAIR, total, LANES), BF16)
    sds_ge = jax.ShapeDtypeStruct((2, bsz, total // CHUNK, N_PAIR, LANES), F32)
    return pl.pallas_call(
        kern,
        grid=(total // tm,),
        in_specs=[row(d), before, after, pl.BlockSpec((bsz, 2, 6, d), lambda i: (0, 0, 0, 0)),
                  once((1, d)), _layer_weight((d, W_GATE + 3 * d), layer),
                  pl.BlockSpec((tm, 2 * LANES), lambda i: (i, 0)),
                  once((2, tm, tm)), once((LANES, LANES)), once((2, 3 * d)), once((1, 2 * d)), once((LANES, 2 * d)),
                  once((1, 2 * d)), once((LANES, 2 * d)), once((R_GATE, d)), once((1, d)), once((1, d)),
                  once((1, d))],
        out_specs=[row(3 * d), row(N_PROJ_B), pm2, pm2, pm2, pm2, ge, pm, pm, pm],
        out_shape=[jax.ShapeDtypeStruct((bsz, total, 3 * d), BF16),
                   jax.ShapeDtypeStruct((bsz, total, N_PROJ_B), BF16),
                   sds2, sds2, sds2, sds2, sds_ge, sds, sds, sds],
        compiler_params=_cparams(("arbitrary",)),
    )(x, x, x, mod, g, w, rope_tab, tri, head_ones, mu, w0, w2bd, a0, a2bd, g2, k_k, k_a, r_k)


def _head_sums(z, head_ones):
    return jnp.dot(z.astype(BF16), head_ones, preferred_element_type=F32)


def _head_sums_exact(z, lane_lo):
    s_lo = jnp.sum(jnp.where(lane_lo, z, 0.0), axis=-1, keepdims=True)
    s_hi = jnp.sum(jnp.where(lane_lo, 0.0, z), axis=-1, keepdims=True)
    return jnp.where(lane_lo, s_lo, s_hi)


def _chunk_cumsum(tri, z):
    hi = z.astype(BF16)
    low = (z - hi.astype(F32)).astype(BF16)
    dot = functools.partial(jnp.dot, preferred_element_type=F32)
    return dot(tri, low) + dot(tri, hi)


def _rwkv_tokens(prev, x, nxt, lo, tri_ref, ones_ref, mu_ref, w0_ref, w2_ref, a0_ref, a2_ref, g2_ref, kk_ref,
                 ka_ref, rk_ref, at_out, rt_out, bt_out, kt_out, gend_out, v_out, bonus_out, g_out, tm):
    mu_prev = mu_ref[0:1, :]
    mu_next = mu_ref[1:2, :]
    s = prev * mu_prev + x * (1.0 - mu_prev - mu_next) + nxt * mu_next
    d = D_MODEL
    r, k, v = s[:, :d], s[:, d:2 * d], s[:, 2 * d:]

    w_pre = w0_ref[...] + _dot(jnp.tanh(lo[:, :LANES]), w2_ref[...])
    lw = -DECAY_SCALE * _sigmoid(w_pre)
    a = _sigmoid(a0_ref[...] + _dot(lo[:, LANES:2 * LANES], a2_ref[...]))
    g = _dot(_sigmoid(lo[:, 2 * LANES:]), g2_ref[...])
    kk = k * kk_ref[...]
    ka = ka_ref[...]
    rk = rk_ref[...]
    cl = [_chunk_cumsum(tri_ref[dr], lw[:, d * dr:d * (dr + 1)]) for dr in range(2)]

    head_ones = ones_ref[...]
    for p in range(N_PAIR):
        sl = slice(LANES * p, LANES * (p + 1))
        kkp = kk[:, sl]
        kn = kkp * lax.rsqrt(_head_sums(kkp * kkp, head_ones) + KK_EPS)
        rp, kp, vp = r[:, sl], k[:, sl], v[:, sl]
        bonus = jnp.zeros_like(rp)
        for dr in range(2):
            dsl = slice(d * dr + LANES * p, d * dr + LANES * (p + 1))
            ad = a[:, dsl]
            kd = kp * (1.0 + (ad - 1.0) * ka[:, sl])
            bonus = bonus + _head_sums(rp * kd * rk[:, sl], head_ones) * vp
            cl_incl = cl[dr][:, sl]
            inv_g = jnp.exp(-cl_incl)
            at_out[dr, p] = (-kn * jnp.exp(cl_incl - lw[:, dsl])).astype(BF16)
            rt_out[dr, p] = (rp * jnp.exp(cl_incl)).astype(BF16)
            bt_out[dr, p] = (kn * ad * inv_g).astype(BF16)
            kt_out[dr, p] = (kd * inv_g).astype(BF16)
            for q in range(tm // CHUNK):
                end = CHUNK * q + (CHUNK - 1 if dr == 0 else 0)
                gend_out[dr, q, p:p + 1, :] = jnp.exp(cl_incl[end:end + 1, :])
        v_out[p] = vp.astype(BF16)
        bonus_out[p] = bonus.astype(BF16)
        g_out[p] = g[:, sl].astype(BF16)


M_STRICT, M_INCL, M_EYE, M_LEVEL0 = 0, 1, 2, 3
N_MASKS = M_LEVEL0 + 1 + N_LEVELS


def _scan_masks(tm):
    c = CHUNK
    t = np.arange(c)[:, None]
    s = np.arange(c)[None, :]
    cum = np.zeros((2, tm, tm), np.float32)
    msk = np.zeros((2, N_MASKS, c, 2 * c), np.float32)
    for dr in range(2):
        earlier = (s < t) if dr == 0 else (s > t)
        cum[dr] = np.kron(np.eye(tm // c), earlier | (s == t))
        planes = [earlier, earlier | (s == t), s == t]
        m = 1
        while m < c:
            same = (t // (2 * m)) == (s // (2 * m))
            t_late = (t % (2 * m)) >= m
            s_late = (s % (2 * m)) >= m
            planes.append(same & (t_late & ~s_late if dr == 0 else ~t_late & s_late))
            m *= 2
        for q, plane in enumerate(planes):
            msk[dr, q] = np.concatenate([plane, plane], axis=1)
    bdm = np.kron(np.eye(2, dtype=np.float32), np.ones((c, c), np.float32))
    return jnp.asarray(cum, BF16), jnp.asarray(msk), jnp.asarray(bdm)


SCAN_GROUP = 4


def _scan_kernel(msk_ref, bdm_ref, at_ref, rt_ref, bt_ref, kt_ref, gend_ref, v_ref, y_ref, s_ref, *, reverse):
    c = CHUNK
    bsz = v_ref.shape[0]
    n = bsz * N_PAIR
    order = tuple(reversed(range(SCAN_GROUP))) if reverse else tuple(range(SCAN_GROUP))

    @pl.when(pl.program_id(0) == 0)
    def _():
        s_ref[...] = jnp.zeros_like(s_ref)

    bdm = bdm_ref[...]
    bdm_bf = bdm.astype(BF16)
    m_strict = msk_ref[M_STRICT]
    m_incl = msk_ref[M_INCL]

    def load(ref):
        z = ref[...].reshape(n, SCAN_GROUP * c, LANES)
        return jnp.concatenate([z[:, c * k:c * (k + 1)] for k in range(SCAN_GROUP)], axis=0)

    def stack(z):
        z = z.astype(BF16)
        return jnp.concatenate([z, z], axis=1) * bdm_bf

    bt, kt, v = load(bt_ref), load(kt_ref), load(v_ref)
    ar = jnp.concatenate([load(at_ref), load(rt_ref)], axis=1)
    gram = _bmm_nt(ar, jnp.concatenate([stack(bt), stack(kt)], axis=1))
    g_ab = gram[:, :c, :2 * c]
    from_v = jnp.concatenate([gram[:, :c, 2 * c:] * m_strict, gram[:, c:, 2 * c:] * m_incl], axis=1)
    from_v = _bmm(from_v, stack(v))
    y_from_u = gram[:, c:, :2 * c] * m_incl

    tri = msk_ref[M_EYE] + g_ab * msk_ref[M_LEVEL0]
    for lvl in range(1, N_LEVELS + 1):
        w = _bmm(g_ab * msk_ref[M_LEVEL0 + lvl], stack(tri))
        tri = tri + _bmm(tri, stack(w))

    state = s_ref[...]
    ys = [None] * SCAN_GROUP
    for k in order:
        sel = slice(n * k, n * (k + 1))
        g_end = jnp.stack([gend_ref[q // N_PAIR, k, q % N_PAIR:q % N_PAIR + 1, :] for q in range(n)], axis=0)
        partial = _bmm_nt(ar[sel], state) + from_v[sel]
        u = _bmm(tri[sel], stack(partial[:, :c]))
        ys[k] = partial[:, c:] + _bmm(y_from_u[sel], stack(u))
        upd = _bmm_tn(jnp.concatenate([u.astype(BF16), v[sel]], axis=1), jnp.concatenate([bt[sel], kt[sel]], axis=1))
        state = (state + upd * bdm) * g_end
    s_ref[...] = state
    y_ref[...] = jnp.concatenate(ys, axis=1).reshape(bsz, N_PAIR, SCAN_GROUP * c, LANES).astype(y_ref.dtype)


def _scan(at, rt, bt, kt, gend, v, masks, *, ctx, direction):
    bsz, _, total, _ = v.shape
    rows = SCAN_GROUP * CHUNK
    steps, ctx_steps = total // rows, ctx // rows
    assert steps * rows == total and ctx_steps * rows == ctx
    _, msk, bdm = masks

    def blk(j):
        return j if direction == 0 else jnp.where(j < ctx_steps, ctx_steps - 1 - j, steps + ctx_steps - 1 - j)

    shared = pl.BlockSpec((bsz, N_PAIR, rows, LANES), lambda j: (0, 0, blk(j), 0))
    per_dir = pl.BlockSpec((None, bsz, N_PAIR, rows, LANES), lambda j: (direction, 0, 0, blk(j), 0))
    ends = pl.BlockSpec((None, bsz, SCAN_GROUP, N_PAIR, LANES), lambda j: (direction, 0, blk(j), 0, 0))
    return pl.pallas_call(
        functools.partial(_scan_kernel, reverse=direction == 1),
        grid=(steps,),
        in_specs=[pl.BlockSpec((None, N_MASKS, CHUNK, 2 * CHUNK), lambda j: (direction, 0, 0, 0)),
                  pl.BlockSpec((2 * CHUNK, 2 * CHUNK), lambda j: (0, 0)),
                  per_dir, per_dir, per_dir, per_dir, ends, shared],
        out_specs=shared,
        out_shape=jax.ShapeDtypeStruct((bsz, N_PAIR, total, LANES), BF16),
        scratch_shapes=[pltpu.VMEM((bsz * N_PAIR, 2 * HEAD_DIM, 2 * HEAD_DIM), F32)],
        compiler_params=_cparams(("arbitrary",)),
    )(msk, bdm, at, rt, bt, kt, gend, v)


def _attn_kernel(sink_ref, q_ref, kvp_ref, kvc_ref, kvn_ref, kvx_ref, o_ref, *, n_ctx_blk, n_blk):
    i = pl.program_id(1)
    blk = ATT_BLOCK
    kv_refs = (kvp_ref, kvc_ref, kvn_ref, kvx_ref)
    q = q_ref[...].astype(F32)
    k_all = jnp.concatenate([ref[:, :KV_W] for ref in kv_refs], axis=0)
    v_all = jnp.concatenate([ref[:, KV_W:] for ref in kv_refs], axis=0)

    qi = lax.broadcasted_iota(jnp.int32, (blk, 3 * blk), 0)
    si = lax.broadcasted_iota(jnp.int32, (blk, 3 * blk), 1)
    rel = si - blk - qi
    key_blk = i - 1 + jnp.right_shift(si, LOG2_ATT_BLOCK)
    ok = (jnp.where(jnp.abs(rel) <= WINDOW, 1, 0) * jnp.where(key_blk >= n_ctx_blk, 1, 0)
          * jnp.where(key_blk < n_blk, 1, 0) * jnp.where(i >= n_ctx_blk, 1, 0))
    bias = jnp.where(ok > 0, 0.0, NEG_INF)

    lane_lo = lax.broadcasted_iota(jnp.int32, (1, LANES), 1) < HEAD_DIM
    stack = ATT_STACK
    outs = [None] * N_HEADS
    bias_g = jnp.concatenate([bias] * stack, axis=0)

    for h0 in range(0, N_HEADS, stack):
        kvh = h0 // (N_HEADS // N_KV)
        kv_slab = slice(LANES * (kvh // 2), LANES * (kvh // 2 + 1))
        kv_lo = kvh % 2 == 0
        rows = []
        for h in range(h0, h0 + stack):
            qp = q[:, LANES * (h // 2):LANES * (h // 2 + 1)]
            if (h % 2 == 0) != kv_lo:
                qp = pltpu.roll(qp, HEAD_DIM, 1)
            rows.append(jnp.where(lane_lo, qp, 0.0) if kv_lo else jnp.where(lane_lo, 0.0, qp))
        s = _dot_nt(jnp.concatenate(rows, axis=0), k_all[:, kv_slab])
        s = jnp.concatenate([s[:, :3 * blk] + bias_g, s[:, 3 * blk:]], axis=1)
        sink = jnp.concatenate([jnp.full((blk, 1), sink_ref[h], F32) for h in range(h0, h0 + stack)], axis=0)
        m = jnp.maximum(jnp.max(s, axis=-1, keepdims=True), sink)
        e = jnp.exp(s - m)
        denom = jnp.sum(e, axis=-1, keepdims=True) + jnp.exp(sink - m)
        o = _dot(e, v_all[:, kv_slab]) / denom
        for j, h in enumerate(range(h0, h0 + stack)):
            oh = o[j * blk:(j + 1) * blk]
            if (h % 2 == 0) != kv_lo:
                oh = pltpu.roll(oh, HEAD_DIM, 1)
            outs[h] = oh
    o_ref[...] = jnp.concatenate(
        [jnp.where(lane_lo, outs[2 * p], outs[2 * p + 1]) for p in range(N_PAIR)], axis=1).astype(o_ref.dtype)


def _attention(qkv, sink, *, ctx):
    bsz, total, _ = qkv.shape
    blk = ATT_BLOCK
    n_blk, n_ctx_blk = total // blk, ctx // blk
    assert B_V == B_K + KV_W and B_K % (2 * KV_W) == 0
    kv_col = B_K // (2 * KV_W)
    kern = functools.partial(_attn_kernel, n_ctx_blk=n_ctx_blk, n_blk=n_blk)
    prev_i = lambda i: jnp.maximum(i - 1, 0)
    next_i = lambda i: jnp.minimum(i + 1, n_blk - 1)
    same = lambda i: i

    def kv(row):
        return pl.BlockSpec((None, blk, 2 * KV_W), lambda b, i: (b, row(i), kv_col))

    return pl.pallas_call(
        kern,
        grid=(bsz, n_blk),
        in_specs=[pl.BlockSpec(memory_space=pltpu.SMEM),
                  pl.BlockSpec((None, blk, D_MODEL), lambda b, i: (b, i, B_Q // D_MODEL)),
                  kv(prev_i), kv(same), kv(next_i),
                  pl.BlockSpec((None, ctx, 2 * KV_W), lambda b, i: (b, 0, kv_col))],
        out_specs=pl.BlockSpec((None, blk, D_MODEL), lambda b, i: (b, i, 0)),
        out_shape=jax.ShapeDtypeStruct((bsz, total, D_MODEL), BF16),
        compiler_params=_cparams(("arbitrary", "arbitrary")),
    )(sink, qkv, qkv, qkv, qkv, qkv)


def _merge_kernel(x_ref, mod_ref, cv_ref, before_ref, after_ref, gate_lo_ref, gate_hi_ref, att_ref, yf_ref, yb_ref,
                  bonus_ref, g_ref,
                  cw_ref, lng_ref, lnb_ref, wa_ref, wb_ref, wc_ref, wo_ref, o_ref, *, tm, ctx, total):
    i = pl.program_id(0)
    bsz = x_ref.shape[0]
    d = D_MODEL
    cw = cw_ref[...]
    lng = lng_ref[...]
    lnb = lnb_ref[...]
    lane_lo = lax.broadcasted_iota(jnp.int32, (1, LANES), 1) < HEAD_DIM
    rows = bsz * tm
    from_att = _dot(att_ref[...].reshape(rows, d), wc_ref[...])
    convs, rwkvs = [], []
    for b in range(bsz):
        cv = cv_ref[b].astype(F32)
        z = cv[:, d:2 * d] * cv[:, 2 * d:]
        before, after = _edge_rows(before_ref.at[b], after_ref.at[b])
        prev, nxt = _neighbours(z, before[:, d:2 * d] * before[:, 2 * d:], after[:, d:2 * d] * after[:, 2 * d:],
                                i, tm, ctx, total)
        convs.append((cv[:, :d] * (prev * cw[0:1] + z * cw[1:2] + nxt * cw[2:3])).astype(BF16))
    from_conv = _dot(jnp.concatenate(convs, axis=0), wa_ref[...])
    for b in range(bsz):
        slabs = []
        for p in range(N_PAIR):
            sl = slice(LANES * p, LANES * (p + 1))
            y = yf_ref[b, p].astype(F32) + yb_ref[b, p].astype(F32)
            mean = _head_sums_exact(y, lane_lo) * (1.0 / HEAD_DIM)
            yc = y - mean
            var = _head_sums_exact(yc * yc, lane_lo) * (1.0 / HEAD_DIM)
            gn = yc * lax.rsqrt(var + GN_EPS) * lng[:, sl] + lnb[:, sl]
            slabs.append(((gn + bonus_ref[b, p]) * g_ref[b, p]).astype(BF16))
        rwkvs.append(jnp.concatenate(slabs, axis=1))

    gates = jnp.concatenate([gate_lo_ref[...].reshape(rows, GATE_BLK), gate_hi_ref[...].reshape(rows, GATE_BLK)],
                            axis=1).astype(F32)
    m = (gates[:, :d] * from_conv
         + gates[:, d:2 * d] * _dot(jnp.concatenate(rwkvs, axis=0), wb_ref[...])
         + gates[:, 2 * d:] * from_att)
    out = _dot(m, wo_ref[...])
    is_ctx = _row_ids(i, tm) < ctx
    for b in range(bsz):
        o_ref[b] = x_ref[b] + _mod_row(mod_ref.at[b], 2, is_ctx) * out[b * tm:(b + 1) * tm]


def _merge(x, mod, cv, proj_b, att, y, bonus, g, conv_w, ln_g, ln_b, wa, wb, wc, wo, layer, *, ctx, tm):
    bsz, total, d = x.shape
    kern = functools.partial(_merge_kernel, tm=tm, ctx=ctx, total=total)
    col_cv, col_gate = 0, B_GATE // GATE_BLK
    before, after = _halo_specs(bsz, HALO_BF16, 3 * d, tm, total)
    row = lambda w, col=0: pl.BlockSpec((bsz, tm, w), lambda i: (0, i, col))
    once = lambda shp: pl.BlockSpec(shp, lambda i: tuple(0 for _ in shp), pipeline_mode=pl.Buffered(1))
    pm = pl.BlockSpec((bsz, N_PAIR, tm, LANES), lambda i: (0, 0, i, 0))
    return pl.pallas_call(
        kern,
        grid=(total // tm,),
        in_specs=[row(d), pl.BlockSpec((bsz, 2, 6, d), lambda i: (0, 0, 0, 0)),
                  row(3 * d, col_cv), before, after, row(GATE_BLK, col_gate), row(GATE_BLK, col_gate + 1),
                  row(d), pm, pm, pm, pm,
                  once((3, d)), once((1, d)), once((1, d)),
                  _layer_weight((d, d), layer), _layer_weight((d, d), layer), _layer_weight((d, d), layer),
                  _layer_weight((d, d), layer)],
        out_specs=row(d),
        out_shape=jax.ShapeDtypeStruct((bsz, total, d), F32),
        compiler_params=_cparams(("arbitrary",)),
    )(x, mod, cv, cv, cv, proj_b, proj_b, att, y[0], y[1], bonus, g, conv_w, ln_g, ln_b,
      wa, wb, wc, wo)


def _ffn_kernel(x_ref, before_ref, after_ref, mod_ref, g_ref, wu_ref, cw_ref, wd_ref, fin_ref, o_ref, *,
                tm, ctx, total, tile0):
    i = pl.program_id(0) + tile0
    bsz = x_ref.shape[0]
    start = i * tm
    ext = tm + 2 * HALO
    is_ctx = (start - HALO + lax.broadcasted_iota(jnp.int32, (ext, 1), 0)) < ctx
    tile_is_ctx = _row_ids(i, tm) < ctx

    keep_before = jnp.where((start == 0) | (start == ctx), 0.0, 1.0)
    keep_after = jnp.where((start + tm == ctx) | (start + tm == total), 0.0, 1.0)

    def up(b):
        xe = jnp.concatenate([before_ref[b], x_ref[b], after_ref[b]], axis=0)
        y = xe * lax.rsqrt(jnp.mean(xe * xe, axis=-1, keepdims=True) + NORM_EPS) * g_ref[...]
        mod_b = mod_ref.at[b]
        h = (y * (1.0 + _mod_row(mod_b, 4, is_ctx)) + _mod_row(mod_b, 3, is_ctx)).astype(BF16)
        return _dot(h, wu_ref[:, :D_FF]), _dot(h, wu_ref[:, D_FF:])

    def conv(u, cw):
        prev, nxt = _tile_neighbours(u, tm, keep_before, keep_after)
        return prev * cw[0:1] + u[HALO:HALO + tm] * cw[1:2] + nxt * cw[2:3]

    u_next = up(0)
    for b in range(bsz):
        u_gate, u_val = u_next
        if b + 1 < bsz:
            u_next = up(b + 1)
        ug = conv(u_gate, cw_ref[:, :D_FF])
        uv = conv(u_val, cw_ref[:, D_FF:])
        out = x_ref[b] + _mod_row(mod_ref.at[b], 5, tile_is_ctx) * _dot(ug * _sigmoid(ug) * uv, wd_ref[...])
        if fin_ref is not None:
            out = out * lax.rsqrt(jnp.mean(out * out, axis=-1, keepdims=True) + NORM_EPS) * fin_ref[...]
        o_ref[b] = out


def _ffn_mid_kernel(x_ref, before_ref, after_ref, mod_ref, g_ref, wu_ref, cw_ref, wd_ref, o_ref, **kw):
    _ffn_kernel(x_ref, before_ref, after_ref, mod_ref, g_ref, wu_ref, cw_ref, wd_ref, None, o_ref, **kw)


def _ffn(x, mod, g, wu, conv_w, wd, layer, final_g, *, ctx, tm):
    bsz, total, d = x.shape
    tile0 = 0 if final_g is None else ctx // tm
    before, after = _halo_specs(bsz, HALO, d, tm, total, tile0)
    row = pl.BlockSpec((bsz, tm, d), lambda i: (0, i + tile0, 0))
    once = lambda shp: pl.BlockSpec(shp, lambda i: tuple(0 for _ in shp), pipeline_mode=pl.Buffered(1))
    in_specs = [row, before, after, pl.BlockSpec((bsz, 2, 6, d), lambda i: (0, 0, 0, 0)),
                once((1, d)), _layer_weight((d, 2 * D_FF), layer), once((3, 2 * D_FF)),
                _layer_weight((D_FF, d), layer)]
    args = [x, x, x, mod, g, wu, conv_w, wd]
    kw = dict(tm=tm, ctx=ctx, total=total, tile0=tile0)
    if final_g is None:
        kern = functools.partial(_ffn_mid_kernel, **kw)
    else:
        kern = functools.partial(_ffn_kernel, **kw)
        in_specs.append(once((1, d)))
        args.append(final_g)
    out_spec = pl.BlockSpec((bsz, tm, d), lambda i: (0, i, 0))
    out_rows = total - tile0 * tm
    return pl.pallas_call(
        kern,
        grid=(total // tm - tile0,),
        in_specs=in_specs,
        out_specs=out_spec,
        out_shape=jax.ShapeDtypeStruct((bsz, out_rows, d), F32),
        compiler_params=_cparams(("arbitrary",)),
    )(*args)


def _rope_tables(ctx, seq):
    rows = seq // GRID_W
    t_row = jnp.broadcast_to(jnp.arange(rows)[:, None], (rows, GRID_W)).reshape(-1).astype(F32)
    t_col = jnp.broadcast_to(jnp.arange(GRID_W)[None, :], (rows, GRID_W)).reshape(-1).astype(F32)
    n_freq = HEAD_DIM // 4
    inv = ROPE_THETA ** (-jnp.arange(n_freq, dtype=F32) / n_freq)
    ar, ac = t_row[:, None] * inv, t_col[:, None] * inv
    cos = jnp.concatenate([jnp.cos(ar), jnp.cos(ar), jnp.cos(ac), jnp.cos(ac)], axis=1)
    sin = jnp.concatenate([-jnp.sin(ar), jnp.sin(ar), -jnp.sin(ac), jnp.sin(ac)], axis=1)
    cos = jnp.concatenate([jnp.ones((ctx, HEAD_DIM), F32), cos], axis=0)
    sin = jnp.concatenate([jnp.zeros((ctx, HEAD_DIM), F32), sin], axis=0)
    return jnp.concatenate([jnp.tile(cos, (1, LANES // HEAD_DIM)), jnp.tile(sin, (1, LANES // HEAD_DIM))], axis=1)


def _block_diag2(w):
    z = jnp.zeros_like(w[0])
    return jnp.concatenate([jnp.concatenate([w[0], z], axis=1), jnp.concatenate([z, w[1]], axis=1)], axis=0)


def kernel(x, c, ctx, c_ctx, ada_w, ada_b, norm1_g, w_in, conv_a_w, a_out_w, rwkv_mu, rwkv_w0, rwkv_w2,
           rwkv_a0, rwkv_a2, rwkv_g2, rwkv_k_k, rwkv_k_a, rwkv_r_k, rwkv_ln_g, rwkv_ln_b, rwkv_out_w,
           attn_sink, attn_out_w, w_o, norm2_g, ffn_up, ffn_conv, ffn_down, final_norm_g):
    bsz, seq, d = x.shape
    n_ctx = ctx.shape[1]
    depth = ada_w.shape[0]
    total = n_ctx + seq
    assert d == D_MODEL and seq % ATT_BLOCK == 0 and n_ctx % ATT_BLOCK == 0 and bsz + 1 <= HALO
    tm = 256
    assert total % tm == 0 and n_ctx % tm == 0

    cvec = jnp.zeros((HALO, d), F32).at[:bsz].set(c).at[bsz].set(c_ctx)
    mods = _ada(cvec, ada_w, ada_b)
    mods = mods.reshape(depth, HALO, 6, d)
    mods = jnp.stack([jnp.broadcast_to(mods[:, bsz:bsz + 1], (depth, bsz, 6, d)), mods[:, :bsz]], axis=2)

    rope_tab = _rope_tables(n_ctx, seq)
    tm_proj = tm // 2
    masks = _scan_masks(tm_proj)
    head_ones = masks[2].astype(BF16)
    xa = jnp.concatenate([ctx, x], axis=1)
    w_in, a_out_w, rwkv_out_w, attn_out_w, w_o, ffn_up, ffn_down = (
        w.astype(BF16) for w in (w_in, a_out_w, rwkv_out_w, attn_out_w, w_o, ffn_up, ffn_down))

    for l in range(depth):
        mod = mods[l]
        cv, proj_b, at, rt, bt, kt, gend, v, bonus, g = _proj(
            xa, mod, norm1_g[l][None], w_in, l, rope_tab, masks[0], head_ones, rwkv_mu[l],
            rwkv_w0[l].reshape(1, 2 * d), _block_diag2(rwkv_w2[l]).astype(BF16),
            rwkv_a0[l].reshape(1, 2 * d), _block_diag2(rwkv_a2[l]).astype(BF16), rwkv_g2[l].astype(BF16),
            rwkv_k_k[l][None], rwkv_k_a[l][None], rwkv_r_k[l].reshape(1, d), ctx=n_ctx, tm=tm_proj)
        y = [_scan(at, rt, bt, kt, gend, v, masks, ctx=n_ctx, direction=dr) for dr in range(2)]
        att = _attention(proj_b, attn_sink[l], ctx=n_ctx)
        xa = _merge(xa, mod, cv, proj_b, att, y, bonus, g, conv_a_w[l], rwkv_ln_g[l][None], rwkv_ln_b[l][None],
                    a_out_w, rwkv_out_w, attn_out_w, w_o, l, ctx=n_ctx, tm=tm)
        xa = _ffn(xa, mod, norm2_g[l][None], ffn_up, ffn_conv[l], ffn_down, l,
                  final_norm_g[None] if l == depth - 1 else None, ctx=n_ctx, tm=tm)
    return xa
```
